```python
import jax, jax.numpy as jnp
from jax import lax
import numpy as np

D_MODEL = 2048
BATCH = 8
SEQ = 4096
DEPTH = 4

D_MIX = D_MODEL
HEAD_DIM = 128
D_ATTN = D_MIX // 2
D_GMLP = D_MIX - D_ATTN
N_HEADS = D_ATTN // HEAD_DIM
GMLP_GROUP = 128
N_GMLP_GROUPS = D_GMLP // GMLP_GROUP
CHUNK = 128
DILATED_PATTERNS = ((128, 1), (512, 4), (2048, 16))
BLK = 64
ROPE_THETA = 10000.0
D_FF = -(-(8 * D_MODEL) // (3 * 256)) * 256
D_IN = 3 * D_ATTN + 2 * D_GMLP
EPS = 1e-6
NEG = -1e30

kernel_name = "hybrid_dilated_attn_gmlp_encoder"


def rmsnorm(x, g):
    xf = x.astype(jnp.float32)
    y = xf * lax.rsqrt(jnp.mean(xf * xf, axis=-1, keepdims=True) + EPS)
    return (y * g.astype(jnp.float32)).astype(x.dtype)


def layernorm(x, g):
    xf = x.astype(jnp.float32)
    mu = jnp.mean(xf, axis=-1, keepdims=True)
    xc = xf - mu
    y = xc * lax.rsqrt(jnp.mean(xc * xc, axis=-1, keepdims=True) + EPS)
    return (y * g.astype(jnp.float32)).astype(x.dtype)


def rope_tables(seq):
    pos = jnp.arange(seq, dtype=jnp.float32)
    inv = ROPE_THETA ** (-jnp.arange(0, HEAD_DIM, 2, dtype=jnp.float32) / HEAD_DIM)
    ang = pos[:, None] * inv[None, :]
    return jnp.cos(ang), jnp.sin(ang)


def apply_rope(t, cos, sin):
    tf = t.astype(jnp.float32)
    t1, t2 = jnp.split(tf, 2, axis=-1)
    c = cos[None, :, None, :]
    s = sin[None, :, None, :]
    out = jnp.concatenate([t1 * c - t2 * s, t1 * s + t2 * c], axis=-1)
    return out.astype(t.dtype)


def _band_blocks(t, nb):
    return jnp.concatenate([t[:, :, 0:nb], t[:, :, 1:nb + 1], t[:, :, 2:nb + 2]], axis=3)


def dilated_window_branch(q, k, v, window, dilation):
    B, S, H, Dh = q.shape
    n_side = window // (2 * dilation)
    span = dilation * BLK
    s_pad = -(-S // span) * span
    L = s_pad // dilation
    nb = L // BLK

    def to_sub(t):
        t = jnp.pad(t, ((0, 0), (0, s_pad - S), (0, 0), (0, 0)))
        return t.reshape(B, L, dilation, H, Dh).transpose(0, 2, 1, 3, 4)

    qs, ks, vs = to_sub(q), to_sub(k), to_sub(v)
    valid = (jnp.arange(s_pad) < S).reshape(L, dilation).T

    qb = qs.reshape(B, dilation, nb, BLK, H, Dh)
    pad_kv = ((0, 0), (0, 0), (BLK, BLK), (0, 0), (0, 0))
    kb = _band_blocks(jnp.pad(ks, pad_kv).reshape(B, dilation, nb + 2, BLK, H, Dh), nb)
    vb = _band_blocks(jnp.pad(vs, pad_kv).reshape(B, dilation, nb + 2, BLK, H, Dh), nb)
    vk = jnp.pad(valid, ((0, 0), (BLK, BLK))).reshape(dilation, nb + 2, BLK)
    vk = jnp.concatenate([vk[:, 0:nb], vk[:, 1:nb + 1], vk[:, 2:nb + 2]], axis=2)

    scale = HEAD_DIM ** -0.5
    s = jnp.einsum('brnqhd,brnkhd->brnhqk', qb, kb,
                   preferred_element_type=jnp.float32) * scale
    rel = jnp.arange(3 * BLK)[None, :] - BLK - jnp.arange(BLK)[:, None]
    band = jnp.abs(rel) <= n_side
    mask = band[None, None, None, None] & vk[None, :, :, None, None, :]
    s = jnp.where(mask, s, NEG)
    m = jnp.max(s, axis=-1, keepdims=True)
    p = jnp.exp(s - m)
    denom = jnp.sum(p, axis=-1)
    o = jnp.einsum('brnhqk,brnkhd->brnqhd', p, vb.astype(jnp.float32))
    o = o / jnp.moveaxis(denom, -1, -2)[..., None]
    lse = jnp.moveaxis(m[..., 0] + jnp.log(denom), -1, -2)

    o = o.reshape(B, dilation, L, H, Dh).transpose(0, 2, 1, 3, 4).reshape(B, s_pad, H, Dh)[:, :S]
    lse = lse.reshape(B, dilation, L, H).transpose(0, 2, 1, 3).reshape(B, s_pad, H)[:, :S]
    return o, lse


def dilated_attention(q, k, v):
    outs, lses = [], []
    for window, dilation in DILATED_PATTERNS:
        o, l = dilated_window_branch(q, k, v, window, dilation)
        outs.append(o)
        lses.append(l)
    w = jax.nn.softmax(jnp.stack(lses, axis=0), axis=0)
    o = jnp.sum(w[..., None] * jnp.stack(outs, axis=0), axis=0)
    B, S = q.shape[0], q.shape[1]
    return o.reshape(B, S, D_ATTN).astype(q.dtype)


def chunked_spatial_gating(uv, ln_g, w_s, b_s):
    uv = jax.nn.gelu(uv, approximate=False)
    u, v = jnp.split(uv, 2, axis=-1)
    v = layernorm(v, ln_g)
    B, S, _ = v.shape
    vg = v.reshape(B, S // CHUNK, CHUNK, N_GMLP_GROUPS, GMLP_GROUP)
    mixed = jnp.einsum('gpq,bnqgc->bnpgc', w_s, vg) + b_s.T[None, None, :, :, None]
    return u * mixed.reshape(B, S, D_GMLP)


def setup_inputs(seed: int = 0) -> dict:
    key = jax.random.key(seed)
    ks = jax.random.split(key, 14)
    f32 = jnp.float32
    nrm = lambda k, shape, scale: jax.random.normal(k, shape, f32) * scale
    gain = lambda k, shape: 1.0 + 0.02 * jax.random.normal(k, shape, f32)
    return {
        "x": jax.random.normal(ks[0], (BATCH, SEQ, D_MODEL), f32),
        "norm1_g": gain(ks[1], (DEPTH, D_MODEL)),
        "w_in": nrm(ks[2], (DEPTH, D_MODEL, D_IN), D_MODEL ** -0.5),
        "gmlp_ln_g": gain(ks[3], (DEPTH, D_GMLP)),
        "w_spatial": nrm(ks[4], (DEPTH, N_GMLP_GROUPS, CHUNK, CHUNK), CHUNK ** -0.5),
        "b_spatial": gain(ks[5], (DEPTH, N_GMLP_GROUPS, CHUNK)),
        "mix_norm_attn_g": gain(ks[6], (DEPTH, D_ATTN)),
        "mix_norm_gmlp_g": gain(ks[7], (DEPTH, D_GMLP)),
        "w_out": nrm(ks[8], (DEPTH, D_MIX, D_MODEL), D_MIX ** -0.5),
        "norm2_g": gain(ks[9], (DEPTH, D_MODEL)),
        "w_gate": nrm(ks[10], (DEPTH, D_MODEL, D_FF), D_MODEL ** -0.5),
        "w_up": nrm(ks[11], (DEPTH, D_MODEL, D_FF), D_MODEL ** -0.5),
        "w_down": nrm(ks[12], (DEPTH, D_FF, D_MODEL), D_FF ** -0.5),
        "final_g": gain(ks[13], (D_MODEL,)),
    }


def reference(x, norm1_g, w_in, gmlp_ln_g, w_spatial, b_spatial, mix_norm_attn_g,
              mix_norm_gmlp_g, w_out, norm2_g, w_gate, w_up, w_down, final_g):
    B, S, _ = x.shape
    cos, sin = rope_tables(S)
    for l in range(DEPTH):
        h = rmsnorm(x, norm1_g[l])
        proj = jnp.einsum('bsd,de->bse', h, w_in[l])
        q = proj[..., 0:D_ATTN].reshape(B, S, N_HEADS, HEAD_DIM)
        k = proj[..., D_ATTN:2 * D_ATTN].reshape(B, S, N_HEADS, HEAD_DIM)
        v = proj[..., 2 * D_ATTN:3 * D_ATTN].reshape(B, S, N_HEADS, HEAD_DIM)
        uv = proj[..., 3 * D_ATTN:]
        q = apply_rope(q, cos, sin)
        k = apply_rope(k, cos, sin)
        a = dilated_attention(q, k, v)
        g = chunked_spatial_gating(uv, gmlp_ln_g[l], w_spatial[l], b_spatial[l])
        mix = jnp.concatenate([rmsnorm(a, mix_norm_attn_g[l]),
                               rmsnorm(g, mix_norm_gmlp_g[l])], axis=-1)
        x = x + jnp.einsum('bse,ed->bsd', mix, w_out[l])
        h = rmsnorm(x, norm2_g[l])
        ff = jax.nn.silu(jnp.einsum('bsd,df->bsf', h, w_gate[l])) * jnp.einsum('bsd,df->bsf', h, w_up[l])
        x = x + jnp.einsum('bsf,fd->bsd', ff, w_down[l])
    return rmsnorm(x, final_g)
```

```python
import functools
import math

import jax
import jax.numpy as jnp
from jax import lax
from jax.experimental import pallas as pl
from jax.experimental.pallas import tpu as pltpu

HEAD_DIM = 128
GMLP_GROUP = 128
CHUNK = 128
DILATED_PATTERNS = ((128, 1), (512, 4), (2048, 16))
ROPE_THETA = 10000.0
EPS = 1e-6
NEG = -1e30

LANES = 128
Q_TILE = 128
VMEM_LIMIT = 56 * 1024 * 1024

BF16 = jnp.bfloat16
F32 = jnp.float32


def _gelu(x):
    return 0.5 * x * (1.0 + lax.erf(x * (1.0 / math.sqrt(2.0))))


def _rms_scale(x):
    return lax.rsqrt(jnp.mean(x * x, axis=-1, keepdims=True) + EPS)


def _inproj_kernel(x_ref, g1_ref, w_ref, cos_ref, sin_ref, lng_ref, ws_ref, bs_ref,
                   g2_ref, q_ref, k_ref, v_ref, gn_ref, uv_scr, vln_scr, gate_scr,
                   *, d_attn, d_gmlp, q_scale):
    tm = x_ref.shape[0]
    n_heads = d_attn // HEAD_DIM
    n_groups = d_gmlp // GMLP_GROUP

    x = x_ref[...]
    h = (x * _rms_scale(x) * g1_ref[...]).astype(BF16)

    cos = cos_ref[...]
    sin = sin_ref[...]

    def rope(t):
        return t * cos + pltpu.roll(t, HEAD_DIM // 2, axis=1) * sin

    q = jnp.dot(h, w_ref[:, 0:d_attn], preferred_element_type=F32)
    for hh in range(n_heads):
        sl = slice(hh * HEAD_DIM, (hh + 1) * HEAD_DIM)
        q_ref[:, sl] = (rope(q[:, sl]) * q_scale).astype(BF16)
    k = jnp.dot(h, w_ref[:, d_attn:2 * d_attn], preferred_element_type=F32)
    for hh in range(n_heads):
        sl = slice(hh * HEAD_DIM, (hh + 1) * HEAD_DIM)
        k_ref[:, sl] = rope(k[:, sl]).astype(BF16)
    v_ref[...] = jnp.dot(h, w_ref[:, 2 * d_attn:3 * d_attn],
                         preferred_element_type=F32).astype(BF16)

    uv_scr[...] = jnp.dot(h, w_ref[:, 3 * d_attn:3 * d_attn + 2 * d_gmlp],
                          preferred_element_type=F32)

    lng = lng_ref[...]
    g2 = g2_ref[...]
    for c in range(tm // CHUNK):
        rows = slice(c * CHUNK, (c + 1) * CHUNK)
        vv = _gelu(uv_scr[rows, d_gmlp:2 * d_gmlp])
        vc = vv - jnp.mean(vv, axis=-1, keepdims=True)
        vln = vc * lax.rsqrt(jnp.mean(vc * vc, axis=-1, keepdims=True) + EPS) * lng
        vln_scr[rows, :] = vln.astype(BF16)
        for g in range(n_groups):
            cols = slice(g * GMLP_GROUP, (g + 1) * GMLP_GROUP)
            mixed = jnp.dot(ws_ref[g], vln_scr[rows, cols], preferred_element_type=F32)
            u = _gelu(uv_scr[rows, cols])
            gate_scr[rows, cols] = u * (mixed + bs_ref[:, cols])
        gate = gate_scr[rows, :]
        gn_ref[rows, :] = (gate * _rms_scale(gate) * g2).astype(BF16)


def _inproj(x2, g1, w_in, cos2, sin2, lng, ws, bsb, g2, *, seq, d_attn, d_gmlp, tm):
    m, d_model = x2.shape
    d_in = w_in.shape[1]
    n_groups = d_gmlp // GMLP_GROUP
    pos_blocks = seq // tm
    const = lambda i: (0, 0)
    row = lambda i: (i, 0)
    single = pl.Buffered(1)
    kern = functools.partial(_inproj_kernel, d_attn=d_attn, d_gmlp=d_gmlp,
                             q_scale=HEAD_DIM ** -0.5)
    out_sds = jax.ShapeDtypeStruct((m, d_attn), BF16)
    return pl.pallas_call(
        kern,
        grid=(m // tm,),
        in_specs=[
            pl.BlockSpec((tm, d_model), row),
            pl.BlockSpec((1, d_model), const),
            pl.BlockSpec((d_model, d_in), const, pipeline_mode=single),
            pl.BlockSpec((tm, HEAD_DIM), lambda i: (i % pos_blocks, 0)),
            pl.BlockSpec((tm, HEAD_DIM), lambda i: (i % pos_blocks, 0)),
            pl.BlockSpec((1, d_gmlp), const),
            pl.BlockSpec((n_groups, CHUNK, CHUNK), lambda i: (0, 0, 0)),
            pl.BlockSpec((CHUNK, d_gmlp), const),
            pl.BlockSpec((1, d_gmlp), const),
        ],
        out_specs=[pl.BlockSpec((tm, d_attn), row)] * 3 + [pl.BlockSpec((tm, d_gmlp), row)],
        out_shape=[out_sds] * 3 + [jax.ShapeDtypeStruct((m, d_gmlp), BF16)],
        scratch_shapes=[
            pltpu.VMEM((tm, 2 * d_gmlp), F32),
            pltpu.VMEM((tm, d_gmlp), BF16),
            pltpu.VMEM((tm, d_gmlp), F32),
        ],
        compiler_params=pltpu.CompilerParams(
            dimension_semantics=("arbitrary",), vmem_limit_bytes=VMEM_LIMIT),
    )(x2, g1, w_in, cos2, sin2, lng, ws, bsb, g2)


def _attn_kernel(*refs, n_side, sub_len, has_prev, is_last, n_heads):
    q_ref, kp_ref, k_ref, kn_ref, vp_ref, v_ref, vn_ref = refs[:7]
    pos = 7
    if has_prev:
        o_in_ref, m_in_ref = refs[pos:pos + 2]
        pos += 2
    if is_last:
        g_ref = refs[pos]
        pos += 1
        o_out_ref = refs[pos]
        pos += 1
    else:
        o_out_ref, m_out_ref = refs[pos:pos + 2]
        pos += 2
    kx, vx = refs[pos:pos + 2]

    tl = q_ref.shape[1]
    halo = kp_ref.shape[1]
    t = pl.program_id(2)

    kx[0:halo, :] = kp_ref[0]
    kx[halo:halo + tl, :] = k_ref[0]
    kx[halo + tl:, :] = kn_ref[0]
    vx[0:halo, :] = vp_ref[0]
    vx[halo:halo + tl, :] = v_ref[0]
    vx[halo + tl:, :] = vn_ref[0]

    kw = Q_TILE + 2 * halo
    row_i = lax.broadcasted_iota(jnp.int32, (Q_TILE, kw), 0)
    col_j = lax.broadcasted_iota(jnp.int32, (Q_TILE, kw), 1)
    rel = col_j - row_i - halo
    band = (rel >= -n_side) & (rel <= n_side)
    lane = lax.broadcasted_iota(jnp.int32, (Q_TILE, LANES), 1)

    def tile(i, carry):
        r0 = pl.multiple_of(i * Q_TILE, Q_TILE)
        rows = pl.ds(r0, Q_TILE)
        krows = pl.ds(r0, kw)
        kpos = t * tl + i * Q_TILE - halo + col_j
        mask = band & (kpos >= 0) & (kpos < sub_len)
        bias = jnp.where(mask, 0.0, NEG)
        if has_prev:
            m_prev_tile = m_in_ref[0, rows, :]
        m_tile = jnp.zeros((Q_TILE, LANES), F32)
        outs = []
        for hh in range(n_heads):
            cols = slice(hh * HEAD_DIM, (hh + 1) * HEAD_DIM)
            qh = q_ref[0, rows, cols]
            kh = kx[krows, cols]
            vh = vx[krows, cols]
            s = lax.dot_general(qh, kh, (((1,), (1,)), ((), ())),
                                preferred_element_type=F32) + bias
            mx = jnp.max(s, axis=-1, keepdims=True)
            p = jnp.exp(s - mx)
            den = jnp.sum(p, axis=-1, keepdims=True)
            o = jnp.dot(p.astype(BF16), vh, preferred_element_type=F32) / den
            lse = mx + jnp.log(den)
            if has_prev:
                m_prev = jnp.sum(jnp.where(lane == hh, m_prev_tile, 0.0), axis=-1, keepdims=True)
                top = jnp.maximum(m_prev, lse)
                m_new = top + jnp.log(jnp.exp(m_prev - top) + jnp.exp(lse - top))
                o = o_in_ref[0, rows, cols] * jnp.exp(m_prev - m_new) + o * jnp.exp(lse - m_new)
                lse = m_new
            if is_last:
                outs.append(o)
            else:
                o_out_ref[0, rows, cols] = o
                m_tile = jnp.where(lane == hh, lse, m_tile)
        if is_last:
            ssq = outs[0] * outs[0]
            for o in outs[1:]:
                ssq = ssq + o * o
            inv = lax.rsqrt(jnp.sum(ssq, axis=-1, keepdims=True) / (n_heads * HEAD_DIM) + EPS)
            for hh, o in enumerate(outs):
                cols = slice(hh * HEAD_DIM, (hh + 1) * HEAD_DIM)
                o_out_ref[0, rows, cols] = (o * inv * g_ref[:, cols]).astype(o_out_ref.dtype)
        else:
            m_out_ref[0, rows, :] = m_tile
        return carry

    lax.fori_loop(0, tl // Q_TILE, tile, 0)


def _attn_branch(q, k, v, prev, gain, *, window, dilation, is_last):
    b, s, c = q.shape
    n_heads = c // HEAD_DIM
    n_side = window // (2 * dilation)
    halo = n_side
    sub_len = s // dilation
    assert s % (dilation * Q_TILE) == 0 and halo * 2 == Q_TILE and n_heads <= LANES
    tl = min(sub_len, 1024)
    assert sub_len % tl == 0 and tl % halo == 0
    hb = tl // halo
    n_hb = sub_len // halo

    view = lambda a: a.reshape(b, sub_len, dilation * a.shape[-1])
    qv, kv, vv = view(q), view(k), view(v)

    main = lambda bi, r, t: (bi, t, r)
    prev_halo = lambda bi, r, t: (bi, jnp.maximum(t * hb - 1, 0), r)
    next_halo = lambda bi, r, t: (bi, jnp.minimum((t + 1) * hb, n_hb - 1), r)
    big = pl.BlockSpec((1, tl, c), main)
    small_p = pl.BlockSpec((1, halo, c), prev_halo)
    small_n = pl.BlockSpec((1, halo, c), next_halo)
    lse_spec = pl.BlockSpec((1, tl, LANES), main)

    args = [qv, kv, kv, kv, vv, vv, vv]
    in_specs = [big, small_p, big, small_n, small_p, big, small_n]
    has_prev = prev is not None
    if has_prev:
        args += [view(prev[0]), view(prev[1])]
        in_specs += [big, lse_spec]
    if is_last:
        args.append(gain)
        in_specs.append(pl.BlockSpec((1, c), lambda bi, r, t: (0, 0)))
        out_shape = jax.ShapeDtypeStruct((b, sub_len, dilation * c), BF16)
        out_specs = big
    else:
        out_shape = [jax.ShapeDtypeStruct((b, sub_len, dilation * c), F32),
                     jax.ShapeDtypeStruct((b, sub_len, dilation * LANES), F32)]
        out_specs = [big, lse_spec]

    kern = functools.partial(_attn_kernel, n_side=n_side, sub_len=sub_len,
                             has_prev=has_prev, is_last=is_last, n_heads=n_heads)
    res = pl.pallas_call(
        kern,
        grid=(b, dilation, sub_len // tl),
        in_specs=in_specs,
        out_specs=out_specs,
        out_shape=out_shape,
        scratch_shapes=[pltpu.VMEM((tl + 2 * halo, c), BF16),
                        pltpu.VMEM((tl + 2 * halo, c), BF16)],
        compiler_params=pltpu.CompilerParams(
            dimension_semantics=("arbitrary",) * 3, vmem_limit_bytes=VMEM_LIMIT),
    )(*args)
    if is_last:
        return res.reshape(b, s, c)
    return res[0].reshape(b, s, c), res[1].reshape(b, s, LANES)


def _dilated_attention(q, k, v, gain):
    state = None
    last = len(DILATED_PATTERNS) - 1
    for idx, (window, dilation) in enumerate(DILATED_PATTERNS):
        state = _attn_branch(q, k, v, state, gain, window=window, dilation=dilation,
                             is_last=idx == last)
    return state


def _outproj_kernel(x_ref, a_ref, g_ref, w_ref, o_ref):
    d_attn = a_ref.shape[1]
    acc = jnp.dot(a_ref[...], w_ref[0:d_attn, :], preferred_element_type=F32)
    acc = acc + jnp.dot(g_ref[...], w_ref[d_attn:, :], preferred_element_type=F32)
    o_ref[...] = x_ref[...] + acc


def _outproj(x2, an, gn, w_out, *, tm):
    m, d_model = x2.shape
    row = lambda i: (i, 0)
    return pl.pallas_call(
        _outproj_kernel,
        grid=(m // tm,),
        in_specs=[
            pl.BlockSpec((tm, d_model), row),
            pl.BlockSpec((tm, an.shape[1]), row),
            pl.BlockSpec((tm, gn.shape[1]), row),
            pl.BlockSpec(w_out.shape, lambda i: (0, 0), pipeline_mode=pl.Buffered(1)),
        ],
        out_specs=pl.BlockSpec((tm, d_model), row),
        out_shape=jax.ShapeDtypeStruct((m, d_model), F32),
        compiler_params=pltpu.CompilerParams(
            dimension_semantics=("arbitrary",), vmem_limit_bytes=VMEM_LIMIT),
    )(x2, an, gn, w_out)


def _ffn_kernel(*refs, final_norm):
    if final_norm:
        x_ref, g_ref, wg_ref, wu_ref, wd_ref, fg_ref, o_ref, h_scr = refs
    else:
        x_ref, g_ref, wg_ref, wu_ref, wd_ref, o_ref, h_scr = refs
    f = pl.program_id(1)

    @pl.when(f == 0)
    def _():
        x = x_ref[...]
        h_scr[...] = (x * _rms_scale(x) * g_ref[...]).astype(BF16)
        o_ref[...] = x

    h = h_scr[...]
    gate = jnp.dot(h, wg_ref[...], preferred_element_type=F32)
    up = jnp.dot(h, wu_ref[...], preferred_element_type=F32)
    ff = (gate * (1.0 / (1.0 + jnp.exp(-gate))) * up).astype(BF16)
    o_ref[...] += jnp.dot(ff, wd_ref[...], preferred_element_type=F32)

    if final_norm:
        @pl.when(f == pl.num_programs(1) - 1)
        def _():
            y = o_ref[...]
            o_ref[...] = y * _rms_scale(y) * fg_ref[...]


def _ffn(x2, g, w_gate, w_up, w_down, final_g, *, tm, tf):
    m, d_model = x2.shape
    d_ff = w_gate.shape[1]
    final_norm = final_g is not None
    args = [x2, g, w_gate, w_up, w_down]
    in_specs = [
        pl.BlockSpec((tm, d_model), lambda i, f: (i, 0)),
        pl.BlockSpec((1, d_model), lambda i, f: (0, 0)),
        pl.BlockSpec((d_model, tf), lambda i, f: (0, f)),
        pl.BlockSpec((d_model, tf), lambda i, f: (0, f)),
        pl.BlockSpec((tf, d_model), lambda i, f: (f, 0)),
    ]
    if final_norm:
        args.append(final_g)
        in_specs.append(pl.BlockSpec((1, d_model), lambda i, f: (0, 0)))
    return pl.pallas_call(
        functools.partial(_ffn_kernel, final_norm=final_norm),
        grid=(m // tm, d_ff // tf),
        in_specs=in_specs,
        out_specs=pl.BlockSpec((tm, d_model), lambda i, f: (i, 0)),
        out_shape=jax.ShapeDtypeStruct((m, d_model), F32),
        scratch_shapes=[pltpu.VMEM((tm, d_model), BF16)],
        input_output_aliases={0: 0},
        compiler_params=pltpu.CompilerParams(
            dimension_semantics=("arbitrary", "arbitrary"), vmem_limit_bytes=VMEM_LIMIT),
    )(*args)


def _rope_tables(seq):
    pos = jnp.arange(seq, dtype=F32)
    inv = ROPE_THETA ** (-jnp.arange(0, HEAD_DIM, 2, dtype=F32) / HEAD_DIM)
    ang = pos[:, None] * inv[None, :]
    cos, sin = jnp.cos(ang), jnp.sin(ang)
    return jnp.concatenate([cos, cos], axis=-1), jnp.concatenate([-sin, sin], axis=-1)


def kernel(x, norm1_g, w_in, gmlp_ln_g, w_spatial, b_spatial, mix_norm_attn_g,
           mix_norm_gmlp_g, w_out, norm2_g, w_gate, w_up, w_down, final_g):
    b, s, d_model = x.shape
    depth = w_in.shape[0]
    d_gmlp = gmlp_ln_g.shape[1]
    d_attn = mix_norm_attn_g.shape[1]
    n_groups = w_spatial.shape[1]
    d_ff = w_gate.shape[2]
    assert w_in.shape[2] == 3 * d_attn + 2 * d_gmlp and n_groups * GMLP_GROUP == d_gmlp
    assert w_spatial.shape[2] == CHUNK

    tm_in, tm_out, tm_ffn, tf = 512, 512, 512, 512
    assert s % tm_in == 0 and s % CHUNK == 0 and d_ff % tf == 0

    cos2, sin2 = _rope_tables(s)
    x2 = x.reshape(b * s, d_model)
    row = lambda a: a.reshape(1, -1)

    for l in range(depth):
        bsb = jnp.repeat(b_spatial[l].T, GMLP_GROUP, axis=1)
        q, k, v, gn = _inproj(
            x2, row(norm1_g[l]), w_in[l].astype(BF16), cos2, sin2, row(gmlp_ln_g[l]),
            w_spatial[l].astype(BF16), bsb, row(mix_norm_gmlp_g[l]),
            seq=s, d_attn=d_attn, d_gmlp=d_gmlp, tm=tm_in)
        shape3 = lambda a: a.reshape(b, s, a.shape[-1])
        an = _dilated_attention(shape3(q), shape3(k), shape3(v), row(mix_norm_attn_g[l]))
        x2 = _outproj(x2, an.reshape(b * s, d_attn), gn, w_out[l].astype(BF16), tm=tm_out)
        x2 = _ffn(x2, row(norm2_g[l]), w_gate[l].astype(BF16), w_up[l].astype(BF16),
                  w_down[l].astype(BF16), row(final_g) if l == depth - 1 else None,
                  tm=tm_ffn, tf=tf)
    return x2.reshape(b, s, d_model)
```

```python
import functools
import math

import jax
import jax.numpy as jnp
from jax import lax
from jax.experimental import pallas as pl
from jax.experimental.pallas import tpu as pltpu

HEAD_DIM = 128
GMLP_GROUP = 128
CHUNK = 128
DILATED_PATTERNS = ((128, 1), (512, 4), (2048, 16))
ROPE_THETA = 10000.0
EPS = 1e-6
NEG = -1e30

LANES = 128
Q_TILE = 128
VMEM_LIMIT = 56 * 1024 * 1024

BF16 = jnp.bfloat16
F32 = jnp.float32


def _gelu(x):
    return 0.5 * x * (1.0 + lax.erf(x * (1.0 / math.sqrt(2.0))))


def _rms_scale(x):
    return lax.rsqrt(jnp.mean(x * x, axis=-1, keepdims=True) + EPS)


def _head_cols(hh):
    return slice(hh * HEAD_DIM, (hh + 1) * HEAD_DIM)


def _inproj_kernel(*refs, d_attn, d_gmlp, q_scale, dilations):
    (x_ref, g1_ref, w_ref, cos_ref, sin_ref, lng_ref, ws_ref, bs_ref, g2_ref) = refs[:9]
    n_out = 3 * (1 + len(dilations))
    qkv_refs = refs[9:9 + n_out]
    gn_ref = refs[9 + n_out]
    slab, uv_scr, vln_scr, gate_scr = refs[10 + n_out:]

    tm = x_ref.shape[0]
    n_heads = d_attn // HEAD_DIM
    n_groups = d_gmlp // GMLP_GROUP

    x = x_ref[...]
    h = (x * _rms_scale(x) * g1_ref[...]).astype(BF16)

    cos = cos_ref[...]
    sin = sin_ref[...]

    def rope(t):
        return t * cos + pltpu.roll(t, HEAD_DIM // 2, axis=1) * sin

    def emit(idx, head_fn):
        proj = jnp.dot(h, w_ref[:, idx * d_attn:(idx + 1) * d_attn], preferred_element_type=F32)
        outs = qkv_refs[idx::3]
        for hh in range(n_heads):
            val = head_fn(proj[:, _head_cols(hh)])
            outs[0][:, _head_cols(hh)] = val.astype(BF16)
            slab[hh] = val
        for d, out in zip(dilations, outs[1:]):
            for r in range(d):
                for hh in range(n_heads):
                    out[:, _head_cols(r * n_heads + hh)] = (
                        slab[hh, pl.ds(r, tm // d, stride=d), :].astype(BF16))

    emit(0, lambda t: rope(t) * q_scale)
    emit(1, rope)
    emit(2, lambda t: t)

    uv_scr[...] = jnp.dot(h, w_ref[:, 3 * d_attn:3 * d_attn + 2 * d_gmlp],
                          preferred_element_type=F32)

    lng = lng_ref[...]
    g2 = g2_ref[...]
    for c in range(tm // CHUNK):
        rows = slice(c * CHUNK, (c + 1) * CHUNK)
        vv = _gelu(uv_scr[rows, d_gmlp:2 * d_gmlp])
        vc = vv - jnp.mean(vv, axis=-1, keepdims=True)
        vln = vc * lax.rsqrt(jnp.mean(vc * vc, axis=-1, keepdims=True) + EPS) * lng
        vln_scr[rows, :] = vln.astype(BF16)
        for g in range(n_groups):
            cols = slice(g * GMLP_GROUP, (g + 1) * GMLP_GROUP)
            mixed = jnp.dot(ws_ref[g], vln_scr[rows, cols], preferred_element_type=F32)
            u = _gelu(uv_scr[rows, cols])
            gate_scr[rows, cols] = u * (mixed + bs_ref[:, cols])
        gate = gate_scr[rows, :]
        gn_ref[rows, :] = (gate * _rms_scale(gate) * g2).astype(BF16)


def _inproj(x2, g1, w_in, cos2, sin2, lng, ws, bsb, g2, *, layer, seq, d_attn, d_gmlp,
            dilations, tm):
    m, d_model = x2.shape
    d_in = w_in.shape[2]
    n_groups = d_gmlp // GMLP_GROUP
    n_heads = d_attn // HEAD_DIM
    pos_blocks = seq // tm
    assert all(tm % (d * 16) == 0 for d in dilations)
    vec = lambda i: (layer, 0, 0)
    row = lambda i: (i, 0)
    kern = functools.partial(_inproj_kernel, d_attn=d_attn, d_gmlp=d_gmlp,
                             q_scale=HEAD_DIM ** -0.5, dilations=dilations)
    qkv_specs, qkv_shapes = [], []
    for d in (1,) + tuple(dilations):
        qkv_specs += [pl.BlockSpec((tm // d, d * d_attn), row)] * 3
        qkv_shapes += [jax.ShapeDtypeStruct((m // d, d * d_attn), BF16)] * 3
    return pl.pallas_call(
        kern,
        grid=(m // tm,),
        in_specs=[
            pl.BlockSpec((tm, d_model), row),
            pl.BlockSpec((None, 1, d_model), vec),
            pl.BlockSpec((None, d_model, d_in), vec, pipeline_mode=pl.Buffered(1)),
            pl.BlockSpec((tm, HEAD_DIM), lambda i: (i % pos_blocks, 0)),
            pl.BlockSpec((tm, HEAD_DIM), lambda i: (i % pos_blocks, 0)),
            pl.BlockSpec((None, 1, d_gmlp), vec),
            pl.BlockSpec((None, n_groups, CHUNK, CHUNK), lambda i: (layer, 0, 0, 0)),
            pl.BlockSpec((None, CHUNK, d_gmlp), vec),
            pl.BlockSpec((None, 1, d_gmlp), vec),
        ],
        out_specs=qkv_specs + [pl.BlockSpec((tm, d_gmlp), row)],
        out_shape=qkv_shapes + [jax.ShapeDtypeStruct((m, d_gmlp), BF16)],
        scratch_shapes=[
            pltpu.VMEM((n_heads, tm, HEAD_DIM), F32),
            pltpu.VMEM((tm, 2 * d_gmlp), F32),
            pltpu.VMEM((tm, d_gmlp), BF16),
            pltpu.VMEM((tm, d_gmlp), F32),
        ],
        compiler_params=pltpu.CompilerParams(
            dimension_semantics=("arbitrary",), vmem_limit_bytes=VMEM_LIMIT),
    )(x2, g1, w_in, cos2, sin2, lng, ws, bsb, g2)


def _attn_kernel(q_ref, kp_ref, k_ref, kn_ref, vp_ref, v_ref, vn_ref, o_ref, l_ref, kx, vx,
                 *, n_side, sub_len, n_heads):
    tl = q_ref.shape[1]
    halo = kp_ref.shape[1]
    t = pl.program_id(2)

    kx[0:halo, :] = kp_ref[0]
    kx[halo:halo + tl, :] = k_ref[0]
    kx[halo + tl:, :] = kn_ref[0]
    vx[0:halo, :] = vp_ref[0]
    vx[halo:halo + tl, :] = v_ref[0]
    vx[halo + tl:, :] = vn_ref[0]

    kw = Q_TILE + 2 * halo
    row_i = lax.broadcasted_iota(jnp.int32, (Q_TILE, kw), 0)
    col_j = lax.broadcasted_iota(jnp.int32, (Q_TILE, kw), 1)
    rel = col_j - row_i - halo
    band = (rel >= -n_side) & (rel <= n_side)
    lane = lax.broadcasted_iota(jnp.int32, (Q_TILE, LANES), 1)

    def tile(i, carry):
        r0 = pl.multiple_of(i * Q_TILE, Q_TILE)
        rows = pl.ds(r0, Q_TILE)
        krows = pl.ds(r0, kw)
        kpos = t * tl + i * Q_TILE - halo + col_j
        mask = band & (kpos >= 0) & (kpos < sub_len)
        bias = jnp.where(mask, 0.0, NEG)
        l_tile = jnp.zeros((Q_TILE, LANES), F32)
        for hh in range(n_heads):
            cols = _head_cols(hh)
            s = lax.dot_general(q_ref[0, rows, cols], kx[krows, cols], (((1,), (1,)), ((), ())),
                                preferred_element_type=F32) + bias
            mx = jnp.max(s, axis=-1, keepdims=True)
            p = jnp.exp(s - mx)
            den = jnp.sum(p, axis=-1, keepdims=True)
            o = jnp.dot(p.astype(BF16), vx[krows, cols], preferred_element_type=F32) / den
            o_ref[0, rows, cols] = o.astype(o_ref.dtype)
            l_tile = jnp.where(lane == hh, mx + jnp.log(den), l_tile)
        l_ref[0, rows, :] = l_tile
        return carry

    lax.fori_loop(0, tl // Q_TILE, tile, 0)


def _attn_branch(q, k, v, *, batch, window, dilation):
    rows_total, width = q.shape
    c = width // dilation
    sub_len = rows_total // batch
    n_heads = c // HEAD_DIM
    n_side = window // (2 * dilation)
    halo = n_side
    assert halo * 2 == Q_TILE and n_heads <= LANES and sub_len % Q_TILE == 0
    tl = min(sub_len, 1024)
    assert sub_len % tl == 0 and tl % halo == 0
    hb = tl // halo
    n_hb = sub_len // halo

    view = lambda a: a.reshape(batch, sub_len, a.shape[-1])
    qv, kv, vv = view(q), view(k), view(v)

    main = lambda bi, r, t: (bi, t, r)
    prev_halo = lambda bi, r, t: (bi, jnp.maximum(t * hb - 1, 0), r)
    next_halo = lambda bi, r, t: (bi, jnp.minimum((t + 1) * hb, n_hb - 1), r)
    big = pl.BlockSpec((1, tl, c), main)
    small_p = pl.BlockSpec((1, halo, c), prev_halo)
    small_n = pl.BlockSpec((1, halo, c), next_halo)
    lse_spec = pl.BlockSpec((1, tl, LANES), main)

    kern = functools.partial(_attn_kernel, n_side=n_side, sub_len=sub_len, n_heads=n_heads)
    o, lse = pl.pallas_call(
        kern,
        grid=(batch, dilation, sub_len // tl),
        in_specs=[big, small_p, big, small_n, small_p, big, small_n],
        out_specs=[big, lse_spec],
        out_shape=[jax.ShapeDtypeStruct((batch, sub_len, width), BF16),
                   jax.ShapeDtypeStruct((batch, sub_len, dilation * LANES), F32)],
        scratch_shapes=[pltpu.VMEM((tl + 2 * halo, c), BF16),
                        pltpu.VMEM((tl + 2 * halo, c), BF16)],
        compiler_params=pltpu.CompilerParams(
            dimension_semantics=("arbitrary",) * 3, vmem_limit_bytes=VMEM_LIMIT),
    )(qv, kv, kv, kv, vv, vv, vv)
    return o.reshape(rows_total, width), lse.reshape(rows_total, dilation * LANES)


def _outproj_kernel(*refs, dilations, n_heads):
    n_pat = 1 + len(dilations)
    x_ref = refs[0]
    o_refs = refs[1:1 + 2 * n_pat:2]
    l_refs = refs[2:2 + 2 * n_pat:2]
    ga_ref, gn_ref, w_ref, out_ref = refs[1 + 2 * n_pat:5 + 2 * n_pat]
    scratch = refs[5 + 2 * n_pat:]
    o_slabs = scratch[0:len(dilations)]
    l_slabs = scratch[len(dilations):2 * len(dilations)]
    an_scr = scratch[2 * len(dilations)]

    tm = x_ref.shape[0]
    d_attn = n_heads * HEAD_DIM

    for d, o_ref, l_ref, o_slab, l_slab in zip(dilations, o_refs[1:], l_refs[1:], o_slabs, l_slabs):
        for r in range(d):
            dst = pl.ds(r, tm // d, stride=d)
            l_slab[dst, :] = l_ref[:, r * LANES:(r + 1) * LANES]
            for hh in range(n_heads):
                o_slab[hh, dst, :] = o_ref[:, _head_cols(r * n_heads + hh)].astype(F32)

    lses = [l_refs[0][...]] + [l_slab[...] for l_slab in l_slabs]
    top = functools.reduce(jnp.maximum, lses)
    es = [jnp.exp(l - top) for l in lses]
    inv = 1.0 / functools.reduce(jnp.add, es)
    ws = [e * inv for e in es]

    heads = []
    ssq = jnp.zeros((tm, 1), F32)
    for hh in range(n_heads):
        parts = [o_refs[0][:, _head_cols(hh)].astype(F32)] + [o_slab[hh] for o_slab in o_slabs]
        a = functools.reduce(jnp.add, [w[:, hh:hh + 1] * p for w, p in zip(ws, parts)])
        ssq = ssq + jnp.sum(a * a, axis=-1, keepdims=True)
        heads.append(a)
    scale = lax.rsqrt(ssq / d_attn + EPS)
    for hh, a in enumerate(heads):
        an_scr[:, _head_cols(hh)] = (a * scale * ga_ref[:, _head_cols(hh)]).astype(BF16)

    acc = jnp.dot(an_scr[...], w_ref[0:d_attn, :], preferred_element_type=F32)
    acc = acc + jnp.dot(gn_ref[...], w_ref[d_attn:, :], preferred_element_type=F32)
    out_ref[...] = x_ref[...] + acc


def _outproj(x2, branches, ga, gn, w_out, *, layer, dilations, tm):
    m, d_model = x2.shape
    d_attn = branches[0][0].shape[1]
    n_heads = d_attn // HEAD_DIM
    row = lambda i: (i, 0)
    args, in_specs = [x2], [pl.BlockSpec((tm, d_model), row)]
    for d, (o, lse) in zip((1,) + tuple(dilations), branches):
        args += [o, lse]
        in_specs += [pl.BlockSpec((tm // d, d * d_attn), row),
                     pl.BlockSpec((tm // d, d * LANES), row)]
    args += [ga, gn, w_out]
    in_specs += [
        pl.BlockSpec((None, 1, d_attn), lambda i: (layer, 0, 0)),
        pl.BlockSpec((tm, gn.shape[1]), row),
        pl.BlockSpec((None,) + w_out.shape[1:], lambda i: (layer, 0, 0),
                     pipeline_mode=pl.Buffered(1)),
    ]
    return pl.pallas_call(
        functools.partial(_outproj_kernel, dilations=dilations, n_heads=n_heads),
        grid=(m // tm,),
        in_specs=in_specs,
        out_specs=pl.BlockSpec((tm, d_model), row),
        out_shape=jax.ShapeDtypeStruct((m, d_model), F32),
        scratch_shapes=(
            [pltpu.VMEM((n_heads, tm, HEAD_DIM), F32) for _ in dilations]
            + [pltpu.VMEM((tm, LANES), F32) for _ in dilations]
            + [pltpu.VMEM((tm, d_attn), BF16)]),
        compiler_params=pltpu.CompilerParams(
            dimension_semantics=("arbitrary",), vmem_limit_bytes=VMEM_LIMIT),
    )(*args)


def _ffn_kernel(*refs, final_norm):
    if final_norm:
        x_ref, g_ref, wg_ref, wu_ref, wd_ref, fg_ref, o_ref, h_scr = refs
    else:
        x_ref, g_ref, wg_ref, wu_ref, wd_ref, o_ref, h_scr = refs
    f = pl.program_id(1)

    @pl.when(f == 0)
    def _():
        x = x_ref[...]
        h_scr[...] = (x * _rms_scale(x) * g_ref[...]).astype(BF16)
        o_ref[...] = x

    h = h_scr[...]
    gate = jnp.dot(h, wg_ref[...], preferred_element_type=F32)
    up = jnp.dot(h, wu_ref[...], preferred_element_type=F32)
    ff = (gate * (1.0 / (1.0 + jnp.exp(-gate))) * up).astype(BF16)
    o_ref[...] += jnp.dot(ff, wd_ref[...], preferred_element_type=F32)

    if final_norm:
        @pl.when(f == pl.num_programs(1) - 1)
        def _():
            y = o_ref[...]
            o_ref[...] = y * _rms_scale(y) * fg_ref[...]


def _ffn(x2, g, w_gate, w_up, w_down, final_g, *, layer, tm, tf):
    m, d_model = x2.shape
    d_ff = w_gate.shape[2]
    final_norm = final_g is not None
    args = [x2, g, w_gate, w_up, w_down]
    in_specs = [
        pl.BlockSpec((tm, d_model), lambda i, f: (i, 0)),
        pl.BlockSpec((None, 1, d_model), lambda i, f: (layer, 0, 0)),
        pl.BlockSpec((None, d_model, tf), lambda i, f: (layer, 0, f)),
        pl.BlockSpec((None, d_model, tf), lambda i, f: (layer, 0, f)),
        pl.BlockSpec((None, tf, d_model), lambda i, f: (layer, f, 0)),
    ]
    if final_norm:
        args.append(final_g)
        in_specs.append(pl.BlockSpec((1, d_model), lambda i, f: (0, 0)))
    return pl.pallas_call(
        functools.partial(_ffn_kernel, final_norm=final_norm),
        grid=(m // tm, d_ff // tf),
        in_specs=in_specs,
        out_specs=pl.BlockSpec((tm, d_model), lambda i, f: (i, 0)),
        out_shape=jax.ShapeDtypeStruct((m, d_model), F32),
        scratch_shapes=[pltpu.VMEM((tm, d_model), BF16)],
        input_output_aliases={0: 0},
        compiler_params=pltpu.CompilerParams(
            dimension_semantics=("arbitrary", "arbitrary"), vmem_limit_bytes=VMEM_LIMIT),
    )(*args)


def _rope_tables(seq):
    pos = jnp.arange(seq, dtype=F32)
    inv = ROPE_THETA ** (-jnp.arange(0, HEAD_DIM, 2, dtype=F32) / HEAD_DIM)
    ang = pos[:, None] * inv[None, :]
    cos, sin = jnp.cos(ang), jnp.sin(ang)
    return jnp.concatenate([cos, cos], axis=-1), jnp.concatenate([-sin, sin], axis=-1)


def kernel(x, norm1_g, w_in, gmlp_ln_g, w_spatial, b_spatial, mix_norm_attn_g,
           mix_norm_gmlp_g, w_out, norm2_g, w_gate, w_up, w_down, final_g):
    b, s, d_model = x.shape
    depth = w_in.shape[0]
    d_gmlp = gmlp_ln_g.shape[1]
    d_attn = mix_norm_attn_g.shape[1]
    n_groups = w_spatial.shape[1]
    d_ff = w_gate.shape[2]
    assert w_in.shape[2] == 3 * d_attn + 2 * d_gmlp and n_groups * GMLP_GROUP == d_gmlp
    assert w_spatial.shape[2] == CHUNK
    assert DILATED_PATTERNS[0][1] == 1
    dilations = tuple(d for _, d in DILATED_PATTERNS[1:])

    tm_in, tm_out, tm_ffn, tf = 256, 512, 512, 512
    assert s % tm_in == 0 and s % tm_out == 0 and s % CHUNK == 0 and d_ff % tf == 0

    cos2, sin2 = _rope_tables(s)
    x2 = x.reshape(b * s, d_model)
    vec3 = lambda a: a.reshape(a.shape[0], 1, a.shape[1])

    w_in_b, w_out_b = w_in.astype(BF16), w_out.astype(BF16)
    w_gate_b, w_up_b, w_down_b = w_gate.astype(BF16), w_up.astype(BF16), w_down.astype(BF16)
    ws_b = w_spatial.astype(BF16)
    bsb = jnp.repeat(jnp.swapaxes(b_spatial, 1, 2), GMLP_GROUP, axis=2)
    g1, lng, g2, ga, gf = (vec3(a) for a in (norm1_g, gmlp_ln_g, mix_norm_gmlp_g,
                                              mix_norm_attn_g, norm2_g))

    for l in range(depth):
        outs = _inproj(x2, g1, w_in_b, cos2, sin2, lng, ws_b, bsb, g2, layer=l, seq=s,
                       d_attn=d_attn, d_gmlp=d_gmlp, dilations=dilations, tm=tm_in)
        gn = outs[-1]
        branches = [
            _attn_branch(*outs[3 * i:3 * i + 3], batch=b, window=window, dilation=dilation)
            for i, (window, dilation) in enumerate(DILATED_PATTERNS)]
        x2 = _outproj(x2, branches, ga, gn, w_out_b, layer=l, dilations=dilations, tm=tm_out)
        x2 = _ffn(x2, gf, w_gate_b, w_up_b, w_down_b,
                  final_g.reshape(1, -1) if l == depth - 1 else None,
                  layer=l, tm=tm_ffn, tf=tf)
    return x2.reshape(b, s, d_model)
```

```python
import functools
import math

import jax
import jax.numpy as jnp
from jax import lax
from jax.experimental import pallas as pl
from jax.experimental.pallas import tpu as pltpu

HEAD_DIM = 128
GMLP_GROUP = 128
CHUNK = 128
DILATED_PATTERNS = ((128, 1), (512, 4), (2048, 16))
ROPE_THETA = 10000.0
EPS = 1e-6
NEG = -1e30

LANES = 128
MXU_COLS = 256
Q_TILE = 128
VMEM_LIMIT = 56 * 1024 * 1024

BF16 = jnp.bfloat16
F32 = jnp.float32


def _gelu(x):
    return 0.5 * x * (1.0 + lax.erf(x * (1.0 / math.sqrt(2.0))))


def _rms_scale(x):
    return lax.rsqrt(jnp.mean(x * x, axis=-1, keepdims=True) + EPS)


def _head_cols(hh):
    return slice(hh * HEAD_DIM, (hh + 1) * HEAD_DIM)


def _inproj_kernel(*refs, d_attn, d_gmlp, q_scale, dilations):
    (x_ref, g1_ref, w_ref, cos_ref, sin_ref, lng_ref, ws_ref, bs_ref, g2_ref) = refs[:9]
    n_out = 3 * (1 + len(dilations))
    qkv_refs = refs[9:9 + n_out]
    gn_ref = refs[9 + n_out]
    slabs = refs[10 + n_out:13 + n_out]
    h_scr, uv_scr, vln_scr, gate_scr = refs[13 + n_out:]

    tm = x_ref.shape[0]
    n_heads = d_attn // HEAD_DIM
    n_groups = d_gmlp // GMLP_GROUP

    x = x_ref[...]
    h_scr[...] = (x * _rms_scale(x) * g1_ref[...]).astype(BF16)

    def project(col0, width):
        return jnp.dot(h_scr[...], w_ref[:, col0:col0 + width], preferred_element_type=F32)

    cos = cos_ref[...]
    sin = sin_ref[...]

    def rope(t):
        return t * cos + pltpu.roll(t, HEAD_DIM // 2, axis=1) * sin

    def emit(idx, head_fn):
        outs = qkv_refs[idx::3]
        slab = slabs[idx]
        for pair in range(0, n_heads, MXU_COLS // HEAD_DIM):
            proj = project(idx * d_attn + pair * HEAD_DIM, MXU_COLS)
            for sub in range(MXU_COLS // HEAD_DIM):
                hh = pair + sub
                val = head_fn(proj[:, _head_cols(sub)])
                outs[0][:, _head_cols(hh)] = val.astype(BF16)
                slab[hh] = val
        for d, out in zip(dilations, outs[1:]):
            for r in range(d):
                for hh in range(n_heads):
                    out[:, _head_cols(r * n_heads + hh)] = (
                        slab[hh, pl.ds(r, tm // d, stride=d), :].astype(BF16))

    lng = lng_ref[...]
    g2 = g2_ref[...]

    def gating(c):
        rows = slice(c * CHUNK, (c + 1) * CHUNK)
        vv = _gelu(uv_scr[rows, d_gmlp:2 * d_gmlp])
        vc = vv - jnp.mean(vv, axis=-1, keepdims=True)
        vln = vc * lax.rsqrt(jnp.mean(vc * vc, axis=-1, keepdims=True) + EPS) * lng
        vln_scr[rows, :] = vln.astype(BF16)
        for g in range(n_groups):
            cols = slice(g * GMLP_GROUP, (g + 1) * GMLP_GROUP)
            mixed = jnp.dot(ws_ref[g], vln_scr[rows, cols], preferred_element_type=F32)
            u = _gelu(uv_scr[rows, cols])
            gate_scr[rows, cols] = u * (mixed + bs_ref[:, cols])
        gate = gate_scr[rows, :]
        gn_ref[rows, :] = (gate * _rms_scale(gate) * g2).astype(BF16)

    for col in range(0, 2 * d_gmlp, MXU_COLS):
        uv_scr[:, col:col + MXU_COLS] = project(3 * d_attn + col, MXU_COLS)
    emitters = [lambda: emit(0, lambda t: rope(t) * q_scale), lambda: emit(1, rope),
                lambda: emit(2, lambda t: t)]
    n_chunks = tm // CHUNK
    for step in range(max(n_chunks, len(emitters))):
        if step < len(emitters):
            emitters[step]()
        for c in range(step * n_chunks // len(emitters), (step + 1) * n_chunks // len(emitters)):
            gating(c)


def _inproj(x2, g1, w_in, cos2, sin2, lng, ws, bsb, g2, *, layer, seq, d_attn, d_gmlp,
            dilations, tm):
    m, d_model = x2.shape
    d_in = w_in.shape[2]
    n_groups = d_gmlp // GMLP_GROUP
    n_heads = d_attn // HEAD_DIM
    pos_blocks = seq // tm
    assert all(tm % (d * 16) == 0 for d in dilations)
    vec = lambda i: (layer, 0, 0)
    row = lambda i: (i, 0)
    kern = functools.partial(_inproj_kernel, d_attn=d_attn, d_gmlp=d_gmlp,
                             q_scale=HEAD_DIM ** -0.5 * math.log2(math.e), dilations=dilations)
    qkv_specs, qkv_shapes = [], []
    for d in (1,) + tuple(dilations):
        qkv_specs += [pl.BlockSpec((tm // d, d * d_attn), row)] * 3
        qkv_shapes += [jax.ShapeDtypeStruct((m // d, d * d_attn), BF16)] * 3
    return pl.pallas_call(
        kern,
        grid=(m // tm,),
        in_specs=[
            pl.BlockSpec((tm, d_model), row),
            pl.BlockSpec((None, 1, d_model), vec),
            pl.BlockSpec((None, d_model, d_in), vec, pipeline_mode=pl.Buffered(1)),
            pl.BlockSpec((tm, HEAD_DIM), lambda i: (i % pos_blocks, 0)),
            pl.BlockSpec((tm, HEAD_DIM), lambda i: (i % pos_blocks, 0)),
            pl.BlockSpec((None, 1, d_gmlp), vec),
            pl.BlockSpec((None, n_groups, CHUNK, CHUNK), lambda i: (layer, 0, 0, 0)),
            pl.BlockSpec((None, CHUNK, d_gmlp), vec),
            pl.BlockSpec((None, 1, d_gmlp), vec),
        ],
        out_specs=qkv_specs + [pl.BlockSpec((tm, d_gmlp), row)],
        out_shape=qkv_shapes + [jax.ShapeDtypeStruct((m, d_gmlp), BF16)],
        scratch_shapes=[pltpu.VMEM((n_heads, tm, HEAD_DIM), F32)] * 3 + [
            pltpu.VMEM((tm, d_model), BF16),
            pltpu.VMEM((tm, 2 * d_gmlp), F32),
            pltpu.VMEM((tm, d_gmlp), BF16),
            pltpu.VMEM((tm, d_gmlp), F32),
        ],
        compiler_params=pltpu.CompilerParams(
            dimension_semantics=("arbitrary",), vmem_limit_bytes=VMEM_LIMIT),
    )(x2, g1, w_in, cos2, sin2, lng, ws, bsb, g2)


def _attn_kernel(q_ref, kp_ref, k_ref, kn_ref, vp_ref, v_ref, vn_ref, o_ref, m_ref, d_ref,
                 *, n_side, n_heads):
    tl = q_ref.shape[1]
    halo = kp_ref.shape[1]
    n_tiles = tl // Q_TILE
    t = pl.program_id(2)
    first_block = t == 0
    last_block = t == pl.num_programs(2) - 1

    kw = Q_TILE + 2 * halo
    row_i = lax.broadcasted_iota(jnp.int32, (Q_TILE, kw), 0)
    col_j = lax.broadcasted_iota(jnp.int32, (Q_TILE, kw), 1)
    rel = col_j - row_i - halo
    band = (rel >= -n_side) & (rel <= n_side)
    head_ok = (col_j >= halo) | jnp.logical_not(first_block)
    tail_ok = (col_j < halo + Q_TILE) | jnp.logical_not(last_block)
    lane = lax.broadcasted_iota(jnp.int32, (Q_TILE, LANES), 1)
    ones = jnp.ones((kw, HEAD_DIM), BF16)

    def window(prev_ref, main_ref, next_ref, i, cols):
        lo, hi = i * Q_TILE - halo, (i + 1) * Q_TILE + halo
        parts = []
        if lo < 0:
            parts.append(prev_ref[0, :, cols])
        parts.append(main_ref[0, max(lo, 0):min(hi, tl), cols])
        if hi > tl:
            parts.append(next_ref[0, :, cols])
        return parts[0] if len(parts) == 1 else jnp.concatenate(parts, axis=0)

    for i in range(n_tiles):
        rows = slice(i * Q_TILE, (i + 1) * Q_TILE)
        mask = band
        if i == 0:
            mask = mask & head_ok
        if i == n_tiles - 1:
            mask = mask & tail_ok
        bias = jnp.where(mask, 0.0, NEG)
        m_tile = jnp.zeros((Q_TILE, LANES), F32)
        d_tile = jnp.ones((Q_TILE, LANES), F32)
        for hh in range(n_heads):
            cols = _head_cols(hh)
            kh = window(kp_ref, k_ref, kn_ref, i, cols)
            vh = window(vp_ref, v_ref, vn_ref, i, cols)
            s = lax.dot_general(q_ref[0, rows, cols], kh, (((1,), (1,)), ((), ())),
                                preferred_element_type=F32) + bias
            mx = jnp.max(s, axis=-1, keepdims=True)
            p = jnp.exp2(s - mx).astype(BF16)
            ov = jnp.dot(p, jnp.concatenate([vh, ones], axis=1), preferred_element_type=F32)
            o_ref[0, rows, cols] = ov[:, :HEAD_DIM].astype(o_ref.dtype)
            m_tile = jnp.where(lane == hh, mx, m_tile)
            d_tile = jnp.where(lane == hh, ov[:, HEAD_DIM:], d_tile)
        m_ref[0, rows, :] = m_tile
        d_ref[0, rows, :] = d_tile


def _attn_branch(q, k, v, *, batch, window, dilation):
    rows_total, width = q.shape
    c = width // dilation
    sub_len = rows_total // batch
    n_heads = c // HEAD_DIM
    n_side = window // (2 * dilation)
    halo = n_side
    assert halo * 2 == Q_TILE and n_heads <= LANES and sub_len % Q_TILE == 0
    tl = min(sub_len, 1024)
    assert sub_len % tl == 0 and tl % halo == 0
    hb = tl // halo
    n_hb = sub_len // halo

    view = lambda a: a.reshape(batch, sub_len, a.shape[-1])
    qv, kv, vv = view(q), view(k), view(v)

    main = lambda bi, r, t: (bi, t, r)
    prev_halo = lambda bi, r, t: (bi, jnp.maximum(t * hb - 1, 0), r)
    next_halo = lambda bi, r, t: (bi, jnp.minimum((t + 1) * hb, n_hb - 1), r)
    big = pl.BlockSpec((1, tl, c), main)
    small_p = pl.BlockSpec((1, halo, c), prev_halo)
    small_n = pl.BlockSpec((1, halo, c), next_halo)
    stat_spec = pl.BlockSpec((1, tl, LANES), main)
    stat_shape = jax.ShapeDtypeStruct((batch, sub_len, dilation * LANES), F32)

    kern = functools.partial(_attn_kernel, n_side=n_side, n_heads=n_heads)
    o, mx, den = pl.pallas_call(
        kern,
        grid=(batch, dilation, sub_len // tl),
        in_specs=[big, small_p, big, small_n, small_p, big, small_n],
        out_specs=[big, stat_spec, stat_spec],
        out_shape=[jax.ShapeDtypeStruct((batch, sub_len, width), BF16), stat_shape, stat_shape],
        compiler_params=pltpu.CompilerParams(
            dimension_semantics=("arbitrary",) * 3, vmem_limit_bytes=VMEM_LIMIT),
    )(qv, kv, kv, kv, vv, vv, vv)
    flat = lambda a: a.reshape(rows_total, a.shape[-1])
    return flat(o), flat(mx), flat(den)


def _outproj_kernel(*refs, dilations, n_heads):
    n_pat = 1 + len(dilations)
    n_dil = len(dilations)
    x_ref = refs[0]
    o_refs = refs[1:1 + 3 * n_pat:3]
    m_refs = refs[2:2 + 3 * n_pat:3]
    d_refs = refs[3:3 + 3 * n_pat:3]
    ga_ref, gn_ref, w_ref, out_ref = refs[1 + 3 * n_pat:5 + 3 * n_pat]
    scratch = refs[5 + 3 * n_pat:]
    o_slabs = scratch[0:n_dil]
    m_slabs = scratch[n_dil:2 * n_dil]
    d_slabs = scratch[2 * n_dil:3 * n_dil]
    an_scr = scratch[3 * n_dil]

    tm = x_ref.shape[0]
    d_attn = n_heads * HEAD_DIM

    for idx, d in enumerate(dilations):
        for r in range(d):
            dst = pl.ds(r, tm // d, stride=d)
            m_slabs[idx][dst, :] = m_refs[idx + 1][:, r * LANES:(r + 1) * LANES]
            d_slabs[idx][dst, :] = d_refs[idx + 1][:, r * LANES:(r + 1) * LANES]
            for hh in range(n_heads):
                o_slabs[idx][hh, dst, :] = (
                    o_refs[idx + 1][:, _head_cols(r * n_heads + hh)].astype(F32))

    mxs = [m_refs[0][...]] + [m_slab[...] for m_slab in m_slabs]
    dens = [d_refs[0][...]] + [d_slab[...] for d_slab in d_slabs]
    top = functools.reduce(jnp.maximum, mxs)
    es = [jnp.exp2(mx - top) for mx in mxs]
    inv = 1.0 / functools.reduce(jnp.add, [e * den for e, den in zip(es, dens)])
    ws = [e * inv for e in es]

    heads = []
    ssq = jnp.zeros((tm, 1), F32)
    for hh in range(n_heads):
        parts = [o_refs[0][:, _head_cols(hh)].astype(F32)] + [o_slab[hh] for o_slab in o_slabs]
        a = functools.reduce(jnp.add, [w[:, hh:hh + 1] * p for w, p in zip(ws, parts)])
        ssq = ssq + jnp.sum(a * a, axis=-1, keepdims=True)
        heads.append(a)
    scale = lax.rsqrt(ssq / d_attn + EPS)
    for hh, a in enumerate(heads):
        an_scr[:, _head_cols(hh)] = (a * scale * ga_ref[:, _head_cols(hh)]).astype(BF16)

    acc = jnp.dot(an_scr[...], w_ref[0:d_attn, :], preferred_element_type=F32)
    acc = acc + jnp.dot(gn_ref[...], w_ref[d_attn:, :], preferred_element_type=F32)
    out_ref[...] = x_ref[...] + acc


def _outproj(x2, branches, ga, gn, w_out, *, layer, dilations, tm):
    m, d_model = x2.shape
    d_attn = branches[0][0].shape[1]
    n_heads = d_attn // HEAD_DIM
    row = lambda i: (i, 0)
    args, in_specs = [x2], [pl.BlockSpec((tm, d_model), row)]
    for d, branch in zip((1,) + tuple(dilations), branches):
        args += list(branch)
        in_specs += [pl.BlockSpec((tm // d, d * d_attn), row)] + [
            pl.BlockSpec((tm // d, d * LANES), row)] * 2
    args += [ga, gn, w_out]
    in_specs += [
        pl.BlockSpec((None, 1, d_attn), lambda i: (layer, 0, 0)),
        pl.BlockSpec((tm, gn.shape[1]), row),
        pl.BlockSpec((None,) + w_out.shape[1:], lambda i: (layer, 0, 0),
                     pipeline_mode=pl.Buffered(1)),
    ]
    return pl.pallas_call(
        functools.partial(_outproj_kernel, dilations=dilations, n_heads=n_heads),
        grid=(m // tm,),
        in_specs=in_specs,
        out_specs=pl.BlockSpec((tm, d_model), row),
        out_shape=jax.ShapeDtypeStruct((m, d_model), F32),
        scratch_shapes=(
            [pltpu.VMEM((n_heads, tm, HEAD_DIM), F32) for _ in dilations]
            + [pltpu.VMEM((tm, LANES), F32) for _ in dilations] * 2
            + [pltpu.VMEM((tm, d_attn), BF16)]),
        compiler_params=pltpu.CompilerParams(
            dimension_semantics=("arbitrary",), vmem_limit_bytes=VMEM_LIMIT),
    )(*args)


def _ffn_kernel(*refs, final_norm):
    if final_norm:
        x_ref, g_ref, wg_ref, wu_ref, wd_ref, fg_ref, o_ref, h_scr = refs
    else:
        x_ref, g_ref, wg_ref, wu_ref, wd_ref, o_ref, h_scr = refs
    f = pl.program_id(1)

    @pl.when(f == 0)
    def _():
        x = x_ref[...]
        h_scr[...] = (x * _rms_scale(x) * g_ref[...]).astype(BF16)
        o_ref[...] = x

    h = h_scr[...]
    gate = jnp.dot(h, wg_ref[...], preferred_element_type=F32)
    up = jnp.dot(h, wu_ref[...], preferred_element_type=F32)
    ff = (gate * (1.0 / (1.0 + jnp.exp(-gate))) * up).astype(BF16)
    o_ref[...] += jnp.dot(ff, wd_ref[...], preferred_element_type=F32)

    if final_norm:
        @pl.when(f == pl.num_programs(1) - 1)
        def _():
            y = o_ref[...]
            o_ref[...] = y * _rms_scale(y) * fg_ref[...]


def _ffn(x2, g, w_gate, w_up, w_down, final_g, *, layer, tm, tf):
    m, d_model = x2.shape
    d_ff = w_gate.shape[2]
    final_norm = final_g is not None
    args = [x2, g, w_gate, w_up, w_down]
    in_specs = [
        pl.BlockSpec((tm, d_model), lambda i, f: (i, 0)),
        pl.BlockSpec((None, 1, d_model), lambda i, f: (layer, 0, 0)),
        pl.BlockSpec((None, d_model, tf), lambda i, f: (layer, 0, f)),
        pl.BlockSpec((None, d_model, tf), lambda i, f: (layer, 0, f)),
        pl.BlockSpec((None, tf, d_model), lambda i, f: (layer, f, 0)),
    ]
    if final_norm:
        args.append(final_g)
        in_specs.append(pl.BlockSpec((1, d_model), lambda i, f: (0, 0)))
    return pl.pallas_call(
        functools.partial(_ffn_kernel, final_norm=final_norm),
        grid=(m // tm, d_ff // tf),
        in_specs=in_specs,
        out_specs=pl.BlockSpec((tm, d_model), lambda i, f: (i, 0)),
        out_shape=jax.ShapeDtypeStruct((m, d_model), F32),
        scratch_shapes=[pltpu.VMEM((tm, d_model), BF16)],
        input_output_aliases={0: 0},
        compiler_params=pltpu.CompilerParams(
            dimension_semantics=("arbitrary", "arbitrary"), vmem_limit_bytes=VMEM_LIMIT),
    )(*args)


def _rope_tables(seq):
    pos = jnp.arange(seq, dtype=F32)
    inv = ROPE_THETA ** (-jnp.arange(0, HEAD_DIM, 2, dtype=F32) / HEAD_DIM)
    ang = pos[:, None] * inv[None, :]
    cos, sin = jnp.cos(ang), jnp.sin(ang)
    return jnp.concatenate([cos, cos], axis=-1), jnp.concatenate([-sin, sin], axis=-1)


def kernel(x, norm1_g, w_in, gmlp_ln_g, w_spatial, b_spatial, mix_norm_attn_g,
           mix_norm_gmlp_g, w_out, norm2_g, w_gate, w_up, w_down, final_g):
    b, s, d_model = x.shape
    depth = w_in.shape[0]
    d_gmlp = gmlp_ln_g.shape[1]
    d_attn = mix_norm_attn_g.shape[1]
    n_groups = w_spatial.shape[1]
    d_ff = w_gate.shape[2]
    assert w_in.shape[2] == 3 * d_attn + 2 * d_gmlp and n_groups * GMLP_GROUP == d_gmlp
    assert w_spatial.shape[2] == CHUNK
    assert DILATED_PATTERNS[0][1] == 1
    dilations = tuple(d for _, d in DILATED_PATTERNS[1:])

    tm_in, tm_out, tm_ffn, tf = 256, 512, 512, 512
    assert s % tm_in == 0 and s % tm_out == 0 and s % CHUNK == 0 and d_ff % tf == 0

    cos2, sin2 = _rope_tables(s)
    x2 = x.reshape(b * s, d_model)
    vec3 = lambda a: a.reshape(a.shape[0], 1, a.shape[1])

    w_in_b, w_out_b = w_in.astype(BF16), w_out.astype(BF16)
    w_gate_b, w_up_b, w_down_b = w_gate.astype(BF16), w_up.astype(BF16), w_down.astype(BF16)
    ws_b = w_spatial.astype(BF16)
    bsb = jnp.repeat(jnp.swapaxes(b_spatial, 1, 2), GMLP_GROUP, axis=2)
    g1, lng, g2, ga, gf = (vec3(a) for a in (norm1_g, gmlp_ln_g, mix_norm_gmlp_g,
                                              mix_norm_attn_g, norm2_g))

    for l in range(depth):
        outs = _inproj(x2, g1, w_in_b, cos2, sin2, lng, ws_b, bsb, g2, layer=l, seq=s,
                       d_attn=d_attn, d_gmlp=d_gmlp, dilations=dilations, tm=tm_in)
        gn = outs[-1]
        branches = [
            _attn_branch(*outs[3 * i:3 * i + 3], batch=b, window=window, dilation=dilation)
            for i, (window, dilation) in enumerate(DILATED_PATTERNS)]
        x2 = _outproj(x2, branches, ga, gn, w_out_b, layer=l, dilations=dilations, tm=tm_out)
        x2 = _ffn(x2, gf, w_gate_b, w_up_b, w_down_b,
                  final_g.reshape(1, -1) if l == depth - 1 else None,
                  layer=l, tm=tm_ffn, tf=tf)
    return x2.reshape(b, s, d_model)
```

```python
import functools
import math

import jax
import jax.numpy as jnp
from jax import lax
from jax.experimental import pallas as pl
from jax.experimental.pallas import tpu as pltpu

HEAD_DIM = 128
GMLP_GROUP = 128
CHUNK = 128
DILATED_PATTERNS = ((128, 1), (512, 4), (2048, 16))
ROPE_THETA = 10000.0
EPS = 1e-6
NEG = -1e30

LANES = 128
MXU_COLS = 256
Q_TILE = 128
ATTN_TILES_PER_STEP = 8
VMEM_LIMIT = 56 * 1024 * 1024

BF16 = jnp.bfloat16
F32 = jnp.float32


def _gelu(x):
    return 0.5 * x * (1.0 + lax.erf(x * (1.0 / math.sqrt(2.0))))


def _rms_scale(x):
    return lax.rsqrt(jnp.mean(x * x, axis=-1, keepdims=True) + EPS)


def _head_cols(hh):
    return slice(hh * HEAD_DIM, (hh + 1) * HEAD_DIM)


def _inproj_kernel(*refs, d_attn, d_gmlp, q_scale, dilations):
    (x_ref, g1_ref, w_ref, cos_ref, sin_ref, lng_ref, ws_ref, bs_ref, g2_ref) = refs[:9]
    n_out = 3 * (1 + len(dilations))
    qkv_refs = refs[9:9 + n_out]
    gn_ref = refs[9 + n_out]
    slabs = refs[10 + n_out:13 + n_out]
    h_scr, uv_scr, vln_scr, gate_scr = refs[13 + n_out:]

    tm = x_ref.shape[0]
    n_heads = d_attn // HEAD_DIM
    n_groups = d_gmlp // GMLP_GROUP

    x = x_ref[...]
    h_scr[...] = (x * _rms_scale(x) * g1_ref[...]).astype(BF16)

    def project(col0, width):
        return jnp.dot(h_scr[...], w_ref[:, col0:col0 + width], preferred_element_type=F32)

    cos = cos_ref[...]
    sin = sin_ref[...]

    def rope(t):
        return t * cos + pltpu.roll(t, HEAD_DIM // 2, axis=1) * sin

    def emit(idx, head_fn):
        outs = qkv_refs[idx::3]
        slab = slabs[idx]
        for pair in range(0, n_heads, MXU_COLS // HEAD_DIM):
            proj = project(idx * d_attn + pair * HEAD_DIM, MXU_COLS)
            for sub in range(MXU_COLS // HEAD_DIM):
                hh = pair + sub
                val = head_fn(proj[:, _head_cols(sub)])
                outs[0][:, _head_cols(hh)] = val.astype(BF16)
                slab[hh] = val
        for d, out in zip(dilations, outs[1:]):
            for r in range(d):
                for hh in range(n_heads):
                    out[:, _head_cols(r * n_heads + hh)] = (
                        slab[hh, pl.ds(r, tm // d, stride=d), :].astype(BF16))

    lng = lng_ref[...]
    g2 = g2_ref[...]

    def gating(c):
        rows = slice(c * CHUNK, (c + 1) * CHUNK)
        vv = _gelu(uv_scr[rows, d_gmlp:2 * d_gmlp])
        vc = vv - jnp.mean(vv, axis=-1, keepdims=True)
        vln = vc * lax.rsqrt(jnp.mean(vc * vc, axis=-1, keepdims=True) + EPS) * lng
        vln_scr[rows, :] = vln.astype(BF16)
        for g in range(n_groups):
            cols = slice(g * GMLP_GROUP, (g + 1) * GMLP_GROUP)
            mixed = jnp.dot(ws_ref[g], vln_scr[rows, cols], preferred_element_type=F32)
            u = _gelu(uv_scr[rows, cols])
            gate_scr[rows, cols] = u * (mixed + bs_ref[:, cols])
        gate = gate_scr[rows, :]
        gn_ref[rows, :] = (gate * _rms_scale(gate) * g2).astype(BF16)

    for col in range(0, 2 * d_gmlp, MXU_COLS):
        uv_scr[:, col:col + MXU_COLS] = project(3 * d_attn + col, MXU_COLS)
    emitters = [lambda: emit(0, lambda t: rope(t) * q_scale), lambda: emit(1, rope),
                lambda: emit(2, lambda t: t)]
    n_chunks = tm // CHUNK
    for step in range(max(n_chunks, len(emitters))):
        if step < len(emitters):
            emitters[step]()
        for c in range(step * n_chunks // len(emitters), (step + 1) * n_chunks // len(emitters)):
            gating(c)


def _inproj(x2, g1, w_in, cos2, sin2, lng, ws, bsb, g2, *, layer, seq, d_attn, d_gmlp,
            dilations, tm):
    m, d_model = x2.shape
    d_in = w_in.shape[2]
    n_groups = d_gmlp // GMLP_GROUP
    n_heads = d_attn // HEAD_DIM
    pos_blocks = seq // tm
    assert all(tm % (d * 16) == 0 for d in dilations)
    vec = lambda i: (layer, 0, 0)
    row = lambda i: (i, 0)
    kern = functools.partial(_inproj_kernel, d_attn=d_attn, d_gmlp=d_gmlp,
                             q_scale=HEAD_DIM ** -0.5 * math.log2(math.e), dilations=dilations)
    qkv_specs, qkv_shapes = [], []
    for d in (1,) + tuple(dilations):
        qkv_specs += [pl.BlockSpec((tm // d, d * d_attn), row)] * 3
        qkv_shapes += [jax.ShapeDtypeStruct((m // d, d * d_attn), BF16)] * 3
    return pl.pallas_call(
        kern,
        grid=(m // tm,),
        in_specs=[
            pl.BlockSpec((tm, d_model), row),
            pl.BlockSpec((None, 1, d_model), vec),
            pl.BlockSpec((None, d_model, d_in), vec, pipeline_mode=pl.Buffered(1)),
            pl.BlockSpec((tm, HEAD_DIM), lambda i: (i % pos_blocks, 0)),
            pl.BlockSpec((tm, HEAD_DIM), lambda i: (i % pos_blocks, 0)),
            pl.BlockSpec((None, 1, d_gmlp), vec),
            pl.BlockSpec((None, n_groups, CHUNK, CHUNK), lambda i: (layer, 0, 0, 0)),
            pl.BlockSpec((None, CHUNK, d_gmlp), vec),
            pl.BlockSpec((None, 1, d_gmlp), vec),
        ],
        out_specs=qkv_specs + [pl.BlockSpec((tm, d_gmlp), row)],
        out_shape=qkv_shapes + [jax.ShapeDtypeStruct((m, d_gmlp), BF16)],
        scratch_shapes=[pltpu.VMEM((n_heads, tm, HEAD_DIM), F32)] * 3 + [
            pltpu.VMEM((tm, d_model), BF16),
            pltpu.VMEM((tm, 2 * d_gmlp), F32),
            pltpu.VMEM((tm, d_gmlp), BF16),
            pltpu.VMEM((tm, d_gmlp), F32),
        ],
        compiler_params=pltpu.CompilerParams(
            dimension_semantics=("arbitrary",), vmem_limit_bytes=VMEM_LIMIT),
    )(x2, g1, w_in, cos2, sin2, lng, ws, bsb, g2)


def _attn_kernel(q_ref, kp_ref, k_ref, kn_ref, vp_ref, v_ref, vn_ref, o_ref, m_ref, d_ref,
                 *, n_side, n_heads):
    tl = q_ref.shape[1]
    halo = kp_ref.shape[1]
    n_tiles = tl // Q_TILE
    t = pl.program_id(2)
    first_block = t == 0
    last_block = t == pl.num_programs(2) - 1

    kw = Q_TILE + 2 * halo
    row_i = lax.broadcasted_iota(jnp.int32, (Q_TILE, kw), 0)
    col_j = lax.broadcasted_iota(jnp.int32, (Q_TILE, kw), 1)
    rel = col_j - row_i - halo
    band = (rel >= -n_side) & (rel <= n_side)
    head_ok = (col_j >= halo) | jnp.logical_not(first_block)
    tail_ok = (col_j < halo + Q_TILE) | jnp.logical_not(last_block)
    lane = lax.broadcasted_iota(jnp.int32, (Q_TILE, LANES), 1)
    ones = jnp.ones((kw, HEAD_DIM), BF16)

    def window(prev_ref, main_ref, next_ref, i, cols):
        lo, hi = i * Q_TILE - halo, (i + 1) * Q_TILE + halo
        parts = []
        if lo < 0:
            parts.append(prev_ref[0, :, cols])
        parts.append(main_ref[0, max(lo, 0):min(hi, tl), cols])
        if hi > tl:
            parts.append(next_ref[0, :, cols])
        return parts[0] if len(parts) == 1 else jnp.concatenate(parts, axis=0)

    biases = {}
    for i in range(n_tiles):
        mask = band
        if i == 0:
            mask = mask & head_ok
        if i == n_tiles - 1:
            mask = mask & tail_ok
        biases[i] = jnp.where(mask, 0.0, NEG)

    n_res = q_ref.shape[2] // (n_heads * HEAD_DIM)
    for res, i in ((res, i) for res in range(n_res) for i in range(n_tiles)):
        rows = slice(i * Q_TILE, (i + 1) * Q_TILE)
        bias = biases[i]
        m_tile = jnp.zeros((Q_TILE, LANES), F32)
        d_tile = jnp.ones((Q_TILE, LANES), F32)
        for hh in range(n_heads):
            cols = _head_cols(res * n_heads + hh)
            kh = window(kp_ref, k_ref, kn_ref, i, cols)
            vh = window(vp_ref, v_ref, vn_ref, i, cols)
            s = lax.dot_general(q_ref[0, rows, cols], kh, (((1,), (1,)), ((), ())),
                                preferred_element_type=F32) + bias
            mx = jnp.max(s, axis=-1, keepdims=True)
            p = jnp.exp2(s - mx).astype(BF16)
            ov = jnp.dot(p, jnp.concatenate([vh, ones], axis=1), preferred_element_type=F32)
            o_ref[0, rows, cols] = ov[:, :HEAD_DIM].astype(o_ref.dtype)
            m_tile = jnp.where(lane == hh, mx, m_tile)
            d_tile = jnp.where(lane == hh, ov[:, HEAD_DIM:], d_tile)
        m_ref[0, rows, res * LANES:(res + 1) * LANES] = m_tile
        d_ref[0, rows, res * LANES:(res + 1) * LANES] = d_tile


def _attn_branch(q, k, v, *, batch, window, dilation):
    rows_total, width = q.shape
    c = width // dilation
    sub_len = rows_total // batch
    n_heads = c // HEAD_DIM
    n_side = window // (2 * dilation)
    halo = n_side
    assert halo * 2 == Q_TILE and n_heads <= LANES and sub_len % Q_TILE == 0
    tl = min(sub_len, 1024)
    assert sub_len % tl == 0 and tl % halo == 0
    hb = tl // halo
    n_hb = sub_len // halo
    n_res = min(dilation, max(1, ATTN_TILES_PER_STEP * Q_TILE // tl))
    assert dilation % n_res == 0

    view = lambda a: a.reshape(batch, sub_len, a.shape[-1])
    qv, kv, vv = view(q), view(k), view(v)

    main = lambda bi, r, t: (bi, t, r)
    prev_halo = lambda bi, r, t: (bi, jnp.maximum(t * hb - 1, 0), r)
    next_halo = lambda bi, r, t: (bi, jnp.minimum((t + 1) * hb, n_hb - 1), r)
    big = pl.BlockSpec((1, tl, n_res * c), main)
    small_p = pl.BlockSpec((1, halo, n_res * c), prev_halo)
    small_n = pl.BlockSpec((1, halo, n_res * c), next_halo)
    stat_spec = pl.BlockSpec((1, tl, n_res * LANES), main)
    stat_shape = jax.ShapeDtypeStruct((batch, sub_len, dilation * LANES), F32)

    kern = functools.partial(_attn_kernel, n_side=n_side, n_heads=n_heads)
    o, mx, den = pl.pallas_call(
        kern,
        grid=(batch, dilation // n_res, sub_len // tl),
        in_specs=[big, small_p, big, small_n, small_p, big, small_n],
        out_specs=[big, stat_spec, stat_spec],
        out_shape=[jax.ShapeDtypeStruct((batch, sub_len, width), BF16), stat_shape, stat_shape],
        compiler_params=pltpu.CompilerParams(
            dimension_semantics=("arbitrary",) * 3, vmem_limit_bytes=VMEM_LIMIT),
    )(qv, kv, kv, kv, vv, vv, vv)
    flat = lambda a: a.reshape(rows_total, a.shape[-1])
    return flat(o), flat(mx), flat(den)


def _outproj_kernel(*refs, dilations, n_heads):
    n_pat = 1 + len(dilations)
    n_dil = len(dilations)
    x_ref = refs[0]
    o_refs = refs[1:1 + 3 * n_pat:3]
    m_refs = refs[2:2 + 3 * n_pat:3]
    d_refs = refs[3:3 + 3 * n_pat:3]
    ga_ref, gn_ref, w_ref, out_ref = refs[1 + 3 * n_pat:5 + 3 * n_pat]
    scratch = refs[5 + 3 * n_pat:]
    o_slabs = scratch[0:n_dil]
    m_slabs = scratch[n_dil:2 * n_dil]
    d_slabs = scratch[2 * n_dil:3 * n_dil]
    an_scr = scratch[3 * n_dil]

    tm = x_ref.shape[0]
    d_attn = n_heads * HEAD_DIM

    for idx, d in enumerate(dilations):
        for r in range(d):
            dst = pl.ds(r, tm // d, stride=d)
            m_slabs[idx][dst, :] = m_refs[idx + 1][:, r * LANES:(r + 1) * LANES]
            d_slabs[idx][dst, :] = d_refs[idx + 1][:, r * LANES:(r + 1) * LANES]
            for hh in range(n_heads):
                o_slabs[idx][hh, dst, :] = (
                    o_refs[idx + 1][:, _head_cols(r * n_heads + hh)].astype(F32))

    mxs = [m_refs[0][...]] + [m_slab[...] for m_slab in m_slabs]
    dens = [d_refs[0][...]] + [d_slab[...] for d_slab in d_slabs]
    top = functools.reduce(jnp.maximum, mxs)
    es = [jnp.exp2(mx - top) for mx in mxs]
    inv = 1.0 / functools.reduce(jnp.add, [e * den for e, den in zip(es, dens)])
    ws = [e * inv for e in es]

    heads = []
    ssq = jnp.zeros((tm, 1), F32)
    for hh in range(n_heads):
        parts = [o_refs[0][:, _head_cols(hh)].astype(F32)] + [o_slab[hh] for o_slab in o_slabs]
        a = functools.reduce(jnp.add, [w[:, hh:hh + 1] * p for w, p in zip(ws, parts)])
        ssq = ssq + jnp.sum(a * a, axis=-1, keepdims=True)
        heads.append(a)
    scale = lax.rsqrt(ssq / d_attn + EPS)
    for hh, a in enumerate(heads):
        an_scr[:, _head_cols(hh)] = (a * scale * ga_ref[:, _head_cols(hh)]).astype(BF16)

    acc = jnp.dot(an_scr[...], w_ref[0:d_attn, :], preferred_element_type=F32)
    acc = acc + jnp.dot(gn_ref[...], w_ref[d_attn:, :], preferred_element_type=F32)
    out_ref[...] = x_ref[...] + acc


def _outproj(x2, branches, ga, gn, w_out, *, layer, dilations, tm):
    m, d_model = x2.shape
    d_attn = branches[0][0].shape[1]
    n_heads = d_attn // HEAD_DIM
    row = lambda i: (i, 0)
    args, in_specs = [x2], [pl.BlockSpec((tm, d_model), row)]
    for d, branch in zip((1,) + tuple(dilations), branches):
        args += list(branch)
        in_specs += [pl.BlockSpec((tm // d, d * d_attn), row)] + [
            pl.BlockSpec((tm // d, d * LANES), row)] * 2
    args += [ga, gn, w_out]
    in_specs += [
        pl.BlockSpec((None, 1, d_attn), lambda i: (layer, 0, 0)),
        pl.BlockSpec((tm, gn.shape[1]), row),
        pl.BlockSpec((None,) + w_out.shape[1:], lambda i: (layer, 0, 0),
                     pipeline_mode=pl.Buffered(1)),
    ]
    return pl.pallas_call(
        functools.partial(_outproj_kernel, dilations=dilations, n_heads=n_heads),
        grid=(m // tm,),
        in_specs=in_specs,
        out_specs=pl.BlockSpec((tm, d_model), row),
        out_shape=jax.ShapeDtypeStruct((m, d_model), F32),
        scratch_shapes=(
            [pltpu.VMEM((n_heads, tm, HEAD_DIM), F32) for _ in dilations]
            + [pltpu.VMEM((tm, LANES), F32) for _ in dilations] * 2
            + [pltpu.VMEM((tm, d_attn), BF16)]),
        compiler_params=pltpu.CompilerParams(
            dimension_semantics=("arbitrary",), vmem_limit_bytes=VMEM_LIMIT),
    )(*args)


def _ffn_kernel(*refs, final_norm):
    if final_norm:
        x_ref, g_ref, wg_ref, wu_ref, wd_ref, fg_ref, o_ref, h_scr = refs
    else:
        x_ref, g_ref, wg_ref, wu_ref, wd_ref, o_ref, h_scr = refs
    f = pl.program_id(1)

    @pl.when(f == 0)
    def _():
        x = x_ref[...]
        h_scr[...] = (x * _rms_scale(x) * g_ref[...]).astype(BF16)
        o_ref[...] = x

    h = h_scr[...]
    gate = jnp.dot(h, wg_ref[...], preferred_element_type=F32)
    up = jnp.dot(h, wu_ref[...], preferred_element_type=F32)
    ff = (gate * (1.0 / (1.0 + jnp.exp(-gate))) * up).astype(BF16)
    o_ref[...] += jnp.dot(ff, wd_ref[...], preferred_element_type=F32)

    if final_norm:
        @pl.when(f == pl.num_programs(1) - 1)
        def _():
            y = o_ref[...]
            o_ref[...] = y * _rms_scale(y) * fg_ref[...]


def _ffn(x2, g, w_gate, w_up, w_down, final_g, *, layer, tm, tf):
    m, d_model = x2.shape
    d_ff = w_gate.shape[2]
    final_norm = final_g is not None
    args = [x2, g, w_gate, w_up, w_down]
    in_specs = [
        pl.BlockSpec((tm, d_model), lambda i, f: (i, 0)),
        pl.BlockSpec((None, 1, d_model), lambda i, f: (layer, 0, 0)),
        pl.BlockSpec((None, d_model, tf), lambda i, f: (layer, 0, f)),
        pl.BlockSpec((None, d_model, tf), lambda i, f: (layer, 0, f)),
        pl.BlockSpec((None, tf, d_model), lambda i, f: (layer, f, 0)),
    ]
    if final_norm:
        args.append(final_g)
        in_specs.append(pl.BlockSpec((1, d_model), lambda i, f: (0, 0)))
    return pl.pallas_call(
        functools.partial(_ffn_kernel, final_norm=final_norm),
        grid=(m // tm, d_ff // tf),
        in_specs=in_specs,
        out_specs=pl.BlockSpec((tm, d_model), lambda i, f: (i, 0)),
        out_shape=jax.ShapeDtypeStruct((m, d_model), F32),
        scratch_shapes=[pltpu.VMEM((tm, d_model), BF16)],
        input_output_aliases={0: 0},
        compiler_params=pltpu.CompilerParams(
            dimension_semantics=("arbitrary", "arbitrary"), vmem_limit_bytes=VMEM_LIMIT),
    )(*args)


def _rope_tables(seq):
    pos = jnp.arange(seq, dtype=F32)
    inv = ROPE_THETA ** (-jnp.arange(0, HEAD_DIM, 2, dtype=F32) / HEAD_DIM)
    ang = pos[:, None] * inv[None, :]
    cos, sin = jnp.cos(ang), jnp.sin(ang)
    return jnp.concatenate([cos, cos], axis=-1), jnp.concatenate([-sin, sin], axis=-1)


def kernel(x, norm1_g, w_in, gmlp_ln_g, w_spatial, b_spatial, mix_norm_attn_g,
           mix_norm_gmlp_g, w_out, norm2_g, w_gate, w_up, w_down, final_g):
    b, s, d_model = x.shape
    depth = w_in.shape[0]
    d_gmlp = gmlp_ln_g.shape[1]
    d_attn = mix_norm_attn_g.shape[1]
    n_groups = w_spatial.shape[1]
    d_ff = w_gate.shape[2]
    assert w_in.shape[2] == 3 * d_attn + 2 * d_gmlp and n_groups * GMLP_GROUP == d_gmlp
    assert w_spatial.shape[2] == CHUNK
    assert DILATED_PATTERNS[0][1] == 1
    dilations = tuple(d for _, d in DILATED_PATTERNS[1:])

    tm_in, tm_out, tm_ffn, tf = 256, 512, 1024, 512
    assert s % tm_in == 0 and s % tm_out == 0 and s % CHUNK == 0 and d_ff % tf == 0

    cos2, sin2 = _rope_tables(s)
    x2 = x.reshape(b * s, d_model)
    vec3 = lambda a: a.reshape(a.shape[0], 1, a.shape[1])

    w_in_b, w_out_b = w_in.astype(BF16), w_out.astype(BF16)
    w_gate_b, w_up_b, w_down_b = w_gate.astype(BF16), w_up.astype(BF16), w_down.astype(BF16)
    ws_b = w_spatial.astype(BF16)
    bsb = jnp.repeat(jnp.swapaxes(b_spatial, 1, 2), GMLP_GROUP, axis=2)
    g1, lng, g2, ga, gf = (vec3(a) for a in (norm1_g, gmlp_ln_g, mix_norm_gmlp_g,
                                              mix_norm_attn_g, norm2_g))

    for l in range(depth):
        outs = _inproj(x2, g1, w_in_b, cos2, sin2, lng, ws_b, bsb, g2, layer=l, seq=s,
                       d_attn=d_attn, d_gmlp=d_gmlp, dilations=dilations, tm=tm_in)
        gn = outs[-1]
        branches = [
            _attn_branch(*outs[3 * i:3 * i + 3], batch=b, window=window, dilation=dilation)
            for i, (window, dilation) in enumerate(DILATED_PATTERNS)]
        x2 = _outproj(x2, branches, ga, gn, w_out_b, layer=l, dilations=dilations, tm=tm_out)
        x2 = _ffn(x2, gf, w_gate_b, w_up_b, w_down_b,
                  final_g.reshape(1, -1) if l == depth - 1 else None,
                  layer=l, tm=tm_ffn, tf=tf)
    return x2.reshape(b, s, d_model)
```

```python
import functools
import math

import jax
import jax.numpy as jnp
from jax import lax
from jax.experimental import pallas as pl
from jax.experimental.pallas import tpu as pltpu

HEAD_DIM = 128
GMLP_GROUP = 128
CHUNK = 128
DILATED_PATTERNS = ((128, 1), (512, 4), (2048, 16))
ROPE_THETA = 10000.0
EPS = 1e-6
NEG = -1e30

LANES = 128
MXU_COLS = 256
Q_TILE = 128
ATTN_TILES_PER_STEP = 8
VMEM_LIMIT = 56 * 1024 * 1024

BF16 = jnp.bfloat16
F32 = jnp.float32


def _gelu(x):
    return 0.5 * x * (1.0 + lax.erf(x * (1.0 / math.sqrt(2.0))))


def _rms_scale(x):
    return lax.rsqrt(jnp.mean(x * x, axis=-1, keepdims=True) + EPS)


def _head_cols(hh):
    return slice(hh * HEAD_DIM, (hh + 1) * HEAD_DIM)


def _interleave(major, minor):
    done = 0
    for j, task in enumerate(major):
        task()
        upto = (j + 1) * len(minor) // len(major)
        for other in minor[done:upto]:
            other()
        done = upto


def _inproj_kernel(*refs, d_attn, d_gmlp, q_scale, dilations, n_tiles):
    (x_ref, g1_ref, w_ref, cos_ref, sin_ref, lng_ref, ws_ref, bs_ref, g2_ref) = refs[:9]
    n_res_out = 3 * len(dilations)
    nat_refs = refs[9:12]
    res_refs = refs[12:12 + n_res_out]
    gn_ref = refs[12 + n_res_out]
    scratch = refs[13 + n_res_out:]
    stages = (scratch[0:4], scratch[4:8])
    h_scr, vln_scr, gate_scr = scratch[8:]

    tm = x_ref.shape[0]
    n_heads = d_attn // HEAD_DIM
    n_groups = d_gmlp // GMLP_GROUP
    heads_per_dot = MXU_COLS // HEAD_DIM

    def rope(t):
        return t * cos_ref[...] + pltpu.roll(t, HEAD_DIM // 2, axis=1) * sin_ref[...]

    head_fns = (lambda t: rope(t) * q_scale, rope, lambda t: t)

    def project_tasks(stage):
        slabs, uv_scr = stage[:3], stage[3]

        def norm():
            x = x_ref[...]
            h_scr[...] = (x * _rms_scale(x) * g1_ref[...]).astype(BF16)

        def project(col0):
            return jnp.dot(h_scr[...], w_ref[:, col0:col0 + MXU_COLS], preferred_element_type=F32)

        def uv_task(col):
            uv_scr[:, col:col + MXU_COLS] = project(3 * d_attn + col)

        def qkv_task(idx, first):
            proj = project(idx * d_attn + first * HEAD_DIM)
            for sub in range(heads_per_dot):
                val = head_fns[idx](proj[:, _head_cols(sub)])
                nat_refs[idx][:, _head_cols(first + sub)] = val.astype(BF16)
                slabs[idx][first + sub] = val

        tasks = [norm]
        tasks += [functools.partial(uv_task, col) for col in range(0, 2 * d_gmlp, MXU_COLS)]
        tasks += [functools.partial(qkv_task, idx, first) for idx in range(3)
                  for first in range(0, n_heads, heads_per_dot)]
        return tasks

    def finish_tasks(stage):
        slabs, uv_scr = stage[:3], stage[3]

        def copy_task(idx, d, out, r):
            for hh in range(n_heads):
                out[:, _head_cols(r * n_heads + hh)] = (
                    slabs[idx][hh, pl.ds(r, tm // d, stride=d), :].astype(BF16))

        def layernorm_task(rows):
            vv = _gelu(uv_scr[rows, d_gmlp:2 * d_gmlp])
            vc = vv - jnp.mean(vv, axis=-1, keepdims=True)
            vln = vc * lax.rsqrt(jnp.mean(vc * vc, axis=-1, keepdims=True) + EPS) * lng_ref[...]
            vln_scr[rows, :] = vln.astype(BF16)

        def mix_task(rows, g):
            cols = slice(g * GMLP_GROUP, (g + 1) * GMLP_GROUP)
            mixed = jnp.dot(ws_ref[g], vln_scr[rows, cols], preferred_element_type=F32)
            u = _gelu(uv_scr[rows, cols])
            gate_scr[rows, cols] = u * (mixed + bs_ref[:, cols])

        def gate_norm_task(rows):
            gate = gate_scr[rows, :]
            gn_ref[rows, :] = (gate * _rms_scale(gate) * g2_ref[...]).astype(BF16)

        tasks = []
        for c in range(tm // CHUNK):
            rows = slice(c * CHUNK, (c + 1) * CHUNK)
            tasks.append(functools.partial(layernorm_task, rows))
            tasks += [functools.partial(mix_task, rows, g) for g in range(n_groups)]
            tasks.append(functools.partial(gate_norm_task, rows))
        tasks += [functools.partial(copy_task, idx, d, out, r) for idx in range(3)
                  for d, out in zip(dilations, res_refs[idx::3]) for r in range(d)]
        return tasks

    i = pl.program_id(0)
    interior = (i > 0) & (i < n_tiles)

    @pl.when(i == 0)
    def _():
        _interleave(project_tasks(stages[0]), [])

    for parity in range(2):
        @pl.when(interior & (i % 2 == parity))
        def _():
            _interleave(project_tasks(stages[parity]), finish_tasks(stages[1 - parity]))

    @pl.when(i == n_tiles)
    def _():
        _interleave(finish_tasks(stages[(n_tiles - 1) % 2]), [])


def _inproj(x2, g1, w_in, cos2, sin2, lng, ws, bsb, g2, *, layer, seq, d_attn, d_gmlp,
            dilations, tm):
    m, d_model = x2.shape
    d_in = w_in.shape[2]
    n_groups = d_gmlp // GMLP_GROUP
    n_heads = d_attn // HEAD_DIM
    pos_blocks = seq // tm
    n_tiles = m // tm
    assert all(tm % (d * 16) == 0 for d in dilations)
    vec = lambda i: (layer, 0, 0)
    cur = lambda i: (jnp.minimum(i, n_tiles - 1), 0)
    lag = lambda i: (jnp.maximum(i - 1, 0), 0)
    pos = lambda i: (jnp.minimum(i, n_tiles - 1) % pos_blocks, 0)
    kern = functools.partial(_inproj_kernel, d_attn=d_attn, d_gmlp=d_gmlp,
                             q_scale=HEAD_DIM ** -0.5 * math.log2(math.e), dilations=dilations,
                             n_tiles=n_tiles)
    qkv_specs = [pl.BlockSpec((tm, d_attn), cur)] * 3
    qkv_shapes = [jax.ShapeDtypeStruct((m, d_attn), BF16)] * 3
    for d in dilations:
        qkv_specs += [pl.BlockSpec((tm // d, d * d_attn), lag)] * 3
        qkv_shapes += [jax.ShapeDtypeStruct((m // d, d * d_attn), BF16)] * 3
    stage = [pltpu.VMEM((n_heads, tm, HEAD_DIM), F32)] * 3 + [pltpu.VMEM((tm, 2 * d_gmlp), F32)]
    return pl.pallas_call(
        kern,
        grid=(n_tiles + 1,),
        in_specs=[
            pl.BlockSpec((tm, d_model), cur),
            pl.BlockSpec((None, 1, d_model), vec),
            pl.BlockSpec((None, d_model, d_in), vec, pipeline_mode=pl.Buffered(1)),
            pl.BlockSpec((tm, HEAD_DIM), pos),
            pl.BlockSpec((tm, HEAD_DIM), pos),
            pl.BlockSpec((None, 1, d_gmlp), vec),
            pl.BlockSpec((None, n_groups, CHUNK, CHUNK), lambda i: (layer, 0, 0, 0)),
            pl.BlockSpec((None, CHUNK, d_gmlp), vec),
            pl.BlockSpec((None, 1, d_gmlp), vec),
        ],
        out_specs=qkv_specs + [pl.BlockSpec((tm, d_gmlp), lag)],
        out_shape=qkv_shapes + [jax.ShapeDtypeStruct((m, d_gmlp), BF16)],
        scratch_shapes=stage * 2 + [
            pltpu.VMEM((tm, d_model), BF16),
            pltpu.VMEM((tm, d_gmlp), BF16),
            pltpu.VMEM((tm, d_gmlp), F32),
        ],
        compiler_params=pltpu.CompilerParams(
            dimension_semantics=("arbitrary",), vmem_limit_bytes=VMEM_LIMIT),
    )(x2, g1, w_in, cos2, sin2, lng, ws, bsb, g2)


def _attn_kernel(q_ref, kp_ref, k_ref, kn_ref, vp_ref, v_ref, vn_ref, o_ref, m_ref, d_ref,
                 *, n_side, n_heads):
    tl = q_ref.shape[1]
    halo = kp_ref.shape[1]
    n_tiles = tl // Q_TILE
    t = pl.program_id(2)
    first_block = t == 0
    last_block = t == pl.num_programs(2) - 1

    kw = Q_TILE + 2 * halo
    row_i = lax.broadcasted_iota(jnp.int32, (Q_TILE, kw), 0)
    col_j = lax.broadcasted_iota(jnp.int32, (Q_TILE, kw), 1)
    rel = col_j - row_i - halo
    band = (rel >= -n_side) & (rel <= n_side)
    head_ok = (col_j >= halo) | jnp.logical_not(first_block)
    tail_ok = (col_j < halo + Q_TILE) | jnp.logical_not(last_block)
    lane = lax.broadcasted_iota(jnp.int32, (Q_TILE, LANES), 1)
    ones = jnp.ones((kw, HEAD_DIM), BF16)

    def window(prev_ref, main_ref, next_ref, i, cols):
        lo, hi = i * Q_TILE - halo, (i + 1) * Q_TILE + halo
        parts = []
        if lo < 0:
            parts.append(prev_ref[0, :, cols])
        parts.append(main_ref[0, max(lo, 0):min(hi, tl), cols])
        if hi > tl:
            parts.append(next_ref[0, :, cols])
        return parts[0] if len(parts) == 1 else jnp.concatenate(parts, axis=0)

    biases = {}
    for i in range(n_tiles):
        mask = band
        if i == 0:
            mask = mask & head_ok
        if i == n_tiles - 1:
            mask = mask & tail_ok
        biases[i] = jnp.where(mask, 0.0, NEG)

    n_res = q_ref.shape[2] // (n_heads * HEAD_DIM)
    for res, i in ((res, i) for res in range(n_res) for i in range(n_tiles)):
        rows = slice(i * Q_TILE, (i + 1) * Q_TILE)
        bias = biases[i]
        m_tile = jnp.zeros((Q_TILE, LANES), F32)
        d_tile = jnp.ones((Q_TILE, LANES), F32)
        for hh in range(n_heads):
            cols = _head_cols(res * n_heads + hh)
            kh = window(kp_ref, k_ref, kn_ref, i, cols)
            vh = window(vp_ref, v_ref, vn_ref, i, cols)
            s = lax.dot_general(q_ref[0, rows, cols], kh, (((1,), (1,)), ((), ())),
                                preferred_element_type=F32) + bias
            mx = jnp.max(s, axis=-1, keepdims=True)
            p = jnp.exp2(s - mx).astype(BF16)
            ov = jnp.dot(p, jnp.concatenate([vh, ones], axis=1), preferred_element_type=F32)
            o_ref[0, rows, cols] = ov[:, :HEAD_DIM].astype(o_ref.dtype)
            m_tile = jnp.where(lane == hh, mx, m_tile)
            d_tile = jnp.where(lane == hh, ov[:, HEAD_DIM:], d_tile)
        m_ref[0, rows, res * LANES:(res + 1) * LANES] = m_tile
        d_ref[0, rows, res * LANES:(res + 1) * LANES] = d_tile


def _attn_branch(q, k, v, *, batch, window, dilation):
    rows_total, width = q.shape
    c = width // dilation
    sub_len = rows_total // batch
    n_heads = c // HEAD_DIM
    n_side = window // (2 * dilation)
    halo = n_side
    assert halo * 2 == Q_TILE and n_heads <= LANES and sub_len % Q_TILE == 0
    tl = min(sub_len, 1024)
    assert sub_len % tl == 0 and tl % halo == 0
    hb = tl // halo
    n_hb = sub_len // halo
    n_res = min(dilation, max(1, ATTN_TILES_PER_STEP * Q_TILE // tl))
    assert dilation % n_res == 0

    view = lambda a: a.reshape(batch, sub_len, a.shape[-1])
    qv, kv, vv = view(q), view(k), view(v)

    main = lambda bi, r, t: (bi, t, r)
    prev_halo = lambda bi, r, t: (bi, jnp.maximum(t * hb - 1, 0), r)
    next_halo = lambda bi, r, t: (bi, jnp.minimum((t + 1) * hb, n_hb - 1), r)
    big = pl.BlockSpec((1, tl, n_res * c), main)
    small_p = pl.BlockSpec((1, halo, n_res * c), prev_halo)
    small_n = pl.BlockSpec((1, halo, n_res * c), next_halo)
    stat_spec = pl.BlockSpec((1, tl, n_res * LANES), main)
    stat_shape = jax.ShapeDtypeStruct((batch, sub_len, dilation * LANES), F32)

    kern = functools.partial(_attn_kernel, n_side=n_side, n_heads=n_heads)
    o, mx, den = pl.pallas_call(
        kern,
        grid=(batch, dilation // n_res, sub_len // tl),
        in_specs=[big, small_p, big, small_n, small_p, big, small_n],
        out_specs=[big, stat_spec, stat_spec],
        out_shape=[jax.ShapeDtypeStruct((batch, sub_len, width), BF16), stat_shape, stat_shape],
        compiler_params=pltpu.CompilerParams(
            dimension_semantics=("arbitrary",) * 3, vmem_limit_bytes=VMEM_LIMIT),
    )(qv, kv, kv, kv, vv, vv, vv)
    flat = lambda a: a.reshape(rows_total, a.shape[-1])
    return flat(o), flat(mx), flat(den)


def _outproj_kernel(*refs, dilations, n_heads):
    n_pat = 1 + len(dilations)
    n_dil = len(dilations)
    x_ref = refs[0]
    o_refs = refs[1:1 + 3 * n_pat:3]
    m_refs = refs[2:2 + 3 * n_pat:3]
    d_refs = refs[3:3 + 3 * n_pat:3]
    ga_ref, gn_ref, w_ref, out_ref = refs[1 + 3 * n_pat:5 + 3 * n_pat]
    scratch = refs[5 + 3 * n_pat:]
    o_slabs = scratch[0:n_dil]
    m_slabs = scratch[n_dil:2 * n_dil]
    d_slabs = scratch[2 * n_dil:3 * n_dil]
    an_scr = scratch[3 * n_dil]

    tm = x_ref.shape[0]
    d_attn = n_heads * HEAD_DIM

    for idx, d in enumerate(dilations):
        for r in range(d):
            dst = pl.ds(r, tm // d, stride=d)
            m_slabs[idx][dst, :] = m_refs[idx + 1][:, r * LANES:(r + 1) * LANES]
            d_slabs[idx][dst, :] = d_refs[idx + 1][:, r * LANES:(r + 1) * LANES]
            for hh in range(n_heads):
                o_slabs[idx][hh, dst, :] = (
                    o_refs[idx + 1][:, _head_cols(r * n_heads + hh)].astype(F32))

    mxs = [m_refs[0][...]] + [m_slab[...] for m_slab in m_slabs]
    dens = [d_refs[0][...]] + [d_slab[...] for d_slab in d_slabs]
    top = functools.reduce(jnp.maximum, mxs)
    es = [jnp.exp2(mx - top) for mx in mxs]
    inv = 1.0 / functools.reduce(jnp.add, [e * den for e, den in zip(es, dens)])
    ws = [e * inv for e in es]

    heads = []
    ssq = jnp.zeros((tm, 1), F32)
    for hh in range(n_heads):
        parts = [o_refs[0][:, _head_cols(hh)].astype(F32)] + [o_slab[hh] for o_slab in o_slabs]
        a = functools.reduce(jnp.add, [w[:, hh:hh + 1] * p for w, p in zip(ws, parts)])
        ssq = ssq + jnp.sum(a * a, axis=-1, keepdims=True)
        heads.append(a)
    scale = lax.rsqrt(ssq / d_attn + EPS)
    for hh, a in enumerate(heads):
        an_scr[:, _head_cols(hh)] = (a * scale * ga_ref[:, _head_cols(hh)]).astype(BF16)

    acc = jnp.dot(an_scr[...], w_ref[0:d_attn, :], preferred_element_type=F32)
    acc = acc + jnp.dot(gn_ref[...], w_ref[d_attn:, :], preferred_element_type=F32)
    out_ref[...] = x_ref[...] + acc


def _outproj(x2, branches, ga, gn, w_out, *, layer, dilations, tm):
    m, d_model = x2.shape
    d_attn = branches[0][0].shape[1]
    n_heads = d_attn // HEAD_DIM
    row = lambda i: (i, 0)
    args, in_specs = [x2], [pl.BlockSpec((tm, d_model), row)]
    for d, branch in zip((1,) + tuple(dilations), branches):
        args += list(branch)
        in_specs += [pl.BlockSpec((tm // d, d * d_attn), row)] + [
            pl.BlockSpec((tm // d, d * LANES), row)] * 2
    args += [ga, gn, w_out]
    in_specs += [
        pl.BlockSpec((None, 1, d_attn), lambda i: (layer, 0, 0)),
        pl.BlockSpec((tm, gn.shape[1]), row),
        pl.BlockSpec((None,) + w_out.shape[1:], lambda i: (layer, 0, 0),
                     pipeline_mode=pl.Buffered(1)),
    ]
    return pl.pallas_call(
        functools.partial(_outproj_kernel, dilations=dilations, n_heads=n_heads),
        grid=(m // tm,),
        in_specs=in_specs,
        out_specs=pl.BlockSpec((tm, d_model), row),
        out_shape=jax.ShapeDtypeStruct((m, d_model), F32),
        scratch_shapes=(
            [pltpu.VMEM((n_heads, tm, HEAD_DIM), F32) for _ in dilations]
            + [pltpu.VMEM((tm, LANES), F32) for _ in dilations] * 2
            + [pltpu.VMEM((tm, d_attn), BF16)]),
        compiler_params=pltpu.CompilerParams(
            dimension_semantics=("arbitrary",), vmem_limit_bytes=VMEM_LIMIT),
    )(*args)


def _ffn_kernel(*refs, final_norm):
    if final_norm:
        x_ref, g_ref, wg_ref, wu_ref, wd_ref, fg_ref, o_ref, h_scr = refs
    else:
        x_ref, g_ref, wg_ref, wu_ref, wd_ref, o_ref, h_scr = refs
    f = pl.program_id(1)

    @pl.when(f == 0)
    def _():
        x = x_ref[...]
        h_scr[...] = (x * _rms_scale(x) * g_ref[...]).astype(BF16)
        o_ref[...] = x

    h = h_scr[...]
    gate = jnp.dot(h, wg_ref[...], preferred_element_type=F32)
    up = jnp.dot(h, wu_ref[...], preferred_element_type=F32)
    ff = (gate * (1.0 / (1.0 + jnp.exp(-gate))) * up).astype(BF16)
    o_ref[...] += jnp.dot(ff, wd_ref[...], preferred_element_type=F32)

    if final_norm:
        @pl.when(f == pl.num_programs(1) - 1)
        def _():
            y = o_ref[...]
            o_ref[...] = y * _rms_scale(y) * fg_ref[...]


def _ffn(x2, g, w_gate, w_up, w_down, final_g, *, layer, tm, tf):
    m, d_model = x2.shape
    d_ff = w_gate.shape[2]
    final_norm = final_g is not None
    args = [x2, g, w_gate, w_up, w_down]
    in_specs = [
        pl.BlockSpec((tm, d_model), lambda i, f: (i, 0)),
        pl.BlockSpec((None, 1, d_model), lambda i, f: (layer, 0, 0)),
        pl.BlockSpec((None, d_model, tf), lambda i, f: (layer, 0, f)),
        pl.BlockSpec((None, d_model, tf), lambda i, f: (layer, 0, f)),
        pl.BlockSpec((None, tf, d_model), lambda i, f: (layer, f, 0)),
    ]
    if final_norm:
        args.append(final_g)
        in_specs.append(pl.BlockSpec((1, d_model), lambda i, f: (0, 0)))
    return pl.pallas_call(
        functools.partial(_ffn_kernel, final_norm=final_norm),
        grid=(m // tm, d_ff // tf),
        in_specs=in_specs,
        out_specs=pl.BlockSpec((tm, d_model), lambda i, f: (i, 0)),
        out_shape=jax.ShapeDtypeStruct((m, d_model), F32),
        scratch_shapes=[pltpu.VMEM((tm, d_model), BF16)],
        input_output_aliases={0: 0},
        compiler_params=pltpu.CompilerParams(
            dimension_semantics=("arbitrary", "arbitrary"), vmem_limit_bytes=VMEM_LIMIT),
    )(*args)


def _rope_tables(seq):
    pos = jnp.arange(seq, dtype=F32)
    inv = ROPE_THETA ** (-jnp.arange(0, HEAD_DIM, 2, dtype=F32) / HEAD_DIM)
    ang = pos[:, None] * inv[None, :]
    cos, sin = jnp.cos(ang), jnp.sin(ang)
    return jnp.concatenate([cos, cos], axis=-1), jnp.concatenate([-sin, sin], axis=-1)


def kernel(x, norm1_g, w_in, gmlp_ln_g, w_spatial, b_spatial, mix_norm_attn_g,
           mix_norm_gmlp_g, w_out, norm2_g, w_gate, w_up, w_down, final_g):
    b, s, d_model = x.shape
    depth = w_in.shape[0]
    d_gmlp = gmlp_ln_g.shape[1]
    d_attn = mix_norm_attn_g.shape[1]
    n_groups = w_spatial.shape[1]
    d_ff = w_gate.shape[2]
    assert w_in.shape[2] == 3 * d_attn + 2 * d_gmlp and n_groups * GMLP_GROUP == d_gmlp
    assert w_spatial.shape[2] == CHUNK
    assert DILATED_PATTERNS[0][1] == 1
    dilations = tuple(d for _, d in DILATED_PATTERNS[1:])

    tm_in, tm_out, tm_ffn, tf = 256, 512, 1024, 512
    assert s % tm_in == 0 and s % tm_out == 0 and s % CHUNK == 0 and d_ff % tf == 0

    cos2, sin2 = _rope_tables(s)
    x2 = x.reshape(b * s, d_model)
    vec3 = lambda a: a.reshape(a.shape[0], 1, a.shape[1])

    w_in_b, w_out_b = w_in.astype(BF16), w_out.astype(BF16)
    w_gate_b, w_up_b, w_down_b = w_gate.astype(BF16), w_up.astype(BF16), w_down.astype(BF16)
    ws_b = w_spatial.astype(BF16)
    bsb = jnp.repeat(jnp.swapaxes(b_spatial, 1, 2), GMLP_GROUP, axis=2)
    g1, lng, g2, ga, gf = (vec3(a) for a in (norm1_g, gmlp_ln_g, mix_norm_gmlp_g,
                                              mix_norm_attn_g, norm2_g))

    for l in range(depth):
        outs = _inproj(x2, g1, w_in_b, cos2, sin2, lng, ws_b, bsb, g2, layer=l, seq=s,
                       d_attn=d_attn, d_gmlp=d_gmlp, dilations=dilations, tm=tm_in)
        gn = outs[-1]
        branches = [
            _attn_branch(*outs[3 * i:3 * i + 3], batch=b, window=window, dilation=dilation)
            for i, (window, dilation) in enumerate(DILATED_PATTERNS)]
        x2 = _outproj(x2, branches, ga, gn, w_out_b, layer=l, dilations=dilations, tm=tm_out)
        x2 = _ffn(x2, gf, w_gate_b, w_up_b, w_down_b,
                  final_g.reshape(1, -1) if l == depth - 1 else None,
                  layer=l, tm=tm_ffn, tf=tf)
    return x2.reshape(b, s, d_model)
```

```python
import functools
import math

import jax
import jax.numpy as jnp
from jax import lax
from jax.experimental import pallas as pl
from jax.experimental.pallas import tpu as pltpu

HEAD_DIM = 128
GMLP_GROUP = 128
CHUNK = 128
DILATED_PATTERNS = ((128, 1), (512, 4), (2048, 16))
ROPE_THETA = 10000.0
EPS = 1e-6
NEG = -1e30

LANES = 128
MXU_COLS = 256
Q_TILE = 128
ATTN_TILES_PER_STEP = 8
VMEM_LIMIT = 56 * 1024 * 1024

BF16 = jnp.bfloat16
F32 = jnp.float32


def _gelu(x):
    return 0.5 * x * (1.0 + lax.erf(x * (1.0 / math.sqrt(2.0))))


def _rms_scale(x):
    return lax.rsqrt(jnp.mean(x * x, axis=-1, keepdims=True) + EPS)


def _head_cols(hh):
    return slice(hh * HEAD_DIM, (hh + 1) * HEAD_DIM)


def _interleave(major, minor):
    done = 0
    for j, task in enumerate(major):
        task()
        upto = (j + 1) * len(minor) // len(major)
        for other in minor[done:upto]:
            other()
        done = upto


def _inproj_kernel(*refs, d_attn, d_gmlp, q_scale, dilations, n_tiles):
    (x_ref, g1_ref, w_ref, cos_ref, sin_ref, lng_ref, ws_ref, bs_ref, g2_ref) = refs[:9]
    n_res_out = 3 * len(dilations)
    nat_refs = refs[9:12]
    res_refs = refs[12:12 + n_res_out]
    gn_ref = refs[12 + n_res_out]
    scratch = refs[13 + n_res_out:]
    stages = (scratch[0:4], scratch[4:8])
    h_scr, vln_scr, gate_scr = scratch[8:]

    tm = x_ref.shape[0]
    n_heads = d_attn // HEAD_DIM
    n_groups = d_gmlp // GMLP_GROUP
    heads_per_dot = MXU_COLS // HEAD_DIM

    def rope(t):
        return t * cos_ref[...] + pltpu.roll(t, HEAD_DIM // 2, axis=1) * sin_ref[...]

    head_fns = (lambda t: rope(t) * q_scale, rope, lambda t: t)

    def project_tasks(stage):
        slabs, uv_scr = stage[:3], stage[3]

        def norm():
            x = x_ref[...]
            h_scr[...] = (x * _rms_scale(x) * g1_ref[...]).astype(BF16)

        def project(col0):
            return jnp.dot(h_scr[...], w_ref[:, col0:col0 + MXU_COLS], preferred_element_type=F32)

        def uv_task(col):
            uv_scr[:, col:col + MXU_COLS] = project(3 * d_attn + col)

        def qkv_task(idx, first):
            proj = project(idx * d_attn + first * HEAD_DIM)
            for sub in range(heads_per_dot):
                val = head_fns[idx](proj[:, _head_cols(sub)])
                nat_refs[idx][:, _head_cols(first + sub)] = val.astype(BF16)
                slabs[idx][first + sub] = val

        tasks = [norm]
        tasks += [functools.partial(uv_task, col) for col in range(0, 2 * d_gmlp, MXU_COLS)]
        tasks += [functools.partial(qkv_task, idx, first) for idx in range(3)
                  for first in range(0, n_heads, heads_per_dot)]
        return tasks

    def finish_tasks(stage):
        slabs, uv_scr = stage[:3], stage[3]

        def copy_task(idx, d, out, r):
            for hh in range(n_heads):
                out[:, _head_cols(r * n_heads + hh)] = (
                    slabs[idx][hh, pl.ds(r, tm // d, stride=d), :].astype(BF16))

        def layernorm_task(rows):
            vv = _gelu(uv_scr[rows, d_gmlp:2 * d_gmlp])
            vc = vv - jnp.mean(vv, axis=-1, keepdims=True)
            vln = vc * lax.rsqrt(jnp.mean(vc * vc, axis=-1, keepdims=True) + EPS) * lng_ref[...]
            vln_scr[rows, :] = vln.astype(BF16)

        def mix_task(rows, g):
            cols = slice(g * GMLP_GROUP, (g + 1) * GMLP_GROUP)
            mixed = jnp.dot(ws_ref[g], vln_scr[rows, cols], preferred_element_type=F32)
            u = _gelu(uv_scr[rows, cols])
            gate_scr[rows, cols] = u * (mixed + bs_ref[:, cols])

        def gate_norm_task(rows):
            gate = gate_scr[rows, :]
            gn_ref[rows, :] = (gate * _rms_scale(gate) * g2_ref[...]).astype(BF16)

        tasks = []
        for c in range(tm // CHUNK):
            rows = slice(c * CHUNK, (c + 1) * CHUNK)
            tasks.append(functools.partial(layernorm_task, rows))
            tasks += [functools.partial(mix_task, rows, g) for g in range(n_groups)]
            tasks.append(functools.partial(gate_norm_task, rows))
        tasks += [functools.partial(copy_task, idx, d, out, r) for idx in range(3)
                  for d, out in zip(dilations, res_refs[idx::3]) for r in range(d)]
        return tasks

    i = pl.program_id(0)
    interior = (i > 0) & (i < n_tiles)

    @pl.when(i == 0)
    def _():
        _interleave(project_tasks(stages[0]), [])

    for parity in range(2):
        @pl.when(interior & (i % 2 == parity))
        def _():
            _interleave(project_tasks(stages[parity]), finish_tasks(stages[1 - parity]))

    @pl.when(i == n_tiles)
    def _():
        _interleave(finish_tasks(stages[(n_tiles - 1) % 2]), [])


def _inproj(x2, g1, w_in, cos2, sin2, lng, ws, bsb, g2, *, layer, seq, d_attn, d_gmlp,
            dilations, tm):
    m, d_model = x2.shape
    d_in = w_in.shape[2]
    n_groups = d_gmlp // GMLP_GROUP
    n_heads = d_attn // HEAD_DIM
    pos_blocks = seq // tm
    n_tiles = m // tm
    assert all(tm % (d * 16) == 0 for d in dilations)
    vec = lambda i: (layer, 0, 0)
    cur = lambda i: (jnp.minimum(i, n_tiles - 1), 0)
    lag = lambda i: (jnp.maximum(i - 1, 0), 0)
    pos = lambda i: (jnp.minimum(i, n_tiles - 1) % pos_blocks, 0)
    kern = functools.partial(_inproj_kernel, d_attn=d_attn, d_gmlp=d_gmlp,
                             q_scale=HEAD_DIM ** -0.5 * math.log2(math.e), dilations=dilations,
                             n_tiles=n_tiles)
    qkv_specs = [pl.BlockSpec((tm, d_attn), cur)] * 3
    qkv_shapes = [jax.ShapeDtypeStruct((m, d_attn), BF16)] * 3
    for d in dilations:
        qkv_specs += [pl.BlockSpec((tm // d, d * d_attn), lag)] * 3
        qkv_shapes += [jax.ShapeDtypeStruct((m // d, d * d_attn), BF16)] * 3
    stage = [pltpu.VMEM((n_heads, tm, HEAD_DIM), F32)] * 3 + [pltpu.VMEM((tm, 2 * d_gmlp), F32)]
    return pl.pallas_call(
        kern,
        grid=(n_tiles + 1,),
        in_specs=[
            pl.BlockSpec((tm, d_model), cur),
            pl.BlockSpec((None, 1, d_model), vec),
            pl.BlockSpec((None, d_model, d_in), vec, pipeline_mode=pl.Buffered(1)),
            pl.BlockSpec((tm, HEAD_DIM), pos),
            pl.BlockSpec((tm, HEAD_DIM), pos),
            pl.BlockSpec((None, 1, d_gmlp), vec),
            pl.BlockSpec((None, n_groups, CHUNK, CHUNK), lambda i: (layer, 0, 0, 0)),
            pl.BlockSpec((None, CHUNK, d_gmlp), vec),
            pl.BlockSpec((None, 1, d_gmlp), vec),
        ],
        out_specs=qkv_specs + [pl.BlockSpec((tm, d_gmlp), lag)],
        out_shape=qkv_shapes + [jax.ShapeDtypeStruct((m, d_gmlp), BF16)],
        scratch_shapes=stage * 2 + [
            pltpu.VMEM((tm, d_model), BF16),
            pltpu.VMEM((tm, d_gmlp), BF16),
            pltpu.VMEM((tm, d_gmlp), F32),
        ],
        compiler_params=pltpu.CompilerParams(
            dimension_semantics=("arbitrary",), vmem_limit_bytes=VMEM_LIMIT),
    )(x2, g1, w_in, cos2, sin2, lng, ws, bsb, g2)


def _attn_kernel(q_ref, kp_ref, k_ref, kn_ref, vp_ref, v_ref, vn_ref, o_ref, m_ref, d_ref,
                 *, n_side, n_heads):
    tl = q_ref.shape[1]
    halo = kp_ref.shape[1]
    n_tiles = tl // Q_TILE
    t = pl.program_id(2)
    first_block = t == 0
    last_block = t == pl.num_programs(2) - 1

    kw = Q_TILE + 2 * halo
    row_i = lax.broadcasted_iota(jnp.int32, (Q_TILE, kw), 0)
    col_j = lax.broadcasted_iota(jnp.int32, (Q_TILE, kw), 1)
    rel = col_j - row_i - halo
    band = (rel >= -n_side) & (rel <= n_side)
    head_ok = (col_j >= halo) | jnp.logical_not(first_block)
    tail_ok = (col_j < halo + Q_TILE) | jnp.logical_not(last_block)
    lane = lax.broadcasted_iota(jnp.int32, (Q_TILE, LANES), 1)
    ones = jnp.ones((kw, HEAD_DIM), BF16)

    def window(prev_ref, main_ref, next_ref, i, cols):
        lo, hi = i * Q_TILE - halo, (i + 1) * Q_TILE + halo
        parts = []
        if lo < 0:
            parts.append(prev_ref[0, :, cols])
        parts.append(main_ref[0, max(lo, 0):min(hi, tl), cols])
        if hi > tl:
            parts.append(next_ref[0, :, cols])
        return parts[0] if len(parts) == 1 else jnp.concatenate(parts, axis=0)

    biases = {}
    for i in range(n_tiles):
        mask = band
        if i == 0:
            mask = mask & head_ok
        if i == n_tiles - 1:
            mask = mask & tail_ok
        biases[i] = jnp.where(mask, 0.0, NEG)

    n_res = q_ref.shape[2] // (n_heads * HEAD_DIM)
    for res, i in ((res, i) for res in range(n_res) for i in range(n_tiles)):
        rows = slice(i * Q_TILE, (i + 1) * Q_TILE)
        bias = biases[i]
        m_tile = jnp.zeros((Q_TILE, LANES), F32)
        d_tile = jnp.ones((Q_TILE, LANES), F32)
        for hh in range(n_heads):
            cols = _head_cols(res * n_heads + hh)
            kh = window(kp_ref, k_ref, kn_ref, i, cols)
            vh = window(vp_ref, v_ref, vn_ref, i, cols)
            s = lax.dot_general(q_ref[0, rows, cols], kh, (((1,), (1,)), ((), ())),
                                preferred_element_type=F32) + bias
            mx = jnp.max(s, axis=-1, keepdims=True)
            p = jnp.exp2(s - mx).astype(BF16)
            ov = jnp.dot(p, jnp.concatenate([vh, ones], axis=1), preferred_element_type=F32)
            o_ref[0, rows, cols] = ov[:, :HEAD_DIM].astype(o_ref.dtype)
            m_tile = jnp.where(lane == hh, mx, m_tile)
            d_tile = jnp.where(lane == hh, ov[:, HEAD_DIM:], d_tile)
        m_ref[0, rows, res * LANES:(res + 1) * LANES] = m_tile
        d_ref[0, rows, res * LANES:(res + 1) * LANES] = d_tile


def _attn_branch(q, k, v, *, batch, window, dilation):
    rows_total, width = q.shape
    c = width // dilation
    sub_len = rows_total // batch
    n_heads = c // HEAD_DIM
    n_side = window // (2 * dilation)
    halo = n_side
    assert halo * 2 == Q_TILE and n_heads <= LANES and sub_len % Q_TILE == 0
    tl = min(sub_len, 1024)
    assert sub_len % tl == 0 and tl % halo == 0
    hb = tl // halo
    n_hb = sub_len // halo
    n_res = min(dilation, max(1, ATTN_TILES_PER_STEP * Q_TILE // tl))
    assert dilation % n_res == 0

    view = lambda a: a.reshape(batch, sub_len, a.shape[-1])
    qv, kv, vv = view(q), view(k), view(v)

    main = lambda bi, r, t: (bi, t, r)
    prev_halo = lambda bi, r, t: (bi, jnp.maximum(t * hb - 1, 0), r)
    next_halo = lambda bi, r, t: (bi, jnp.minimum((t + 1) * hb, n_hb - 1), r)
    big = pl.BlockSpec((1, tl, n_res * c), main)
    small_p = pl.BlockSpec((1, halo, n_res * c), prev_halo)
    small_n = pl.BlockSpec((1, halo, n_res * c), next_halo)
    stat_spec = pl.BlockSpec((1, tl, n_res * LANES), main)
    stat_shape = jax.ShapeDtypeStruct((batch, sub_len, dilation * LANES), F32)

    kern = functools.partial(_attn_kernel, n_side=n_side, n_heads=n_heads)
    o, mx, den = pl.pallas_call(
        kern,
        grid=(batch, dilation // n_res, sub_len // tl),
        in_specs=[big, small_p, big, small_n, small_p, big, small_n],
        out_specs=[big, stat_spec, stat_spec],
        out_shape=[jax.ShapeDtypeStruct((batch, sub_len, width), BF16), stat_shape, stat_shape],
        compiler_params=pltpu.CompilerParams(
            dimension_semantics=("arbitrary",) * 3, vmem_limit_bytes=VMEM_LIMIT),
    )(qv, kv, kv, kv, vv, vv, vv)
    flat = lambda a: a.reshape(rows_total, a.shape[-1])
    return flat(o), flat(mx), flat(den)


def _outproj_kernel(*refs, dilations, n_heads, n_tiles):
    n_pat = 1 + len(dilations)
    n_dil = len(dilations)
    x_ref = refs[0]
    o_refs = refs[1:1 + 3 * n_pat:3]
    m_refs = refs[2:2 + 3 * n_pat:3]
    d_refs = refs[3:3 + 3 * n_pat:3]
    ga_ref, gn_ref, w_ref, out_ref = refs[1 + 3 * n_pat:5 + 3 * n_pat]
    scratch = refs[5 + 3 * n_pat:]
    o_slabs = scratch[0:n_dil]
    m_slabs = scratch[n_dil:2 * n_dil]
    d_slabs = scratch[2 * n_dil:3 * n_dil]
    an_stages = scratch[3 * n_dil:3 * n_dil + 2]
    a_slab = scratch[3 * n_dil + 2]

    tm, d_model = x_ref.shape
    d_attn = n_heads * HEAD_DIM
    state = {}

    def unpermute_task(idx, d, r):
        dst = pl.ds(r, tm // d, stride=d)
        m_slabs[idx][dst, :] = m_refs[idx + 1][:, r * LANES:(r + 1) * LANES]
        d_slabs[idx][dst, :] = d_refs[idx + 1][:, r * LANES:(r + 1) * LANES]
        for hh in range(n_heads):
            o_slabs[idx][hh, dst, :] = o_refs[idx + 1][:, _head_cols(r * n_heads + hh)].astype(F32)

    def weights_task():
        mxs = [m_refs[0][...]] + [m_slab[...] for m_slab in m_slabs]
        dens = [d_refs[0][...]] + [d_slab[...] for d_slab in d_slabs]
        top = functools.reduce(jnp.maximum, mxs)
        es = [jnp.exp2(mx - top) for mx in mxs]
        inv = 1.0 / functools.reduce(jnp.add, [e * den for e, den in zip(es, dens)])
        state["ws"] = [e * inv for e in es]
        state["ssq"] = jnp.zeros((tm, 1), F32)

    def combine_task(hh):
        parts = [o_refs[0][:, _head_cols(hh)].astype(F32)] + [o_slab[hh] for o_slab in o_slabs]
        a = functools.reduce(jnp.add, [w[:, hh:hh + 1] * p for w, p in zip(state["ws"], parts)])
        state["ssq"] = state["ssq"] + jnp.sum(a * a, axis=-1, keepdims=True)
        a_slab[hh] = a

    def norm_task(an_scr, hh):
        if hh == 0:
            state["scale"] = lax.rsqrt(state["ssq"] / d_attn + EPS)
        an_scr[:, _head_cols(hh)] = (
            a_slab[hh] * state["scale"] * ga_ref[:, _head_cols(hh)]).astype(BF16)

    def merge_tasks(an_scr):
        state.clear()
        tasks = [functools.partial(unpermute_task, idx, d, r)
                 for idx, d in enumerate(dilations) for r in range(d)]
        tasks.append(weights_task)
        tasks += [functools.partial(combine_task, hh) for hh in range(n_heads)]
        tasks += [functools.partial(norm_task, an_scr, hh) for hh in range(n_heads)]
        return tasks

    def dot_tasks(an_scr):
        def dot_task(col):
            cols = slice(col, col + MXU_COLS)
            acc = jnp.dot(gn_ref[...], w_ref[d_attn:, cols], preferred_element_type=F32)
            acc = acc + jnp.dot(an_scr[...], w_ref[0:d_attn, cols], preferred_element_type=F32)
            out_ref[:, cols] = x_ref[:, cols] + acc

        return [functools.partial(dot_task, col) for col in range(0, d_model, MXU_COLS)]

    i = pl.program_id(0)
    interior = (i > 0) & (i < n_tiles)

    @pl.when(i == 0)
    def _():
        _interleave(merge_tasks(an_stages[0]), [])

    for parity in range(2):
        @pl.when(interior & (i % 2 == parity))
        def _():
            _interleave(dot_tasks(an_stages[1 - parity]), merge_tasks(an_stages[parity]))

    @pl.when(i == n_tiles)
    def _():
        _interleave(dot_tasks(an_stages[(n_tiles - 1) % 2]), [])


def _outproj(x2, branches, ga, gn, w_out, *, layer, dilations, tm):
    m, d_model = x2.shape
    d_attn = branches[0][0].shape[1]
    n_heads = d_attn // HEAD_DIM
    n_tiles = m // tm
    cur = lambda i: (jnp.minimum(i, n_tiles - 1), 0)
    lag = lambda i: (jnp.maximum(i - 1, 0), 0)
    args, in_specs = [x2], [pl.BlockSpec((tm, d_model), lag)]
    for d, branch in zip((1,) + tuple(dilations), branches):
        args += list(branch)
        in_specs += [pl.BlockSpec((tm // d, d * d_attn), cur)] + [
            pl.BlockSpec((tm // d, d * LANES), cur)] * 2
    args += [ga, gn, w_out]
    in_specs += [
        pl.BlockSpec((None, 1, d_attn), lambda i: (layer, 0, 0)),
        pl.BlockSpec((tm, gn.shape[1]), lag),
        pl.BlockSpec((None,) + w_out.shape[1:], lambda i: (layer, 0, 0),
                     pipeline_mode=pl.Buffered(1)),
    ]
    return pl.pallas_call(
        functools.partial(_outproj_kernel, dilations=dilations, n_heads=n_heads,
                          n_tiles=n_tiles),
        grid=(n_tiles + 1,),
        in_specs=in_specs,
        out_specs=pl.BlockSpec((tm, d_model), lag),
        out_shape=jax.ShapeDtypeStruct((m, d_model), F32),
        scratch_shapes=(
            [pltpu.VMEM((n_heads, tm, HEAD_DIM), F32) for _ in dilations]
            + [pltpu.VMEM((tm, LANES), F32) for _ in dilations] * 2
            + [pltpu.VMEM((tm, d_attn), BF16)] * 2
            + [pltpu.VMEM((n_heads, tm, HEAD_DIM), F32)]),
        compiler_params=pltpu.CompilerParams(
            dimension_semantics=("arbitrary",), vmem_limit_bytes=VMEM_LIMIT),
    )(*args)


def _ffn_kernel(*refs, final_norm):
    if final_norm:
        x_ref, g_ref, wg_ref, wu_ref, wd_ref, fg_ref, o_ref, h_scr = refs
    else:
        x_ref, g_ref, wg_ref, wu_ref, wd_ref, o_ref, h_scr = refs
    f = pl.program_id(1)

    @pl.when(f == 0)
    def _():
        x = x_ref[...]
        h_scr[...] = (x * _rms_scale(x) * g_ref[...]).astype(BF16)
        o_ref[...] = x

    h = h_scr[...]
    gate = jnp.dot(h, wg_ref[...], preferred_element_type=F32)
    up = jnp.dot(h, wu_ref[...], preferred_element_type=F32)
    ff = (gate * (1.0 / (1.0 + jnp.exp(-gate))) * up).astype(BF16)
    o_ref[...] += jnp.dot(ff, wd_ref[...], preferred_element_type=F32)

    if final_norm:
        @pl.when(f == pl.num_programs(1) - 1)
        def _():
            y = o_ref[...]
            o_ref[...] = y * _rms_scale(y) * fg_ref[...]


def _ffn(x2, g, w_gate, w_up, w_down, final_g, *, layer, tm, tf):
    m, d_model = x2.shape
    d_ff = w_gate.shape[2]
    final_norm = final_g is not None
    args = [x2, g, w_gate, w_up, w_down]
    in_specs = [
        pl.BlockSpec((tm, d_model), lambda i, f: (i, 0)),
        pl.BlockSpec((None, 1, d_model), lambda i, f: (layer, 0, 0)),
        pl.BlockSpec((None, d_model, tf), lambda i, f: (layer, 0, f)),
        pl.BlockSpec((None, d_model, tf), lambda i, f: (layer, 0, f)),
        pl.BlockSpec((None, tf, d_model), lambda i, f: (layer, f, 0)),
    ]
    if final_norm:
        args.append(final_g)
        in_specs.append(pl.BlockSpec((1, d_model), lambda i, f: (0, 0)))
    return pl.pallas_call(
        functools.partial(_ffn_kernel, final_norm=final_norm),
        grid=(m // tm, d_ff // tf),
        in_specs=in_specs,
        out_specs=pl.BlockSpec((tm, d_model), lambda i, f: (i, 0)),
        out_shape=jax.ShapeDtypeStruct((m, d_model), F32),
        scratch_shapes=[pltpu.VMEM((tm, d_model), BF16)],
        input_output_aliases={0: 0},
        compiler_params=pltpu.CompilerParams(
            dimension_semantics=("arbitrary", "arbitrary"), vmem_limit_bytes=VMEM_LIMIT),
    )(*args)


def _rope_tables(seq):
    pos = jnp.arange(seq, dtype=F32)
    inv = ROPE_THETA ** (-jnp.arange(0, HEAD_DIM, 2, dtype=F32) / HEAD_DIM)
    ang = pos[:, None] * inv[None, :]
    cos, sin = jnp.cos(ang), jnp.sin(ang)
    return jnp.concatenate([cos, cos], axis=-1), jnp.concatenate([-sin, sin], axis=-1)


def kernel(x, norm1_g, w_in, gmlp_ln_g, w_spatial, b_spatial, mix_norm_attn_g,
           mix_norm_gmlp_g, w_out, norm2_g, w_gate, w_up, w_down, final_g):
    b, s, d_model = x.shape
    depth = w_in.shape[0]
    d_gmlp = gmlp_ln_g.shape[1]
    d_attn = mix_norm_attn_g.shape[1]
    n_groups = w_spatial.shape[1]
    d_ff = w_gate.shape[2]
    assert w_in.shape[2] == 3 * d_attn + 2 * d_gmlp and n_groups * GMLP_GROUP == d_gmlp
    assert w_spatial.shape[2] == CHUNK
    assert DILATED_PATTERNS[0][1] == 1
    dilations = tuple(d for _, d in DILATED_PATTERNS[1:])

    tm_in, tm_out, tm_ffn, tf = 256, 512, 1024, 512
    assert s % tm_in == 0 and s % tm_out == 0 and s % CHUNK == 0 and d_ff % tf == 0

    cos2, sin2 = _rope_tables(s)
    x2 = x.reshape(b * s, d_model)
    vec3 = lambda a: a.reshape(a.shape[0], 1, a.shape[1])

    w_in_b, w_out_b = w_in.astype(BF16), w_out.astype(BF16)
    w_gate_b, w_up_b, w_down_b = w_gate.astype(BF16), w_up.astype(BF16), w_down.astype(BF16)
    ws_b = w_spatial.astype(BF16)
    bsb = jnp.repeat(jnp.swapaxes(b_spatial, 1, 2), GMLP_GROUP, axis=2)
    g1, lng, g2, ga, gf = (vec3(a) for a in (norm1_g, gmlp_ln_g, mix_norm_gmlp_g,
                                              mix_norm_attn_g, norm2_g))

    for l in range(depth):
        outs = _inproj(x2, g1, w_in_b, cos2, sin2, lng, ws_b, bsb, g2, layer=l, seq=s,
                       d_attn=d_attn, d_gmlp=d_gmlp, dilations=dilations, tm=tm_in)
        gn = outs[-1]
        branches = [
            _attn_branch(*outs[3 * i:3 * i + 3], batch=b, window=window, dilation=dilation)
            for i, (window, dilation) in enumerate(DILATED_PATTERNS)]
        x2 = _outproj(x2, branches, ga, gn, w_out_b, layer=l, dilations=dilations, tm=tm_out)
        x2 = _ffn(x2, gf, w_gate_b, w_up_b, w_down_b,
                  final_g.reshape(1, -1) if l == depth - 1 else None,
                  layer=l, tm=tm_ffn, tf=tf)
    return x2.reshape(b, s, d_model)
```

```python
import functools
import math

import jax
import jax.numpy as jnp
from jax import lax
from jax.experimental import pallas as pl
from jax.experimental.pallas import tpu as pltpu

HEAD_DIM = 128
GMLP_GROUP = 128
CHUNK = 128
DILATED_PATTERNS = ((128, 1), (512, 4), (2048, 16))
ROPE_THETA = 10000.0
EPS = 1e-6
NEG = -1e30

LANES = 128
MXU_COLS = 256
Q_TILE = 128
ATTN_TILES_PER_STEP = 16
VMEM_LIMIT = 56 * 1024 * 1024

BF16 = jnp.bfloat16
F32 = jnp.float32


def _gelu(x):
    return 0.5 * x * (1.0 + lax.erf(x * (1.0 / math.sqrt(2.0))))


def _rms_scale(x):
    return lax.rsqrt(jnp.mean(x * x, axis=-1, keepdims=True) + EPS)


def _head_cols(hh):
    return slice(hh * HEAD_DIM, (hh + 1) * HEAD_DIM)


def _interleave(major, minor):
    done = 0
    for j, task in enumerate(major):
        task()
        upto = (j + 1) * len(minor) // len(major)
        for other in minor[done:upto]:
            other()
        done = upto


def _inproj_kernel(*refs, d_attn, d_gmlp, q_scale, dilations, n_tiles):
    (x_ref, g1_ref, w_ref, cos_ref, sin_ref, lng_ref, ws_ref, bs_ref, g2_ref) = refs[:9]
    n_res_out = 3 * len(dilations)
    nat_refs = refs[9:12]
    res_refs = refs[12:12 + n_res_out]
    gn_ref = refs[12 + n_res_out]
    scratch = refs[13 + n_res_out:]
    stages = (scratch[0:4], scratch[4:8])
    h_scr, vln_scr, gate_scr = scratch[8:]

    tm = x_ref.shape[0]
    n_heads = d_attn // HEAD_DIM
    n_groups = d_gmlp // GMLP_GROUP
    heads_per_dot = MXU_COLS // HEAD_DIM

    def rope(t):
        return t * cos_ref[...] + pltpu.roll(t, HEAD_DIM // 2, axis=1) * sin_ref[...]

    head_fns = (lambda t: rope(t) * q_scale, rope, lambda t: t)

    def project_tasks(stage):
        slabs, uv_scr = stage[:3], stage[3]

        def norm():
            x = x_ref[...]
            h_scr[...] = (x * _rms_scale(x) * g1_ref[...]).astype(BF16)

        def project(col0):
            return jnp.dot(h_scr[...], w_ref[:, col0:col0 + MXU_COLS], preferred_element_type=F32)

        def uv_task(col):
            uv_scr[:, col:col + MXU_COLS] = project(3 * d_attn + col)

        def qkv_task(idx, first):
            proj = project(idx * d_attn + first * HEAD_DIM)
            for sub in range(heads_per_dot):
                val = head_fns[idx](proj[:, _head_cols(sub)])
                nat_refs[idx][:, _head_cols(first + sub)] = val.astype(BF16)
                slabs[idx][first + sub] = val

        tasks = [norm]
        tasks += [functools.partial(uv_task, col) for col in range(0, 2 * d_gmlp, MXU_COLS)]
        tasks += [functools.partial(qkv_task, idx, first) for idx in range(3)
                  for first in range(0, n_heads, heads_per_dot)]
        return tasks

    def finish_tasks(stage):
        slabs, uv_scr = stage[:3], stage[3]

        def copy_task(idx, d, out, r):
            for hh in range(n_heads):
                out[:, _head_cols(r * n_heads + hh)] = (
                    slabs[idx][hh, pl.ds(r, tm // d, stride=d), :].astype(BF16))

        def layernorm_task(rows):
            vv = _gelu(uv_scr[rows, d_gmlp:2 * d_gmlp])
            vc = vv - jnp.mean(vv, axis=-1, keepdims=True)
            vln = vc * lax.rsqrt(jnp.mean(vc * vc, axis=-1, keepdims=True) + EPS) * lng_ref[...]
            vln_scr[rows, :] = vln.astype(BF16)

        def mix_task(rows, g):
            cols = slice(g * GMLP_GROUP, (g + 1) * GMLP_GROUP)
            mixed = jnp.dot(ws_ref[g], vln_scr[rows, cols], preferred_element_type=F32)
            u = _gelu(uv_scr[rows, cols])
            gate_scr[rows, cols] = u * (mixed + bs_ref[:, cols])

        def gate_norm_task(rows):
            gate = gate_scr[rows, :]
            gn_ref[rows, :] = (gate * _rms_scale(gate) * g2_ref[...]).astype(BF16)

        tasks = []
        for c in range(tm // CHUNK):
            rows = slice(c * CHUNK, (c + 1) * CHUNK)
            tasks.append(functools.partial(layernorm_task, rows))
            tasks += [functools.partial(mix_task, rows, g) for g in range(n_groups)]
            tasks.append(functools.partial(gate_norm_task, rows))
        tasks += [functools.partial(copy_task, idx, d, out, r) for idx in range(3)
                  for d, out in zip(dilations, res_refs[idx::3]) for r in range(d)]
        return tasks

    i = pl.program_id(0)
    interior = (i > 0) & (i < n_tiles)

    @pl.when(i == 0)
    def _():
        _interleave(project_tasks(stages[0]), [])

    for parity in range(2):
        @pl.when(interior & (i % 2 == parity))
        def _():
            _interleave(project_tasks(stages[parity]), finish_tasks(stages[1 - parity]))

    @pl.when(i == n_tiles)
    def _():
        _interleave(finish_tasks(stages[(n_tiles - 1) % 2]), [])


def _inproj(x2, g1, w_in, cos2, sin2, lng, ws, bsb, g2, *, layer, seq, d_attn, d_gmlp,
            dilations, tm):
    m, d_model = x2.shape
    d_in = w_in.shape[2]
    n_groups = d_gmlp // GMLP_GROUP
    n_heads = d_attn // HEAD_DIM
    pos_blocks = seq // tm
    n_tiles = m // tm
    assert all(tm % (d * 16) == 0 for d in dilations)
    vec = lambda i: (layer, 0, 0)
    cur = lambda i: (jnp.minimum(i, n_tiles - 1), 0)
    lag = lambda i: (jnp.maximum(i - 1, 0), 0)
    pos = lambda i: (jnp.minimum(i, n_tiles - 1) % pos_blocks, 0)
    kern = functools.partial(_inproj_kernel, d_attn=d_attn, d_gmlp=d_gmlp,
                             q_scale=HEAD_DIM ** -0.5 * math.log2(math.e), dilations=dilations,
                             n_tiles=n_tiles)
    qkv_specs = [pl.BlockSpec((tm, d_attn), cur)] * 3
    qkv_shapes = [jax.ShapeDtypeStruct((m, d_attn), BF16)] * 3
    for d in dilations:
        qkv_specs += [pl.BlockSpec((tm // d, d * d_attn), lag)] * 3
        qkv_shapes += [jax.ShapeDtypeStruct((m // d, d * d_attn), BF16)] * 3
    stage = [pltpu.VMEM((n_heads, tm, HEAD_DIM), F32)] * 3 + [pltpu.VMEM((tm, 2 * d_gmlp), F32)]
    return pl.pallas_call(
        kern,
        grid=(n_tiles + 1,),
        in_specs=[
            pl.BlockSpec((tm, d_model), cur),
            pl.BlockSpec((None, 1, d_model), vec),
            pl.BlockSpec((None, d_model, d_in), vec, pipeline_mode=pl.Buffered(1)),
            pl.BlockSpec((tm, HEAD_DIM), pos),
            pl.BlockSpec((tm, HEAD_DIM), pos),
            pl.BlockSpec((None, 1, d_gmlp), vec),
            pl.BlockSpec((None, n_groups, CHUNK, CHUNK), lambda i: (layer, 0, 0, 0)),
            pl.BlockSpec((None, CHUNK, d_gmlp), vec),
            pl.BlockSpec((None, 1, d_gmlp), vec),
        ],
        out_specs=qkv_specs + [pl.BlockSpec((tm, d_gmlp), lag)],
        out_shape=qkv_shapes + [jax.ShapeDtypeStruct((m, d_gmlp), BF16)],
        scratch_shapes=stage * 2 + [
            pltpu.VMEM((tm, d_model), BF16),
            pltpu.VMEM((tm, d_gmlp), BF16),
            pltpu.VMEM((tm, d_gmlp), F32),
        ],
        compiler_params=pltpu.CompilerParams(
            dimension_semantics=("arbitrary",), vmem_limit_bytes=VMEM_LIMIT),
    )(x2, g1, w_in, cos2, sin2, lng, ws, bsb, g2)


def _attn_kernel(q_ref, kp_ref, k_ref, kn_ref, vp_ref, v_ref, vn_ref, o_ref, s_ref,
                 *, n_side, n_heads):
    tl = q_ref.shape[1]
    halo = kp_ref.shape[1]
    n_tiles = tl // Q_TILE
    t = pl.program_id(2)
    first_block = t == 0
    last_block = t == pl.num_programs(2) - 1

    kw = Q_TILE + 2 * halo
    row_i = lax.broadcasted_iota(jnp.int32, (Q_TILE, kw), 0)
    col_j = lax.broadcasted_iota(jnp.int32, (Q_TILE, kw), 1)
    rel = col_j - row_i - halo
    band = (rel >= -n_side) & (rel <= n_side)
    head_ok = (col_j >= halo) | jnp.logical_not(first_block)
    tail_ok = (col_j < halo + Q_TILE) | jnp.logical_not(last_block)
    lane = lax.broadcasted_iota(jnp.int32, (Q_TILE, LANES), 1)
    ones = jnp.ones((kw, HEAD_DIM), BF16)

    def window(prev_ref, main_ref, next_ref, i, cols):
        lo, hi = i * Q_TILE - halo, (i + 1) * Q_TILE + halo
        parts = []
        if lo < 0:
            parts.append(prev_ref[0, :, cols])
        parts.append(main_ref[0, max(lo, 0):min(hi, tl), cols])
        if hi > tl:
            parts.append(next_ref[0, :, cols])
        return parts[0] if len(parts) == 1 else jnp.concatenate(parts, axis=0)

    biases = {}
    for i in range(n_tiles):
        mask = band
        if i == 0:
            mask = mask & head_ok
        if i == n_tiles - 1:
            mask = mask & tail_ok
        biases[i] = jnp.where(mask, 0.0, NEG)

    n_res = q_ref.shape[2] // (n_heads * HEAD_DIM)
    for res, i in ((res, i) for res in range(n_res) for i in range(n_tiles)):
        rows = slice(i * Q_TILE, (i + 1) * Q_TILE)
        bias = biases[i]
        stats = jnp.ones((Q_TILE, LANES), F32)
        for hh in range(n_heads):
            cols = _head_cols(res * n_heads + hh)
            kh = window(kp_ref, k_ref, kn_ref, i, cols)
            vh = window(vp_ref, v_ref, vn_ref, i, cols)
            s = lax.dot_general(q_ref[0, rows, cols], kh, (((1,), (1,)), ((), ())),
                                preferred_element_type=F32) + bias
            mx = jnp.max(s, axis=-1, keepdims=True)
            p = jnp.exp2(s - mx).astype(BF16)
            ov = jnp.dot(p, jnp.concatenate([vh, ones], axis=1), preferred_element_type=F32)
            o_ref[0, rows, cols] = ov[:, :HEAD_DIM].astype(o_ref.dtype)
            stats = jnp.where(lane == hh, mx, stats)
            stats = jnp.where(lane == n_heads + hh, ov[:, HEAD_DIM:], stats)
        s_ref[0, rows, res * LANES:(res + 1) * LANES] = stats


def _attn_branch(q, k, v, *, batch, window, dilation):
    rows_total, width = q.shape
    c = width // dilation
    sub_len = rows_total // batch
    n_heads = c // HEAD_DIM
    n_side = window // (2 * dilation)
    halo = n_side
    assert halo * 2 == Q_TILE and 2 * n_heads <= LANES and sub_len % Q_TILE == 0
    tl = min(sub_len, ATTN_TILES_PER_STEP * Q_TILE)
    assert sub_len % tl == 0 and tl % halo == 0
    hb = tl // halo
    n_hb = sub_len // halo
    n_res = min(dilation, max(1, ATTN_TILES_PER_STEP * Q_TILE // tl))
    assert dilation % n_res == 0

    view = lambda a: a.reshape(batch, sub_len, a.shape[-1])
    qv, kv, vv = view(q), view(k), view(v)

    main = lambda bi, r, t: (bi, t, r)
    prev_halo = lambda bi, r, t: (bi, jnp.maximum(t * hb - 1, 0), r)
    next_halo = lambda bi, r, t: (bi, jnp.minimum((t + 1) * hb, n_hb - 1), r)
    big = pl.BlockSpec((1, tl, n_res * c), main)
    small_p = pl.BlockSpec((1, halo, n_res * c), prev_halo)
    small_n = pl.BlockSpec((1, halo, n_res * c), next_halo)
    stat_spec = pl.BlockSpec((1, tl, n_res * LANES), main)
    stat_shape = jax.ShapeDtypeStruct((batch, sub_len, dilation * LANES), F32)

    kern = functools.partial(_attn_kernel, n_side=n_side, n_heads=n_heads)
    o, stats = pl.pallas_call(
        kern,
        grid=(batch, dilation // n_res, sub_len // tl),
        in_specs=[big, small_p, big, small_n, small_p, big, small_n],
        out_specs=[big, stat_spec],
        out_shape=[jax.ShapeDtypeStruct((batch, sub_len, width), BF16), stat_shape],
        compiler_params=pltpu.CompilerParams(
            dimension_semantics=("arbitrary",) * 3, vmem_limit_bytes=VMEM_LIMIT),
    )(qv, kv, kv, kv, vv, vv, vv)
    flat = lambda a: a.reshape(rows_total, a.shape[-1])
    return flat(o), flat(stats)


def _outproj_kernel(*refs, dilations, n_heads, n_tiles):
    n_pat = 1 + len(dilations)
    n_dil = len(dilations)
    x_ref = refs[0]
    o_refs = refs[1:1 + 2 * n_pat:2]
    s_refs = refs[2:2 + 2 * n_pat:2]
    ga_ref, gn_ref, w_ref, out_ref = refs[1 + 2 * n_pat:5 + 2 * n_pat]
    scratch = refs[5 + 2 * n_pat:]
    o_slabs = scratch[0:n_dil]
    s_slabs = scratch[n_dil:2 * n_dil]
    an_stages = scratch[2 * n_dil:2 * n_dil + 2]
    a_slab = scratch[2 * n_dil + 2]

    tm, d_model = x_ref.shape
    d_attn = n_heads * HEAD_DIM
    state = {}

    def unpermute_task(idx, d, r):
        dst = pl.ds(r, tm // d, stride=d)
        s_slabs[idx][dst, :] = s_refs[idx + 1][:, r * LANES:(r + 1) * LANES]
        for hh in range(n_heads):
            o_slabs[idx][hh, dst, :] = o_refs[idx + 1][:, _head_cols(r * n_heads + hh)].astype(F32)

    def weights_task():
        mxs = [s_refs[0][...]] + [s_slab[...] for s_slab in s_slabs]
        dens = [pltpu.roll(stats, LANES - n_heads, axis=1) for stats in mxs]
        top = functools.reduce(jnp.maximum, mxs)
        es = [jnp.exp2(mx - top) for mx in mxs]
        inv = 1.0 / functools.reduce(jnp.add, [e * den for e, den in zip(es, dens)])
        state["ws"] = [e * inv for e in es]
        state["ssq"] = jnp.zeros((tm, 1), F32)

    def combine_task(hh):
        parts = [o_refs[0][:, _head_cols(hh)].astype(F32)] + [o_slab[hh] for o_slab in o_slabs]
        a = functools.reduce(jnp.add, [w[:, hh:hh + 1] * p for w, p in zip(state["ws"], parts)])
        state["ssq"] = state["ssq"] + jnp.sum(a * a, axis=-1, keepdims=True)
        a_slab[hh] = a

    def norm_task(an_scr, hh):
        if hh == 0:
            state["scale"] = lax.rsqrt(state["ssq"] / d_attn + EPS)
        an_scr[:, _head_cols(hh)] = (
            a_slab[hh] * state["scale"] * ga_ref[:, _head_cols(hh)]).astype(BF16)

    def merge_tasks(an_scr):
        state.clear()
        tasks = [functools.partial(unpermute_task, idx, d, r)
                 for idx, d in enumerate(dilations) for r in range(d)]
        tasks.append(weights_task)
        tasks += [functools.partial(combine_task, hh) for hh in range(n_heads)]
        tasks += [functools.partial(norm_task, an_scr, hh) for hh in range(n_heads)]
        return tasks

    def dot_tasks(an_scr):
        def dot_task(col):
            cols = slice(col, col + MXU_COLS)
            acc = jnp.dot(gn_ref[...], w_ref[d_attn:, cols], preferred_element_type=F32)
            acc = acc + jnp.dot(an_scr[...], w_ref[0:d_attn, cols], preferred_element_type=F32)
            out_ref[:, cols] = x_ref[:, cols] + acc

        return [functools.partial(dot_task, col) for col in range(0, d_model, MXU_COLS)]

    i = pl.program_id(0)
    interior = (i > 0) & (i < n_tiles)

    @pl.when(i == 0)
    def _():
        _interleave(merge_tasks(an_stages[0]), [])

    for parity in range(2):
        @pl.when(interior & (i % 2 == parity))
        def _():
            _interleave(dot_tasks(an_stages[1 - parity]), merge_tasks(an_stages[parity]))

    @pl.when(i == n_tiles)
    def _():
        _interleave(dot_tasks(an_stages[(n_tiles - 1) % 2]), [])


def _outproj(x2, branches, ga, gn, w_out, *, layer, dilations, tm):
    m, d_model = x2.shape
    d_attn = branches[0][0].shape[1]
    n_heads = d_attn // HEAD_DIM
    n_tiles = m // tm
    cur = lambda i: (jnp.minimum(i, n_tiles - 1), 0)
    lag = lambda i: (jnp.maximum(i - 1, 0), 0)
    args, in_specs = [x2], [pl.BlockSpec((tm, d_model), lag)]
    for d, branch in zip((1,) + tuple(dilations), branches):
        args += list(branch)
        in_specs += [pl.BlockSpec((tm // d, d * d_attn), cur),
                     pl.BlockSpec((tm // d, d * LANES), cur)]
    args += [ga, gn, w_out]
    in_specs += [
        pl.BlockSpec((None, 1, d_attn), lambda i: (layer, 0, 0)),
        pl.BlockSpec((tm, gn.shape[1]), lag),
        pl.BlockSpec((None,) + w_out.shape[1:], lambda i: (layer, 0, 0),
                     pipeline_mode=pl.Buffered(1)),
    ]
    return pl.pallas_call(
        functools.partial(_outproj_kernel, dilations=dilations, n_heads=n_heads,
                          n_tiles=n_tiles),
        grid=(n_tiles + 1,),
        in_specs=in_specs,
        out_specs=pl.BlockSpec((tm, d_model), lag),
        out_shape=jax.ShapeDtypeStruct((m, d_model), F32),
        scratch_shapes=(
            [pltpu.VMEM((n_heads, tm, HEAD_DIM), F32) for _ in dilations]
            + [pltpu.VMEM((tm, LANES), F32) for _ in dilations]
            + [pltpu.VMEM((tm, d_attn), BF16)] * 2
            + [pltpu.VMEM((n_heads, tm, HEAD_DIM), F32)]),
        compiler_params=pltpu.CompilerParams(
            dimension_semantics=("arbitrary",), vmem_limit_bytes=VMEM_LIMIT),
    )(*args)


def _ffn_kernel(*refs, final_norm):
    if final_norm:
        x_ref, g_ref, wg_ref, wu_ref, wd_ref, fg_ref, o_ref, h_scr = refs
    else:
        x_ref, g_ref, wg_ref, wu_ref, wd_ref, o_ref, h_scr = refs
    f = pl.program_id(1)

    @pl.when(f == 0)
    def _():
        x = x_ref[...]
        h_scr[...] = (x * _rms_scale(x) * g_ref[...]).astype(BF16)
        o_ref[...] = x

    h = h_scr[...]
    gate = jnp.dot(h, wg_ref[...], preferred_element_type=F32)
    up = jnp.dot(h, wu_ref[...], preferred_element_type=F32)
    ff = (gate * (1.0 / (1.0 + jnp.exp(-gate))) * up).astype(BF16)
    o_ref[...] += jnp.dot(ff, wd_ref[...], preferred_element_type=F32)

    if final_norm:
        @pl.when(f == pl.num_programs(1) - 1)
        def _():
            y = o_ref[...]
            o_ref[...] = y * _rms_scale(y) * fg_ref[...]


def _ffn(x2, g, w_gate, w_up, w_down, final_g, *, layer, tm, tf):
    m, d_model = x2.shape
    d_ff = w_gate.shape[2]
    final_norm = final_g is not None
    args = [x2, g, w_gate, w_up, w_down]
    in_specs = [
        pl.BlockSpec((tm, d_model), lambda i, f: (i, 0)),
        pl.BlockSpec((None, 1, d_model), lambda i, f: (layer, 0, 0)),
        pl.BlockSpec((None, d_model, tf), lambda i, f: (layer, 0, f)),
        pl.BlockSpec((None, d_model, tf), lambda i, f: (layer, 0, f)),
        pl.BlockSpec((None, tf, d_model), lambda i, f: (layer, f, 0)),
    ]
    if final_norm:
        args.append(final_g)
        in_specs.append(pl.BlockSpec((1, d_model), lambda i, f: (0, 0)))
    return pl.pallas_call(
        functools.partial(_ffn_kernel, final_norm=final_norm),
        grid=(m // tm, d_ff // tf),
        in_specs=in_specs,
        out_specs=pl.BlockSpec((tm, d_model), lambda i, f: (i, 0)),
        out_shape=jax.ShapeDtypeStruct((m, d_model), F32),
        scratch_shapes=[pltpu.VMEM((tm, d_model), BF16)],
        input_output_aliases={0: 0},
        compiler_params=pltpu.CompilerParams(
            dimension_semantics=("arbitrary", "arbitrary"), vmem_limit_bytes=VMEM_LIMIT),
    )(*args)


def _rope_tables(seq):
    pos = jnp.arange(seq, dtype=F32)
    inv = ROPE_THETA ** (-jnp.arange(0, HEAD_DIM, 2, dtype=F32) / HEAD_DIM)
    ang = pos[:, None] * inv[None, :]
    cos, sin = jnp.cos(ang), jnp.sin(ang)
    return jnp.concatenate([cos, cos], axis=-1), jnp.concatenate([-sin, sin], axis=-1)


def kernel(x, norm1_g, w_in, gmlp_ln_g, w_spatial, b_spatial, mix_norm_attn_g,
           mix_norm_gmlp_g, w_out, norm2_g, w_gate, w_up, w_down, final_g):
    b, s, d_model = x.shape
    depth = w_in.shape[0]
    d_gmlp = gmlp_ln_g.shape[1]
    d_attn = mix_norm_attn_g.shape[1]
    n_groups = w_spatial.shape[1]
    d_ff = w_gate.shape[2]
    assert w_in.shape[2] == 3 * d_attn + 2 * d_gmlp and n_groups * GMLP_GROUP == d_gmlp
    assert w_spatial.shape[2] == CHUNK
    assert DILATED_PATTERNS[0][1] == 1
    dilations = tuple(d for _, d in DILATED_PATTERNS[1:])

    tm_in, tm_out, tm_ffn, tf = 256, 512, 1024, 512
    assert s % tm_in == 0 and s % tm_out == 0 and s % CHUNK == 0 and d_ff % tf == 0

    cos2, sin2 = _rope_tables(s)
    x2 = x.reshape(b * s, d_model)
    vec3 = lambda a: a.reshape(a.shape[0], 1, a.shape[1])

    w_in_b, w_out_b = w_in.astype(BF16), w_out.astype(BF16)
    w_gate_b, w_up_b, w_down_b = w_gate.astype(BF16), w_up.astype(BF16), w_down.astype(BF16)
    ws_b = w_spatial.astype(BF16)
    bsb = jnp.repeat(jnp.swapaxes(b_spatial, 1, 2), GMLP_GROUP, axis=2)
    g1, lng, g2, ga, gf = (vec3(a) for a in (norm1_g, gmlp_ln_g, mix_norm_gmlp_g,
                                              mix_norm_attn_g, norm2_g))

    for l in range(depth):
        outs = _inproj(x2, g1, w_in_b, cos2, sin2, lng, ws_b, bsb, g2, layer=l, seq=s,
                       d_attn=d_attn, d_gmlp=d_gmlp, dilations=dilations, tm=tm_in)
        gn = outs[-1]
        branches = [
            _attn_branch(*outs[3 * i:3 * i + 3], batch=b, window=window, dilation=dilation)
            for i, (window, dilation) in enumerate(DILATED_PATTERNS)]
        x2 = _outproj(x2, branches, ga, gn, w_out_b, layer=l, dilations=dilations, tm=tm_out)
        x2 = _ffn(x2, gf, w_gate_b, w_up_b, w_down_b,
                  final_g.reshape(1, -1) if l == depth - 1 else None,
                  layer=l, tm=tm_ffn, tf=tf)
    return x2.reshape(b, s, d_model)
```

```python
import functools
import math

import jax
import jax.numpy as jnp
from jax import lax
from jax.experimental import pallas as pl
from jax.experimental.pallas import tpu as pltpu

HEAD_DIM = 128
GMLP_GROUP = 128
CHUNK = 128
DILATED_PATTERNS = ((128, 1), (512, 4), (2048, 16))
ROPE_THETA = 10000.0
EPS = 1e-6
NEG = -1e30

LANES = 128
MXU_COLS = 256
Q_TILE = 128
ATTN_TILES_PER_STEP = 16
VMEM_LIMIT = 56 * 1024 * 1024

BF16 = jnp.bfloat16
F32 = jnp.float32


def _gelu(x):
    return 0.5 * x * (1.0 + lax.erf(x * (1.0 / math.sqrt(2.0))))


def _rms_scale(x):
    return lax.rsqrt(jnp.mean(x * x, axis=-1, keepdims=True) + EPS)


def _head_cols(hh):
    return slice(hh * HEAD_DIM, (hh + 1) * HEAD_DIM)


def _interleave(major, minor):
    done = 0
    for j, task in enumerate(major):
        task()
        upto = (j + 1) * len(minor) // len(major)
        for other in minor[done:upto]:
            other()
        done = upto


def _inproj_kernel(*refs, d_attn, d_gmlp, q_scale, dilations, n_tiles):
    (x_ref, g1_ref, w_ref, cos_ref, sin_ref, lng_ref, ws_ref, bs_ref, g2_ref) = refs[:9]
    n_res_out = 3 * len(dilations)
    nat_refs = refs[9:12]
    res_refs = refs[12:12 + n_res_out]
    gn_ref = refs[12 + n_res_out]
    scratch = refs[13 + n_res_out:]
    stages = (scratch[0:4], scratch[4:8])
    h_scr, vln_scr, gate_scr = scratch[8:11]
    mids = [scratch[11 + 3 * level:14 + 3 * level] for level in range(len(dilations) - 1)]

    tm = x_ref.shape[0]
    n_heads = d_attn // HEAD_DIM
    n_groups = d_gmlp // GMLP_GROUP
    heads_per_dot = MXU_COLS // HEAD_DIM

    def rope(t):
        return t * cos_ref[...] + pltpu.roll(t, HEAD_DIM // 2, axis=1) * sin_ref[...]

    head_fns = (lambda t: rope(t) * q_scale, rope, lambda t: t)

    def project_tasks(stage):
        slabs, uv_scr = stage[:3], stage[3]

        def norm():
            x = x_ref[...]
            h_scr[...] = (x * _rms_scale(x) * g1_ref[...]).astype(BF16)

        def project(col0):
            return jnp.dot(h_scr[...], w_ref[:, col0:col0 + MXU_COLS], preferred_element_type=F32)

        def uv_task(col):
            uv_scr[:, col:col + MXU_COLS] = project(3 * d_attn + col)

        def qkv_task(idx, first):
            proj = project(idx * d_attn + first * HEAD_DIM)
            for sub in range(heads_per_dot):
                val = head_fns[idx](proj[:, _head_cols(sub)])
                nat_refs[idx][:, _head_cols(first + sub)] = val.astype(BF16)
                slabs[idx][first + sub] = val

        tasks = [norm]
        tasks += [functools.partial(uv_task, col) for col in range(0, 2 * d_gmlp, MXU_COLS)]
        tasks += [functools.partial(qkv_task, idx, first) for idx in range(3)
                  for first in range(0, n_heads, heads_per_dot)]
        return tasks

    def finish_tasks(stage):
        slabs, uv_scr = stage[:3], stage[3]

        def copy_task(idx, level, r):
            d = dilations[level]
            d_prev = dilations[level - 1] if level else 1
            n = tm // d
            for hh in range(n_heads):
                if level:
                    start = (r % d_prev) * (tm // d_prev) + r // d_prev
                    val = mids[level - 1][idx][hh, pl.ds(start, n, stride=d // d_prev), :]
                else:
                    val = slabs[idx][hh, pl.ds(r, n, stride=d), :]
                res_refs[3 * level + idx][:, _head_cols(r * n_heads + hh)] = val.astype(BF16)
                if level + 1 < len(dilations):
                    mids[level][idx][hh, r * n:(r + 1) * n, :] = val

        def layernorm_task(rows):
            vv = _gelu(uv_scr[rows, d_gmlp:2 * d_gmlp])
            vc = vv - jnp.mean(vv, axis=-1, keepdims=True)
            vln = vc * lax.rsqrt(jnp.mean(vc * vc, axis=-1, keepdims=True) + EPS) * lng_ref[...]
            vln_scr[rows, :] = vln.astype(BF16)

        def mix_task(rows, g):
            cols = slice(g * GMLP_GROUP, (g + 1) * GMLP_GROUP)
            mixed = jnp.dot(ws_ref[g], vln_scr[rows, cols], preferred_element_type=F32)
            u = _gelu(uv_scr[rows, cols])
            gate_scr[rows, cols] = u * (mixed + bs_ref[:, cols])

        def gate_norm_task(rows):
            gate = gate_scr[rows, :]
            gn_ref[rows, :] = (gate * _rms_scale(gate) * g2_ref[...]).astype(BF16)

        tasks = []
        for c in range(tm // CHUNK):
            rows = slice(c * CHUNK, (c + 1) * CHUNK)
            tasks.append(functools.partial(layernorm_task, rows))
            tasks += [functools.partial(mix_task, rows, g) for g in range(n_groups)]
            tasks.append(functools.partial(gate_norm_task, rows))
        tasks += [functools.partial(copy_task, idx, level, r) for level, d in enumerate(dilations)
                  for idx in range(3) for r in range(d)]
        return tasks

    i = pl.program_id(0)
    interior = (i > 0) & (i < n_tiles)

    @pl.when(i == 0)
    def _():
        _interleave(project_tasks(stages[0]), [])

    for parity in range(2):
        @pl.when(interior & (i % 2 == parity))
        def _():
            _interleave(project_tasks(stages[parity]), finish_tasks(stages[1 - parity]))

    @pl.when(i == n_tiles)
    def _():
        _interleave(finish_tasks(stages[(n_tiles - 1) % 2]), [])


def _inproj(x2, g1, w_in, cos2, sin2, lng, ws, bsb, g2, *, layer, seq, d_attn, d_gmlp,
            dilations, tm):
    m, d_model = x2.shape
    d_in = w_in.shape[2]
    n_groups = d_gmlp // GMLP_GROUP
    n_heads = d_attn // HEAD_DIM
    pos_blocks = seq // tm
    n_tiles = m // tm
    assert all(tm % (d * 16) == 0 for d in dilations)
    assert all(d % d_prev == 0 for d, d_prev in zip(dilations, (1,) + tuple(dilations)))
    vec = lambda i: (layer, 0, 0)
    cur = lambda i: (jnp.minimum(i, n_tiles - 1), 0)
    lag = lambda i: (jnp.maximum(i - 1, 0), 0)
    pos = lambda i: (jnp.minimum(i, n_tiles - 1) % pos_blocks, 0)
    kern = functools.partial(_inproj_kernel, d_attn=d_attn, d_gmlp=d_gmlp,
                             q_scale=HEAD_DIM ** -0.5 * math.log2(math.e), dilations=dilations,
                             n_tiles=n_tiles)
    qkv_specs = [pl.BlockSpec((tm, d_attn), cur)] * 3
    qkv_shapes = [jax.ShapeDtypeStruct((m, d_attn), BF16)] * 3
    for d in dilations:
        qkv_specs += [pl.BlockSpec((tm // d, d * d_attn), lag)] * 3
        qkv_shapes += [jax.ShapeDtypeStruct((m // d, d * d_attn), BF16)] * 3
    stage = [pltpu.VMEM((n_heads, tm, HEAD_DIM), F32)] * 3 + [pltpu.VMEM((tm, 2 * d_gmlp), F32)]
    return pl.pallas_call(
        kern,
        grid=(n_tiles + 1,),
        in_specs=[
            pl.BlockSpec((tm, d_model), cur),
            pl.BlockSpec((None, 1, d_model), vec),
            pl.BlockSpec((None, d_model, d_in), vec, pipeline_mode=pl.Buffered(1)),
            pl.BlockSpec((tm, HEAD_DIM), pos),
            pl.BlockSpec((tm, HEAD_DIM), pos),
            pl.BlockSpec((None, 1, d_gmlp), vec),
            pl.BlockSpec((None, n_groups, CHUNK, CHUNK), lambda i: (layer, 0, 0, 0)),
            pl.BlockSpec((None, CHUNK, d_gmlp), vec),
            pl.BlockSpec((None, 1, d_gmlp), vec),
        ],
        out_specs=qkv_specs + [pl.BlockSpec((tm, d_gmlp), lag)],
        out_shape=qkv_shapes + [jax.ShapeDtypeStruct((m, d_gmlp), BF16)],
        scratch_shapes=stage * 2 + [
            pltpu.VMEM((tm, d_model), BF16),
            pltpu.VMEM((tm, d_gmlp), BF16),
            pltpu.VMEM((tm, d_gmlp), F32),
        ] + [pltpu.VMEM((n_heads, tm, HEAD_DIM), F32)] * (3 * (len(dilations) - 1)),
        compiler_params=pltpu.CompilerParams(
            dimension_semantics=("arbitrary",), vmem_limit_bytes=VMEM_LIMIT),
    )(x2, g1, w_in, cos2, sin2, lng, ws, bsb, g2)


def _attn_kernel(q_ref, kp_ref, k_ref, kn_ref, vp_ref, v_ref, vn_ref, o_ref, s_ref,
                 *, n_side, n_heads):
    tl = q_ref.shape[1]
    halo = kp_ref.shape[1]
    n_tiles = tl // Q_TILE
    t = pl.program_id(2)
    first_block = t == 0
    last_block = t == pl.num_programs(2) - 1

    kw = Q_TILE + 2 * halo
    row_i = lax.broadcasted_iota(jnp.int32, (Q_TILE, kw), 0)
    col_j = lax.broadcasted_iota(jnp.int32, (Q_TILE, kw), 1)
    rel = col_j - row_i - halo
    band = (rel >= -n_side) & (rel <= n_side)
    head_ok = (col_j >= halo) | jnp.logical_not(first_block)
    tail_ok = (col_j < halo + Q_TILE) | jnp.logical_not(last_block)
    lane = lax.broadcasted_iota(jnp.int32, (Q_TILE, LANES), 1)
    ones = jnp.ones((kw, HEAD_DIM), BF16)

    def window(prev_ref, main_ref, next_ref, i, cols):
        lo, hi = i * Q_TILE - halo, (i + 1) * Q_TILE + halo
        parts = []
        if lo < 0:
            parts.append(prev_ref[0, :, cols])
        parts.append(main_ref[0, max(lo, 0):min(hi, tl), cols])
        if hi > tl:
            parts.append(next_ref[0, :, cols])
        return parts[0] if len(parts) == 1 else jnp.concatenate(parts, axis=0)

    biases = {}
    for i in range(n_tiles):
        mask = band
        if i == 0:
            mask = mask & head_ok
        if i == n_tiles - 1:
            mask = mask & tail_ok
        biases[i] = jnp.where(mask, 0.0, NEG)

    n_res = q_ref.shape[2] // (n_heads * HEAD_DIM)
    for res, i in ((res, i) for res in range(n_res) for i in range(n_tiles)):
        rows = slice(i * Q_TILE, (i + 1) * Q_TILE)
        bias = biases[i]
        stats = jnp.ones((Q_TILE, LANES), F32)
        for hh in range(n_heads):
            cols = _head_cols(res * n_heads + hh)
            kh = window(kp_ref, k_ref, kn_ref, i, cols)
            vh = window(vp_ref, v_ref, vn_ref, i, cols)
            s = lax.dot_general(q_ref[0, rows, cols], kh, (((1,), (1,)), ((), ())),
                                preferred_element_type=F32) + bias
            mx = jnp.max(s, axis=-1, keepdims=True)
            p = jnp.exp2(s - mx).astype(BF16)
            ov = jnp.dot(p, jnp.concatenate([vh, ones], axis=1), preferred_element_type=F32)
            o_ref[0, rows, cols] = ov[:, :HEAD_DIM].astype(o_ref.dtype)
            stats = jnp.where(lane == hh, mx, stats)
            stats = jnp.where(lane == n_heads + hh, ov[:, HEAD_DIM:], stats)
        s_ref[0, rows, res * LANES:(res + 1) * LANES] = stats


def _attn_branch(q, k, v, *, batch, window, dilation):
    rows_total, width = q.shape
    c = width // dilation
    sub_len = rows_total // batch
    n_heads = c // HEAD_DIM
    n_side = window // (2 * dilation)
    halo = n_side
    assert halo * 2 == Q_TILE and 2 * n_heads <= LANES and sub_len % Q_TILE == 0
    tl = min(sub_len, ATTN_TILES_PER_STEP * Q_TILE)
    assert sub_len % tl == 0 and tl % halo == 0
    hb = tl // halo
    n_hb = sub_len // halo
    n_res = min(dilation, max(1, ATTN_TILES_PER_STEP * Q_TILE // tl))
    assert dilation % n_res == 0

    view = lambda a: a.reshape(batch, sub_len, a.shape[-1])
    qv, kv, vv = view(q), view(k), view(v)

    main = lambda bi, r, t: (bi, t, r)
    prev_halo = lambda bi, r, t: (bi, jnp.maximum(t * hb - 1, 0), r)
    next_halo = lambda bi, r, t: (bi, jnp.minimum((t + 1) * hb, n_hb - 1), r)
    big = pl.BlockSpec((1, tl, n_res * c), main)
    small_p = pl.BlockSpec((1, halo, n_res * c), prev_halo)
    small_n = pl.BlockSpec((1, halo, n_res * c), next_halo)
    stat_spec = pl.BlockSpec((1, tl, n_res * LANES), main)
    stat_shape = jax.ShapeDtypeStruct((batch, sub_len, dilation * LANES), F32)

    kern = functools.partial(_attn_kernel, n_side=n_side, n_heads=n_heads)
    o, stats = pl.pallas_call(
        kern,
        grid=(batch, dilation // n_res, sub_len // tl),
        in_specs=[big, small_p, big, small_n, small_p, big, small_n],
        out_specs=[big, stat_spec],
        out_shape=[jax.ShapeDtypeStruct((batch, sub_len, width), BF16), stat_shape],
        compiler_params=pltpu.CompilerParams(
            dimension_semantics=("arbitrary",) * 3, vmem_limit_bytes=VMEM_LIMIT),
    )(qv, kv, kv, kv, vv, vv, vv)
    flat = lambda a: a.reshape(rows_total, a.shape[-1])
    return flat(o), flat(stats)


def _outproj_kernel(*refs, dilations, n_heads, n_tiles):
    n_pat = 1 + len(dilations)
    n_dil = len(dilations)
    x_ref = refs[0]
    o_refs = refs[1:1 + 2 * n_pat:2]
    s_refs = refs[2:2 + 2 * n_pat:2]
    ga_ref, gn_ref, w_ref, out_ref = refs[1 + 2 * n_pat:5 + 2 * n_pat]
    scratch = refs[5 + 2 * n_pat:]
    o_slabs = scratch[0:n_dil]
    s_slabs = scratch[n_dil:2 * n_dil]
    an_stages = scratch[2 * n_dil:2 * n_dil + 2]
    a_slab = scratch[2 * n_dil + 2]

    tm, d_model = x_ref.shape
    d_attn = n_heads * HEAD_DIM
    state = {}

    def unpermute_task(idx, d, r):
        dst = pl.ds(r, tm // d, stride=d)
        s_slabs[idx][dst, :] = s_refs[idx + 1][:, r * LANES:(r + 1) * LANES]
        for hh in range(n_heads):
            o_slabs[idx][hh, dst, :] = o_refs[idx + 1][:, _head_cols(r * n_heads + hh)].astype(F32)

    def weights_task():
        mxs = [s_refs[0][...]] + [s_slab[...] for s_slab in s_slabs]
        dens = [pltpu.roll(stats, LANES - n_heads, axis=1) for stats in mxs]
        top = functools.reduce(jnp.maximum, mxs)
        es = [jnp.exp2(mx - top) for mx in mxs]
        inv = 1.0 / functools.reduce(jnp.add, [e * den for e, den in zip(es, dens)])
        state["ws"] = [e * inv for e in es]
        state["ssq"] = jnp.zeros((tm, 1), F32)

    def combine_task(hh):
        parts = [o_refs[0][:, _head_cols(hh)].astype(F32)] + [o_slab[hh] for o_slab in o_slabs]
        a = functools.reduce(jnp.add, [w[:, hh:hh + 1] * p for w, p in zip(state["ws"], parts)])
        state["ssq"] = state["ssq"] + jnp.sum(a * a, axis=-1, keepdims=True)
        a_slab[hh] = a

    def norm_task(an_scr, hh):
        if hh == 0:
            state["scale"] = lax.rsqrt(state["ssq"] / d_attn + EPS)
        an_scr[:, _head_cols(hh)] = (
            a_slab[hh] * state["scale"] * ga_ref[:, _head_cols(hh)]).astype(BF16)

    def merge_tasks(an_scr):
        state.clear()
        tasks = [functools.partial(unpermute_task, idx, d, r)
                 for idx, d in enumerate(dilations) for r in range(d)]
        tasks.append(weights_task)
        tasks += [functools.partial(combine_task, hh) for hh in range(n_heads)]
        tasks += [functools.partial(norm_task, an_scr, hh) for hh in range(n_heads)]
        return tasks

    def dot_tasks(an_scr):
        def dot_task(col):
            cols = slice(col, col + MXU_COLS)
            acc = jnp.dot(gn_ref[...], w_ref[d_attn:, cols], preferred_element_type=F32)
            acc = acc + jnp.dot(an_scr[...], w_ref[0:d_attn, cols], preferred_element_type=F32)
            out_ref[:, cols] = x_ref[:, cols] + acc

        return [functools.partial(dot_task, col) for col in range(0, d_model, MXU_COLS)]

    i = pl.program_id(0)
    interior = (i > 0) & (i < n_tiles)

    @pl.when(i == 0)
    def _():
        _interleave(merge_tasks(an_stages[0]), [])

    for parity in range(2):
        @pl.when(interior & (i % 2 == parity))
        def _():
            _interleave(dot_tasks(an_stages[1 - parity]), merge_tasks(an_stages[parity]))

    @pl.when(i == n_tiles)
    def _():
        _interleave(dot_tasks(an_stages[(n_tiles - 1) % 2]), [])


def _outproj(x2, branches, ga, gn, w_out, *, layer, dilations, tm):
    m, d_model = x2.shape
    d_attn = branches[0][0].shape[1]
    n_heads = d_attn // HEAD_DIM
    n_tiles = m // tm
    cur = lambda i: (jnp.minimum(i, n_tiles - 1), 0)
    lag = lambda i: (jnp.maximum(i - 1, 0), 0)
    args, in_specs = [x2], [pl.BlockSpec((tm, d_model), lag)]
    for d, branch in zip((1,) + tuple(dilations), branches):
        args += list(branch)
        in_specs += [pl.BlockSpec((tm // d, d * d_attn), cur),
                     pl.BlockSpec((tm // d, d * LANES), cur)]
    args += [ga, gn, w_out]
    in_specs += [
        pl.BlockSpec((None, 1, d_attn), lambda i: (layer, 0, 0)),
        pl.BlockSpec((tm, gn.shape[1]), lag),
        pl.BlockSpec((None,) + w_out.shape[1:], lambda i: (layer, 0, 0),
                     pipeline_mode=pl.Buffered(1)),
    ]
    return pl.pallas_call(
        functools.partial(_outproj_kernel, dilations=dilations, n_heads=n_heads,
                          n_tiles=n_tiles),
        grid=(n_tiles + 1,),
        in_specs=in_specs,
        out_specs=pl.BlockSpec((tm, d_model), lag),
        out_shape=jax.ShapeDtypeStruct((m, d_model), F32),
        scratch_shapes=(
            [pltpu.VMEM((n_heads, tm, HEAD_DIM), F32) for _ in dilations]
            + [pltpu.VMEM((tm, LANES), F32) for _ in dilations]
            + [pltpu.VMEM((tm, d_attn), BF16)] * 2
            + [pltpu.VMEM((n_heads, tm, HEAD_DIM), F32)]),
        compiler_params=pltpu.CompilerParams(
            dimension_semantics=("arbitrary",), vmem_limit_bytes=VMEM_LIMIT),
    )(*args)


def _ffn_kernel(*refs, final_norm):
    if final_norm:
        x_ref, g_ref, wg_ref, wu_ref, wd_ref, fg_ref, o_ref, h_scr = refs
    else:
        x_ref, g_ref, wg_ref, wu_ref, wd_ref, o_ref, h_scr = refs
    f = pl.program_id(1)

    @pl.when(f == 0)
    def _():
        x = x_ref[...]
        h_scr[...] = (x * _rms_scale(x) * g_ref[...]).astype(BF16)
        o_ref[...] = x

    h = h_scr[...]
    gate = jnp.dot(h, wg_ref[...], preferred_element_type=F32)
    up = jnp.dot(h, wu_ref[...], preferred_element_type=F32)
    ff = (gate * (1.0 / (1.0 + jnp.exp(-gate))) * up).astype(BF16)
    o_ref[...] += jnp.dot(ff, wd_ref[...], preferred_element_type=F32)

    if final_norm:
        @pl.when(f == pl.num_programs(1) - 1)
        def _():
            y = o_ref[...]
            o_ref[...] = y * _rms_scale(y) * fg_ref[...]


def _ffn(x2, g, w_gate, w_up, w_down, final_g, *, layer, tm, tf):
    m, d_model = x2.shape
    d_ff = w_gate.shape[2]
    final_norm = final_g is not None
    args = [x2, g, w_gate, w_up, w_down]
    in_specs = [
        pl.BlockSpec((tm, d_model), lambda i, f: (i, 0)),
        pl.BlockSpec((None, 1, d_model), lambda i, f: (layer, 0, 0)),
        pl.BlockSpec((None, d_model, tf), lambda i, f: (layer, 0, f)),
        pl.BlockSpec((None, d_model, tf), lambda i, f: (layer, 0, f)),
        pl.BlockSpec((None, tf, d_model), lambda i, f: (layer, f, 0)),
    ]
    if final_norm:
        args.append(final_g)
        in_specs.append(pl.BlockSpec((1, d_model), lambda i, f: (0, 0)))
    return pl.pallas_call(
        functools.partial(_ffn_kernel, final_norm=final_norm),
        grid=(m // tm, d_ff // tf),
        in_specs=in_specs,
        out_specs=pl.BlockSpec((tm, d_model), lambda i, f: (i, 0)),
        out_shape=jax.ShapeDtypeStruct((m, d_model), F32),
        scratch_shapes=[pltpu.VMEM((tm, d_model), BF16)],
        input_output_aliases={0: 0},
        compiler_params=pltpu.CompilerParams(
            dimension_semantics=("arbitrary", "arbitrary"), vmem_limit_bytes=VMEM_LIMIT),
    )(*args)


def _rope_tables(seq):
    pos = jnp.arange(seq, dtype=F32)
    inv = ROPE_THETA ** (-jnp.arange(0, HEAD_DIM, 2, dtype=F32) / HEAD_DIM)
    ang = pos[:, None] * inv[None, :]
    cos, sin = jnp.cos(ang), jnp.sin(ang)
    return jnp.concatenate([cos, cos], axis=-1), jnp.concatenate([-sin, sin], axis=-1)


def kernel(x, norm1_g, w_in, gmlp_ln_g, w_spatial, b_spatial, mix_norm_attn_g,
           mix_norm_gmlp_g, w_out, norm2_g, w_gate, w_up, w_down, final_g):
    b, s, d_model = x.shape
    depth = w_in.shape[0]
    d_gmlp = gmlp_ln_g.shape[1]
    d_attn = mix_norm_attn_g.shape[1]
    n_groups = w_spatial.shape[1]
    d_ff = w_gate.shape[2]
    assert w_in.shape[2] == 3 * d_attn + 2 * d_gmlp and n_groups * GMLP_GROUP == d_gmlp
    assert w_spatial.shape[2] == CHUNK
    assert DILATED_PATTERNS[0][1] == 1
    dilations = tuple(d for _, d in DILATED_PATTERNS[1:])

    tm_in, tm_out, tm_ffn, tf = 256, 512, 1024, 512
    assert s % tm_in == 0 and s % tm_out == 0 and s % CHUNK == 0 and d_ff % tf == 0

    cos2, sin2 = _rope_tables(s)
    x2 = x.reshape(b * s, d_model)
    vec3 = lambda a: a.reshape(a.shape[0], 1, a.shape[1])

    w_in_b, w_out_b = w_in.astype(BF16), w_out.astype(BF16)
    w_gate_b, w_up_b, w_down_b = w_gate.astype(BF16), w_up.astype(BF16), w_down.astype(BF16)
    ws_b = w_spatial.astype(BF16)
    bsb = jnp.repeat(jnp.swapaxes(b_spatial, 1, 2), GMLP_GROUP, axis=2)
    g1, lng, g2, ga, gf = (vec3(a) for a in (norm1_g, gmlp_ln_g, mix_norm_gmlp_g,
                                              mix_norm_attn_g, norm2_g))

    for l in range(depth):
        outs = _inproj(x2, g1, w_in_b, cos2, sin2, lng, ws_b, bsb, g2, layer=l, seq=s,
                       d_attn=d_attn, d_gmlp=d_gmlp, dilations=dilations, tm=tm_in)
        gn = outs[-1]
        branches = [
            _attn_branch(*outs[3 * i:3 * i + 3], batch=b, window=window, dilation=dilation)
            for i, (window, dilation) in enumerate(DILATED_PATTERNS)]
        x2 = _outproj(x2, branches, ga, gn, w_out_b, layer=l, dilations=dilations, tm=tm_out)
        x2 = _ffn(x2, gf, w_gate_b, w_up_b, w_down_b,
                  final_g.reshape(1, -1) if l == depth - 1 else None,
                  layer=l, tm=tm_ffn, tf=tf)
    return x2.reshape(b, s, d_model)
```

```python
import functools
import math

import jax
import jax.numpy as jnp
from jax import lax
from jax.experimental import pallas as pl
from jax.experimental.pallas import tpu as pltpu

HEAD_DIM = 128
GMLP_GROUP = 128
CHUNK = 128
DILATED_PATTERNS = ((128, 1), (512, 4), (2048, 16))
ROPE_THETA = 10000.0
EPS = 1e-6
NEG = -1e30

LANES = 128
MXU_COLS = 256
Q_TILE = 128
ATTN_TILES_PER_STEP = 16
VMEM_LIMIT = 56 * 1024 * 1024

BF16 = jnp.bfloat16
F32 = jnp.float32


def _gelu(x):
    return 0.5 * x * (1.0 + lax.erf(x * (1.0 / math.sqrt(2.0))))


def _rms_scale(x):
    return lax.rsqrt(jnp.mean(x * x, axis=-1, keepdims=True) + EPS)


def _head_cols(hh):
    return slice(hh * HEAD_DIM, (hh + 1) * HEAD_DIM)


def _interleave(major, minor):
    done = 0
    for j, task in enumerate(major):
        task()
        upto = (j + 1) * len(minor) // len(major)
        for other in minor[done:upto]:
            other()
        done = upto


def _inproj_kernel(*refs, d_attn, d_gmlp, q_scale, dilations, n_tiles):
    (x_ref, g1_ref, w_ref, cos_ref, sin_ref, lng_ref, ws_ref, bs_ref, g2_ref) = refs[:9]
    n_res_out = 3 * len(dilations)
    nat_refs = refs[9:12]
    res_refs = refs[12:12 + n_res_out]
    gn_ref = refs[12 + n_res_out]
    scratch = refs[13 + n_res_out:]
    stages = (scratch[0:4], scratch[4:8])
    h_scr, vln_scr, gate_scr = scratch[8:11]
    mids = [scratch[11 + 3 * level:14 + 3 * level] for level in range(len(dilations) - 1)]

    tm = x_ref.shape[0]
    n_heads = d_attn // HEAD_DIM
    n_groups = d_gmlp // GMLP_GROUP
    heads_per_dot = MXU_COLS // HEAD_DIM

    def rope(t):
        return t * cos_ref[...] + pltpu.roll(t, HEAD_DIM // 2, axis=1) * sin_ref[...]

    head_fns = (lambda t: rope(t) * q_scale, rope, lambda t: t)

    def project_tasks(stage):
        slabs, uv_scr = stage[:3], stage[3]

        def norm():
            x = x_ref[...]
            h_scr[...] = (x * _rms_scale(x) * g1_ref[...]).astype(BF16)

        def project(col0):
            return jnp.dot(h_scr[...], w_ref[:, col0:col0 + MXU_COLS], preferred_element_type=F32)

        def uv_task(col):
            uv_scr[:, col:col + MXU_COLS] = project(3 * d_attn + col)

        def qkv_task(idx, first):
            proj = project(idx * d_attn + first * HEAD_DIM)
            for sub in range(heads_per_dot):
                val = head_fns[idx](proj[:, _head_cols(sub)])
                nat_refs[idx][:, _head_cols(first + sub)] = val.astype(BF16)
                slabs[idx][first + sub] = val

        tasks = [norm]
        tasks += [functools.partial(uv_task, col) for col in range(0, 2 * d_gmlp, MXU_COLS)]
        tasks += [functools.partial(qkv_task, idx, first) for idx in range(3)
                  for first in range(0, n_heads, heads_per_dot)]
        return tasks

    def finish_tasks(stage):
        slabs, uv_scr = stage[:3], stage[3]

        def copy_task(idx, level, r):
            d = dilations[level]
            d_prev = dilations[level - 1] if level else 1
            n = tm // d
            for hh in range(n_heads):
                if level:
                    start = (r % d_prev) * (tm // d_prev) + r // d_prev
                    val = mids[level - 1][idx][hh, pl.ds(start, n, stride=d // d_prev), :]
                else:
                    val = slabs[idx][hh, pl.ds(r, n, stride=d), :]
                res_refs[3 * level + idx][:, _head_cols(r * n_heads + hh)] = val.astype(BF16)
                if level + 1 < len(dilations):
                    mids[level][idx][hh, r * n:(r + 1) * n, :] = val

        def layernorm_task(rows):
            vv = _gelu(uv_scr[rows, d_gmlp:2 * d_gmlp])
            vc = vv - jnp.mean(vv, axis=-1, keepdims=True)
            vln = vc * lax.rsqrt(jnp.mean(vc * vc, axis=-1, keepdims=True) + EPS) * lng_ref[...]
            vln_scr[rows, :] = vln.astype(BF16)

        def mix_task(rows, g):
            cols = slice(g * GMLP_GROUP, (g + 1) * GMLP_GROUP)
            mixed = jnp.dot(ws_ref[g], vln_scr[rows, cols], preferred_element_type=F32)
            u = _gelu(uv_scr[rows, cols])
            gate_scr[rows, cols] = u * (mixed + bs_ref[:, cols])

        def gate_norm_task(rows):
            gate = gate_scr[rows, :]
            gn_ref[rows, :] = (gate * _rms_scale(gate) * g2_ref[...]).astype(BF16)

        tasks = []
        for c in range(tm // CHUNK):
            rows = slice(c * CHUNK, (c + 1) * CHUNK)
            tasks.append(functools.partial(layernorm_task, rows))
            tasks += [functools.partial(mix_task, rows, g) for g in range(n_groups)]
            tasks.append(functools.partial(gate_norm_task, rows))
        tasks += [functools.partial(copy_task, idx, level, r) for level, d in enumerate(dilations)
                  for idx in range(3) for r in range(d)]
        return tasks

    i = pl.program_id(0)
    interior = (i > 0) & (i < n_tiles)

    @pl.when(i == 0)
    def _():
        _interleave(project_tasks(stages[0]), [])

    for parity in range(2):
        @pl.when(interior & (i % 2 == parity))
        def _():
            _interleave(project_tasks(stages[parity]), finish_tasks(stages[1 - parity]))

    @pl.when(i == n_tiles)
    def _():
        _interleave(finish_tasks(stages[(n_tiles - 1) % 2]), [])


def _inproj(x2, g1, w_in, cos2, sin2, lng, ws, bsb, g2, *, layer, seq, d_attn, d_gmlp,
            dilations, tm):
    m, d_model = x2.shape
    d_in = w_in.shape[2]
    n_groups = d_gmlp // GMLP_GROUP
    n_heads = d_attn // HEAD_DIM
    pos_blocks = seq // tm
    n_tiles = m // tm
    assert all(tm % (d * 16) == 0 for d in dilations)
    assert all(d % d_prev == 0 for d, d_prev in zip(dilations, (1,) + tuple(dilations)))
    vec = lambda i: (layer, 0, 0)
    cur = lambda i: (jnp.minimum(i, n_tiles - 1), 0)
    lag = lambda i: (jnp.maximum(i - 1, 0), 0)
    pos = lambda i: (jnp.minimum(i, n_tiles - 1) % pos_blocks, 0)
    kern = functools.partial(_inproj_kernel, d_attn=d_attn, d_gmlp=d_gmlp,
                             q_scale=HEAD_DIM ** -0.5 * math.log2(math.e), dilations=dilations,
                             n_tiles=n_tiles)
    qkv_specs = [pl.BlockSpec((tm, d_attn), cur)] * 3
    qkv_shapes = [jax.ShapeDtypeStruct((m, d_attn), BF16)] * 3
    for d in dilations:
        qkv_specs += [pl.BlockSpec((tm // d, d * d_attn), lag)] * 3
        qkv_shapes += [jax.ShapeDtypeStruct((m // d, d * d_attn), BF16)] * 3
    stage = [pltpu.VMEM((n_heads, tm, HEAD_DIM), F32)] * 3 + [pltpu.VMEM((tm, 2 * d_gmlp), F32)]
    return pl.pallas_call(
        kern,
        grid=(n_tiles + 1,),
        in_specs=[
            pl.BlockSpec((tm, d_model), cur),
            pl.BlockSpec((None, 1, d_model), vec),
            pl.BlockSpec((None, d_model, d_in), vec, pipeline_mode=pl.Buffered(1)),
            pl.BlockSpec((tm, HEAD_DIM), pos),
            pl.BlockSpec((tm, HEAD_DIM), pos),
            pl.BlockSpec((None, 1, d_gmlp), vec),
            pl.BlockSpec((None, n_groups, CHUNK, CHUNK), lambda i: (layer, 0, 0, 0)),
            pl.BlockSpec((None, CHUNK, d_gmlp), vec),
            pl.BlockSpec((None, 1, d_gmlp), vec),
        ],
        out_specs=qkv_specs + [pl.BlockSpec((tm, d_gmlp), lag)],
        out_shape=qkv_shapes + [jax.ShapeDtypeStruct((m, d_gmlp), BF16)],
        scratch_shapes=stage * 2 + [
            pltpu.VMEM((tm, d_model), BF16),
            pltpu.VMEM((tm, d_gmlp), BF16),
            pltpu.VMEM((tm, d_gmlp), F32),
        ] + [pltpu.VMEM((n_heads, tm, HEAD_DIM), F32)] * (3 * (len(dilations) - 1)),
        compiler_params=pltpu.CompilerParams(
            dimension_semantics=("arbitrary",), vmem_limit_bytes=VMEM_LIMIT),
    )(x2, g1, w_in, cos2, sin2, lng, ws, bsb, g2)


def _attn_kernel(q_ref, kp_ref, k_ref, kn_ref, vp_ref, v_ref, vn_ref, o_ref, s_ref,
                 *, n_side, n_heads):
    tl = q_ref.shape[1]
    halo = kp_ref.shape[1]
    n_tiles = tl // Q_TILE
    t = pl.program_id(2)
    first_block = t == 0
    last_block = t == pl.num_programs(2) - 1

    kw = Q_TILE + 2 * halo
    row_i = lax.broadcasted_iota(jnp.int32, (Q_TILE, kw), 0)
    col_j = lax.broadcasted_iota(jnp.int32, (Q_TILE, kw), 1)
    rel = col_j - row_i - halo
    band = (rel >= -n_side) & (rel <= n_side)
    head_ok = (col_j >= halo) | jnp.logical_not(first_block)
    tail_ok = (col_j < halo + Q_TILE) | jnp.logical_not(last_block)
    lane = lax.broadcasted_iota(jnp.int32, (Q_TILE, LANES), 1)
    ones = jnp.ones((kw, HEAD_DIM), BF16)

    def window(prev_ref, main_ref, next_ref, i, cols):
        lo, hi = i * Q_TILE - halo, (i + 1) * Q_TILE + halo
        parts = []
        if lo < 0:
            parts.append(prev_ref[0, :, cols])
        parts.append(main_ref[0, max(lo, 0):min(hi, tl), cols])
        if hi > tl:
            parts.append(next_ref[0, :, cols])
        return parts[0] if len(parts) == 1 else jnp.concatenate(parts, axis=0)

    biases = {}
    for i in range(n_tiles):
        mask = band
        if i == 0:
            mask = mask & head_ok
        if i == n_tiles - 1:
            mask = mask & tail_ok
        biases[i] = jnp.where(mask, 0.0, NEG)

    n_res = q_ref.shape[2] // (n_heads * HEAD_DIM)
    for res, i in ((res, i) for res in range(n_res) for i in range(n_tiles)):
        rows = slice(i * Q_TILE, (i + 1) * Q_TILE)
        bias = biases[i]
        stats = jnp.ones((Q_TILE, LANES), F32)
        for hh in range(n_heads):
            cols = _head_cols(res * n_heads + hh)
            kh = window(kp_ref, k_ref, kn_ref, i, cols)
            vh = window(vp_ref, v_ref, vn_ref, i, cols)
            s = lax.dot_general(q_ref[0, rows, cols], kh, (((1,), (1,)), ((), ())),
                                preferred_element_type=F32) + bias
            mx = jnp.max(s, axis=-1, keepdims=True)
            p = jnp.exp2(s - mx).astype(BF16)
            ov = jnp.dot(p, jnp.concatenate([vh, ones], axis=1), preferred_element_type=F32)
            o_ref[0, rows, cols] = ov[:, :HEAD_DIM].astype(o_ref.dtype)
            stats = jnp.where(lane == hh, mx, stats)
            stats = jnp.where(lane == n_heads + hh, ov[:, HEAD_DIM:], stats)
        s_ref[0, rows, res * LANES:(res + 1) * LANES] = stats


def _attn_branch(q, k, v, *, batch, window, dilation):
    rows_total, width = q.shape
    c = width // dilation
    sub_len = rows_total // batch
    n_heads = c // HEAD_DIM
    n_side = window // (2 * dilation)
    halo = n_side
    assert halo * 2 == Q_TILE and 2 * n_heads <= LANES and sub_len % Q_TILE == 0
    tl = min(sub_len, ATTN_TILES_PER_STEP * Q_TILE)
    assert sub_len % tl == 0 and tl % halo == 0
    hb = tl // halo
    n_hb = sub_len // halo
    n_res = min(dilation, max(1, ATTN_TILES_PER_STEP * Q_TILE // tl))
    assert dilation % n_res == 0

    view = lambda a: a.reshape(batch, sub_len, a.shape[-1])
    qv, kv, vv = view(q), view(k), view(v)

    main = lambda bi, r, t: (bi, t, r)
    prev_halo = lambda bi, r, t: (bi, jnp.maximum(t * hb - 1, 0), r)
    next_halo = lambda bi, r, t: (bi, jnp.minimum((t + 1) * hb, n_hb - 1), r)
    big = pl.BlockSpec((1, tl, n_res * c), main)
    small_p = pl.BlockSpec((1, halo, n_res * c), prev_halo)
    small_n = pl.BlockSpec((1, halo, n_res * c), next_halo)
    stat_spec = pl.BlockSpec((1, tl, n_res * LANES), main)
    stat_shape = jax.ShapeDtypeStruct((batch, sub_len, dilation * LANES), F32)

    kern = functools.partial(_attn_kernel, n_side=n_side, n_heads=n_heads)
    o, stats = pl.pallas_call(
        kern,
        grid=(batch, dilation // n_res, sub_len // tl),
        in_specs=[big, small_p, big, small_n, small_p, big, small_n],
        out_specs=[big, stat_spec],
        out_shape=[jax.ShapeDtypeStruct((batch, sub_len, width), BF16), stat_shape],
        compiler_params=pltpu.CompilerParams(
            dimension_semantics=("arbitrary",) * 3, vmem_limit_bytes=VMEM_LIMIT),
    )(qv, kv, kv, kv, vv, vv, vv)
    flat = lambda a: a.reshape(rows_total, a.shape[-1])
    return flat(o), flat(stats)


def _outproj_kernel(*refs, dilations, n_heads, n_tiles):
    n_pat = 1 + len(dilations)
    n_dil = len(dilations)
    x_ref = refs[0]
    o_refs = refs[1:1 + 2 * n_pat:2]
    s_refs = refs[2:2 + 2 * n_pat:2]
    ga_ref, gn_ref, w_ref, out_ref = refs[1 + 2 * n_pat:5 + 2 * n_pat]
    scratch = refs[5 + 2 * n_pat:]
    o_slabs = scratch[0:n_dil]
    s_slabs = scratch[n_dil:2 * n_dil]
    an_stages = scratch[2 * n_dil:2 * n_dil + 2]
    a_slab = scratch[2 * n_dil + 2]
    o_mids = scratch[2 * n_dil + 3:3 * n_dil + 2]
    s_mids = scratch[3 * n_dil + 2:4 * n_dil + 1]

    tm, d_model = x_ref.shape
    d_attn = n_heads * HEAD_DIM
    state = {}

    def unpermute_task(idx, level, r):
        d_hi = dilations[level]
        d_lo = dilations[level - 1] if level else 1
        n = tm // d_hi
        dst = pl.ds((r % d_lo) * (tm // d_lo) + r // d_lo, n, stride=d_hi // d_lo)
        o_dst, s_dst = (o_mids[level - 1], s_mids[level - 1]) if level else (o_slabs[idx], s_slabs[idx])
        if level == idx:
            s_dst[dst, :] = s_refs[idx + 1][:, r * LANES:(r + 1) * LANES]
            for hh in range(n_heads):
                o_dst[hh, dst, :] = o_refs[idx + 1][:, _head_cols(r * n_heads + hh)].astype(F32)
        else:
            s_dst[dst, :] = s_mids[level][r * n:(r + 1) * n, :]
            for hh in range(n_heads):
                o_dst[hh, dst, :] = o_mids[level][hh, r * n:(r + 1) * n, :]

    def weights_task():
        mxs = [s_refs[0][...]] + [s_slab[...] for s_slab in s_slabs]
        dens = [pltpu.roll(stats, LANES - n_heads, axis=1) for stats in mxs]
        top = functools.reduce(jnp.maximum, mxs)
        es = [jnp.exp2(mx - top) for mx in mxs]
        inv = 1.0 / functools.reduce(jnp.add, [e * den for e, den in zip(es, dens)])
        state["ws"] = [e * inv for e in es]
        state["ssq"] = jnp.zeros((tm, 1), F32)

    def combine_task(hh):
        parts = [o_refs[0][:, _head_cols(hh)].astype(F32)] + [o_slab[hh] for o_slab in o_slabs]
        a = functools.reduce(jnp.add, [w[:, hh:hh + 1] * p for w, p in zip(state["ws"], parts)])
        state["ssq"] = state["ssq"] + jnp.sum(a * a, axis=-1, keepdims=True)
        a_slab[hh] = a

    def norm_task(an_scr, hh):
        if hh == 0:
            state["scale"] = lax.rsqrt(state["ssq"] / d_attn + EPS)
        an_scr[:, _head_cols(hh)] = (
            a_slab[hh] * state["scale"] * ga_ref[:, _head_cols(hh)]).astype(BF16)

    def merge_tasks(an_scr):
        state.clear()
        tasks = [functools.partial(unpermute_task, idx, level, r) for idx in range(n_dil)
                 for level in range(idx, -1, -1) for r in range(dilations[level])]
        tasks.append(weights_task)
        tasks += [functools.partial(combine_task, hh) for hh in range(n_heads)]
        tasks += [functools.partial(norm_task, an_scr, hh) for hh in range(n_heads)]
        return tasks

    def dot_tasks(an_scr):
        def dot_task(col):
            cols = slice(col, col + MXU_COLS)
            acc = jnp.dot(gn_ref[...], w_ref[d_attn:, cols], preferred_element_type=F32)
            acc = acc + jnp.dot(an_scr[...], w_ref[0:d_attn, cols], preferred_element_type=F32)
            out_ref[:, cols] = x_ref[:, cols] + acc

        return [functools.partial(dot_task, col) for col in range(0, d_model, MXU_COLS)]

    i = pl.program_id(0)
    interior = (i > 0) & (i < n_tiles)

    @pl.when(i == 0)
    def _():
        _interleave(merge_tasks(an_stages[0]), [])

    for parity in range(2):
        @pl.when(interior & (i % 2 == parity))
        def _():
            _interleave(dot_tasks(an_stages[1 - parity]), merge_tasks(an_stages[parity]))

    @pl.when(i == n_tiles)
    def _():
        _interleave(dot_tasks(an_stages[(n_tiles - 1) % 2]), [])


def _outproj(x2, branches, ga, gn, w_out, *, layer, dilations, tm):
    m, d_model = x2.shape
    d_attn = branches[0][0].shape[1]
    n_heads = d_attn // HEAD_DIM
    n_tiles = m // tm
    assert all(d % d_prev == 0 for d, d_prev in zip(dilations, (1,) + tuple(dilations)))
    cur = lambda i: (jnp.minimum(i, n_tiles - 1), 0)
    lag = lambda i: (jnp.maximum(i - 1, 0), 0)
    args, in_specs = [x2], [pl.BlockSpec((tm, d_model), lag)]
    for d, branch in zip((1,) + tuple(dilations), branches):
        args += list(branch)
        in_specs += [pl.BlockSpec((tm // d, d * d_attn), cur),
                     pl.BlockSpec((tm // d, d * LANES), cur)]
    args += [ga, gn, w_out]
    in_specs += [
        pl.BlockSpec((None, 1, d_attn), lambda i: (layer, 0, 0)),
        pl.BlockSpec((tm, gn.shape[1]), lag),
        pl.BlockSpec((None,) + w_out.shape[1:], lambda i: (layer, 0, 0),
                     pipeline_mode=pl.Buffered(1)),
    ]
    return pl.pallas_call(
        functools.partial(_outproj_kernel, dilations=dilations, n_heads=n_heads,
                          n_tiles=n_tiles),
        grid=(n_tiles + 1,),
        in_specs=in_specs,
        out_specs=pl.BlockSpec((tm, d_model), lag),
        out_shape=jax.ShapeDtypeStruct((m, d_model), F32),
        scratch_shapes=(
            [pltpu.VMEM((n_heads, tm, HEAD_DIM), F32) for _ in dilations]
            + [pltpu.VMEM((tm, LANES), F32) for _ in dilations]
            + [pltpu.VMEM((tm, d_attn), BF16)] * 2
            + [pltpu.VMEM((n_heads, tm, HEAD_DIM), F32)] * len(dilations)
            + [pltpu.VMEM((tm, LANES), F32)] * (len(dilations) - 1)),
        compiler_params=pltpu.CompilerParams(
            dimension_semantics=("arbitrary",), vmem_limit_bytes=VMEM_LIMIT),
    )(*args)


def _ffn_kernel(*refs, final_norm):
    if final_norm:
        x_ref, g_ref, wg_ref, wu_ref, wd_ref, fg_ref, o_ref, h_scr = refs
    else:
        x_ref, g_ref, wg_ref, wu_ref, wd_ref, o_ref, h_scr = refs
    f = pl.program_id(1)

    @pl.when(f == 0)
    def _():
        x = x_ref[...]
        h_scr[...] = (x * _rms_scale(x) * g_ref[...]).astype(BF16)
        o_ref[...] = x

    h = h_scr[...]
    gate = jnp.dot(h, wg_ref[...], preferred_element_type=F32)
    up = jnp.dot(h, wu_ref[...], preferred_element_type=F32)
    ff = (gate * (1.0 / (1.0 + jnp.exp(-gate))) * up).astype(BF16)
    o_ref[...] += jnp.dot(ff, wd_ref[...], preferred_element_type=F32)

    if final_norm:
        @pl.when(f == pl.num_programs(1) - 1)
        def _():
            y = o_ref[...]
            o_ref[...] = y * _rms_scale(y) * fg_ref[...]


def _ffn(x2, g, w_gate, w_up, w_down, final_g, *, layer, tm, tf):
    m, d_model = x2.shape
    d_ff = w_gate.shape[2]
    final_norm = final_g is not None
    args = [x2, g, w_gate, w_up, w_down]
    in_specs = [
        pl.BlockSpec((tm, d_model), lambda i, f: (i, 0)),
        pl.BlockSpec((None, 1, d_model), lambda i, f: (layer, 0, 0)),
        pl.BlockSpec((None, d_model, tf), lambda i, f: (layer, 0, f)),
        pl.BlockSpec((None, d_model, tf), lambda i, f: (layer, 0, f)),
        pl.BlockSpec((None, tf, d_model), lambda i, f: (layer, f, 0)),
    ]
    if final_norm:
        args.append(final_g)
        in_specs.append(pl.BlockSpec((1, d_model), lambda i, f: (0, 0)))
    return pl.pallas_call(
        functools.partial(_ffn_kernel, final_norm=final_norm),
        grid=(m // tm, d_ff // tf),
        in_specs=in_specs,
        out_specs=pl.BlockSpec((tm, d_model), lambda i, f: (i, 0)),
        out_shape=jax.ShapeDtypeStruct((m, d_model), F32),
        scratch_shapes=[pltpu.VMEM((tm, d_model), BF16)],
        input_output_aliases={0: 0},
        compiler_params=pltpu.CompilerParams(
            dimension_semantics=("arbitrary", "arbitrary"), vmem_limit_bytes=VMEM_LIMIT),
    )(*args)


def _rope_tables(seq):
    pos = jnp.arange(seq, dtype=F32)
    inv = ROPE_THETA ** (-jnp.arange(0, HEAD_DIM, 2, dtype=F32) / HEAD_DIM)
    ang = pos[:, None] * inv[None, :]
    cos, sin = jnp.cos(ang), jnp.sin(ang)
    return jnp.concatenate([cos, cos], axis=-1), jnp.concatenate([-sin, sin], axis=-1)


def kernel(x, norm1_g, w_in, gmlp_ln_g, w_spatial, b_spatial, mix_norm_attn_g,
           mix_norm_gmlp_g, w_out, norm2_g, w_gate, w_up, w_down, final_g):
    b, s, d_model = x.shape
    depth = w_in.shape[0]
    d_gmlp = gmlp_ln_g.shape[1]
    d_attn = mix_norm_attn_g.shape[1]
    n_groups = w_spatial.shape[1]
    d_ff = w_gate.shape[2]
    assert w_in.shape[2] == 3 * d_attn + 2 * d_gmlp and n_groups * GMLP_GROUP == d_gmlp
    assert w_spatial.shape[2] == CHUNK
    assert DILATED_PATTERNS[0][1] == 1
    dilations = tuple(d for _, d in DILATED_PATTERNS[1:])

    tm_in, tm_out, tm_ffn, tf = 256, 512, 1024, 512
    assert s % tm_in == 0 and s % tm_out == 0 and s % CHUNK == 0 and d_ff % tf == 0

    cos2, sin2 = _rope_tables(s)
    x2 = x.reshape(b * s, d_model)
    vec3 = lambda a: a.reshape(a.shape[0], 1, a.shape[1])

    w_in_b, w_out_b = w_in.astype(BF16), w_out.astype(BF16)
    w_gate_b, w_up_b, w_down_b = w_gate.astype(BF16), w_up.astype(BF16), w_down.astype(BF16)
    ws_b = w_spatial.astype(BF16)
    bsb = jnp.repeat(jnp.swapaxes(b_spatial, 1, 2), GMLP_GROUP, axis=2)
    g1, lng, g2, ga, gf = (vec3(a) for a in (norm1_g, gmlp_ln_g, mix_norm_gmlp_g,
                                              mix_norm_attn_g, norm2_g))

    for l in range(depth):
        outs = _inproj(x2, g1, w_in_b, cos2, sin2, lng, ws_b, bsb, g2, layer=l, seq=s,
                       d_attn=d_attn, d_gmlp=d_gmlp, dilations=dilations, tm=tm_in)
        gn = outs[-1]
        branches = [
            _attn_branch(*outs[3 * i:3 * i + 3], batch=b, window=window, dilation=dilation)
            for i, (window, dilation) in enumerate(DILATED_PATTERNS)]
        x2 = _outproj(x2, branches, ga, gn, w_out_b, layer=l, dilations=dilations, tm=tm_out)
        x2 = _ffn(x2, gf, w_gate_b, w_up_b, w_down_b,
                  final_g.reshape(1, -1) if l == depth - 1 else None,
                  layer=l, tm=tm_ffn, tf=tf)
    return x2.reshape(b, s, d_model)
```

```python
import functools
import math

import jax
import jax.numpy as jnp
from jax import lax
from jax.experimental import pallas as pl
from jax.experimental.pallas import tpu as pltpu

HEAD_DIM = 128
GMLP_GROUP = 128
CHUNK = 128
DILATED_PATTERNS = ((128, 1), (512, 4), (2048, 16))
ROPE_THETA = 10000.0
EPS = 1e-6
NEG = -1e30

LANES = 128
MXU_COLS = 256
Q_TILE = 128
ATTN_TILES_PER_STEP = 16
VMEM_LIMIT = 56 * 1024 * 1024

BF16 = jnp.bfloat16
F32 = jnp.float32


def _gelu(x):
    return 0.5 * x * (1.0 + lax.erf(x * (1.0 / math.sqrt(2.0))))


def _rms_scale(x):
    return lax.rsqrt(jnp.mean(x * x, axis=-1, keepdims=True) + EPS)


def _head_cols(hh):
    return slice(hh * HEAD_DIM, (hh + 1) * HEAD_DIM)


def _interleave(major, minor):
    done = 0
    for j, task in enumerate(major):
        task()
        upto = (j + 1) * len(minor) // len(major)
        for other in minor[done:upto]:
            other()
        done = upto


def _inproj_kernel(*refs, d_attn, d_gmlp, q_scale, dilations, n_tiles):
    (x_ref, g1_ref, w_ref, cos_ref, sin_ref, lng_ref, ws_ref, bs_ref, g2_ref) = refs[:9]
    n_res_out = 3 * len(dilations)
    nat_refs = refs[9:12]
    res_refs = refs[12:12 + n_res_out]
    gn_ref = refs[12 + n_res_out]
    scratch = refs[13 + n_res_out:]
    stages = (scratch[0:4], scratch[4:8])
    h_scr, vln_scr, gate_scr = scratch[8:11]
    mids = [scratch[11 + 3 * level:14 + 3 * level] for level in range(len(dilations) - 1)]

    tm = x_ref.shape[0]
    n_heads = d_attn // HEAD_DIM
    n_groups = d_gmlp // GMLP_GROUP
    heads_per_dot = MXU_COLS // HEAD_DIM

    def rope(t):
        return t * cos_ref[...] + pltpu.roll(t, HEAD_DIM // 2, axis=1) * sin_ref[...]

    head_fns = (lambda t: rope(t) * q_scale, rope, lambda t: t)

    def project_tasks(stage):
        slabs, uv_scr = stage[:3], stage[3]

        def norm():
            x = x_ref[...]
            h_scr[...] = (x * _rms_scale(x) * g1_ref[...]).astype(BF16)

        def project(col0):
            return jnp.dot(h_scr[...], w_ref[:, col0:col0 + MXU_COLS], preferred_element_type=F32)

        def uv_task(col):
            uv_scr[:, col:col + MXU_COLS] = project(3 * d_attn + col)

        def qkv_task(idx, first):
            proj = project(idx * d_attn + first * HEAD_DIM)
            for sub in range(heads_per_dot):
                val = head_fns[idx](proj[:, _head_cols(sub)])
                nat_refs[idx][:, _head_cols(first + sub)] = val.astype(BF16)
                slabs[idx][first + sub] = val

        tasks = [norm]
        tasks += [functools.partial(uv_task, col) for col in range(0, 2 * d_gmlp, MXU_COLS)]
        tasks += [functools.partial(qkv_task, idx, first) for idx in range(3)
                  for first in range(0, n_heads, heads_per_dot)]
        return tasks

    def finish_tasks(stage):
        slabs, uv_scr = stage[:3], stage[3]

        def copy_task(idx, level, r):
            d = dilations[level]
            d_prev = dilations[level - 1] if level else 1
            n = tm // d
            for hh in range(n_heads):
                if level:
                    start = (r % d_prev) * (tm // d_prev) + r // d_prev
                    val = mids[level - 1][idx][hh, pl.ds(start, n, stride=d // d_prev), :]
                else:
                    val = slabs[idx][hh, pl.ds(r, n, stride=d), :]
                res_refs[3 * level + idx][:, _head_cols(r * n_heads + hh)] = val.astype(BF16)
                if level + 1 < len(dilations):
                    mids[level][idx][hh, r * n:(r + 1) * n, :] = val

        def layernorm_task(rows):
            vv = _gelu(uv_scr[rows, d_gmlp:2 * d_gmlp])
            vc = vv - jnp.mean(vv, axis=-1, keepdims=True)
            vln = vc * lax.rsqrt(jnp.mean(vc * vc, axis=-1, keepdims=True) + EPS) * lng_ref[...]
            vln_scr[rows, :] = vln.astype(BF16)

        def mix_task(rows, g):
            cols = slice(g * GMLP_GROUP, (g + 1) * GMLP_GROUP)
            mixed = jnp.dot(ws_ref[g], vln_scr[rows, cols], preferred_element_type=F32)
            u = _gelu(uv_scr[rows, cols])
            gate_scr[rows, cols] = u * (mixed + bs_ref[:, cols])

        def gate_norm_task(rows):
            gate = gate_scr[rows, :]
            gn_ref[rows, :] = (gate * _rms_scale(gate) * g2_ref[...]).astype(BF16)

        tasks = []
        for c in range(tm // CHUNK):
            rows = slice(c * CHUNK, (c + 1) * CHUNK)
            tasks.append(functools.partial(layernorm_task, rows))
            tasks += [functools.partial(mix_task, rows, g) for g in range(n_groups)]
            tasks.append(functools.partial(gate_norm_task, rows))
        tasks += [functools.partial(copy_task, idx, level, r) for level, d in enumerate(dilations)
                  for idx in range(3) for r in range(d)]
        return tasks

    i = pl.program_id(0)
    interior = (i > 0) & (i < n_tiles)

    @pl.when(i == 0)
    def _():
        _interleave(project_tasks(stages[0]), [])

    for parity in range(2):
        @pl.when(interior & (i % 2 == parity))
        def _():
            _interleave(project_tasks(stages[parity]), finish_tasks(stages[1 - parity]))

    @pl.when(i == n_tiles)
    def _():
        _interleave(finish_tasks(stages[(n_tiles - 1) % 2]), [])


def _inproj(x2, g1, w_in, cos2, sin2, lng, ws, bsb, g2, *, layer, seq, d_attn, d_gmlp,
            dilations, tm):
    m, d_model = x2.shape
    d_in = w_in.shape[2]
    n_groups = d_gmlp // GMLP_GROUP
    n_heads = d_attn // HEAD_DIM
    pos_blocks = seq // tm
    n_tiles = m // tm
    assert all(tm % (d * 16) == 0 for d in dilations)
    assert all(d % d_prev == 0 for d, d_prev in zip(dilations, (1,) + tuple(dilations)))
    vec = lambda i: (layer, 0, 0)
    cur = lambda i: (jnp.minimum(i, n_tiles - 1), 0)
    lag = lambda i: (jnp.maximum(i - 1, 0), 0)
    pos = lambda i: (jnp.minimum(i, n_tiles - 1) % pos_blocks, 0)
    kern = functools.partial(_inproj_kernel, d_attn=d_attn, d_gmlp=d_gmlp,
                             q_scale=HEAD_DIM ** -0.5 * math.log2(math.e), dilations=dilations,
                             n_tiles=n_tiles)
    qkv_specs = [pl.BlockSpec((tm, d_attn), cur)] * 3
    qkv_shapes = [jax.ShapeDtypeStruct((m, d_attn), BF16)] * 3
    for d in dilations:
        qkv_specs += [pl.BlockSpec((tm // d, d * d_attn), lag)] * 3
        qkv_shapes += [jax.ShapeDtypeStruct((m // d, d * d_attn), BF16)] * 3
    stage = [pltpu.VMEM((n_heads, tm, HEAD_DIM), F32)] * 3 + [pltpu.VMEM((tm, 2 * d_gmlp), F32)]
    return pl.pallas_call(
        kern,
        grid=(n_tiles + 1,),
        in_specs=[
            pl.BlockSpec((tm, d_model), cur),
            pl.BlockSpec((None, 1, d_model), vec),
            pl.BlockSpec((None, d_model, d_in), vec, pipeline_mode=pl.Buffered(1)),
            pl.BlockSpec((tm, HEAD_DIM), pos),
            pl.BlockSpec((tm, HEAD_DIM), pos),
            pl.BlockSpec((None, 1, d_gmlp), vec),
            pl.BlockSpec((None, n_groups, CHUNK, CHUNK), lambda i: (layer, 0, 0, 0)),
            pl.BlockSpec((None, CHUNK, d_gmlp), vec),
            pl.BlockSpec((None, 1, d_gmlp), vec),
        ],
        out_specs=qkv_specs + [pl.BlockSpec((tm, d_gmlp), lag)],
        out_shape=qkv_shapes + [jax.ShapeDtypeStruct((m, d_gmlp), BF16)],
        scratch_shapes=stage * 2 + [
            pltpu.VMEM((tm, d_model), BF16),
            pltpu.VMEM((tm, d_gmlp), BF16),
            pltpu.VMEM((tm, d_gmlp), F32),
        ] + [pltpu.VMEM((n_heads, tm, HEAD_DIM), F32)] * (3 * (len(dilations) - 1)),
        compiler_params=pltpu.CompilerParams(
            dimension_semantics=("arbitrary",), vmem_limit_bytes=VMEM_LIMIT),
    )(x2, g1, w_in, cos2, sin2, lng, ws, bsb, g2)


def _attn_kernel(q_ref, kp_ref, k_ref, kn_ref, vp_ref, v_ref, vn_ref, o_ref, s_ref,
                 *, n_side, n_heads):
    tl = q_ref.shape[1]
    halo = kp_ref.shape[1]
    n_tiles = tl // Q_TILE
    t = pl.program_id(2)
    first_block = t == 0
    last_block = t == pl.num_programs(2) - 1

    kw = Q_TILE + 2 * halo
    row_i = lax.broadcasted_iota(jnp.int32, (Q_TILE, kw), 0)
    col_j = lax.broadcasted_iota(jnp.int32, (Q_TILE, kw), 1)
    rel = col_j - row_i - halo
    band = (rel >= -n_side) & (rel <= n_side)
    head_ok = (col_j >= halo) | jnp.logical_not(first_block)
    tail_ok = (col_j < halo + Q_TILE) | jnp.logical_not(last_block)
    lane = lax.broadcasted_iota(jnp.int32, (Q_TILE, LANES), 1)
    ones = jnp.ones((kw, HEAD_DIM), BF16)

    def window(prev_ref, main_ref, next_ref, i, cols):
        lo, hi = i * Q_TILE - halo, (i + 1) * Q_TILE + halo
        parts = []
        if lo < 0:
            parts.append(prev_ref[0, :, cols])
        parts.append(main_ref[0, max(lo, 0):min(hi, tl), cols])
        if hi > tl:
            parts.append(next_ref[0, :, cols])
        return parts[0] if len(parts) == 1 else jnp.concatenate(parts, axis=0)

    biases = {}
    for i in range(n_tiles):
        mask = band
        if i == 0:
            mask = mask & head_ok
        if i == n_tiles - 1:
            mask = mask & tail_ok
        biases[i] = jnp.where(mask, 0.0, NEG)

    n_res = q_ref.shape[2] // (n_heads * HEAD_DIM)
    for res, i in ((res, i) for res in range(n_res) for i in range(n_tiles)):
        rows = slice(i * Q_TILE, (i + 1) * Q_TILE)
        bias = biases[i]
        stats = jnp.ones((Q_TILE, LANES), F32)
        for hh in range(n_heads):
            cols = _head_cols(res * n_heads + hh)
            kh = window(kp_ref, k_ref, kn_ref, i, cols)
            vh = window(vp_ref, v_ref, vn_ref, i, cols)
            s = lax.dot_general(q_ref[0, rows, cols], kh, (((1,), (1,)), ((), ())),
                                preferred_element_type=F32) + bias
            mx = jnp.max(s, axis=-1, keepdims=True)
            p = jnp.exp2(s - mx).astype(BF16)
            ov = jnp.dot(p, jnp.concatenate([vh, ones], axis=1), preferred_element_type=F32)
            o_ref[0, rows, cols] = ov[:, :HEAD_DIM].astype(o_ref.dtype)
            stats = jnp.where(lane == hh, mx, stats)
            stats = jnp.where(lane == n_heads + hh, ov[:, HEAD_DIM:], stats)
        s_ref[0, rows, res * LANES:(res + 1) * LANES] = stats


def _attn_branch(q, k, v, *, batch, window, dilation):
    rows_total, width = q.shape
    c = width // dilation
    sub_len = rows_total // batch
    n_heads = c // HEAD_DIM
    n_side = window // (2 * dilation)
    halo = n_side
    assert halo * 2 == Q_TILE and 2 * n_heads <= LANES and sub_len % Q_TILE == 0
    tl = min(sub_len, ATTN_TILES_PER_STEP * Q_TILE)
    assert sub_len % tl == 0 and tl % halo == 0
    hb = tl // halo
    n_hb = sub_len // halo
    n_res = min(dilation, max(1, ATTN_TILES_PER_STEP * Q_TILE // tl))
    assert dilation % n_res == 0

    view = lambda a: a.reshape(batch, sub_len, a.shape[-1])
    qv, kv, vv = view(q), view(k), view(v)

    main = lambda bi, r, t: (bi, t, r)
    prev_halo = lambda bi, r, t: (bi, jnp.maximum(t * hb - 1, 0), r)
    next_halo = lambda bi, r, t: (bi, jnp.minimum((t + 1) * hb, n_hb - 1), r)
    big = pl.BlockSpec((1, tl, n_res * c), main)
    small_p = pl.BlockSpec((1, halo, n_res * c), prev_halo)
    small_n = pl.BlockSpec((1, halo, n_res * c), next_halo)
    stat_spec = pl.BlockSpec((1, tl, n_res * LANES), main)
    stat_shape = jax.ShapeDtypeStruct((batch, sub_len, dilation * LANES), F32)

    kern = functools.partial(_attn_kernel, n_side=n_side, n_heads=n_heads)
    o, stats = pl.pallas_call(
        kern,
        grid=(batch, dilation // n_res, sub_len // tl),
        in_specs=[big, small_p, big, small_n, small_p, big, small_n],
        out_specs=[big, stat_spec],
        out_shape=[jax.ShapeDtypeStruct((batch, sub_len, width), BF16), stat_shape],
        compiler_params=pltpu.CompilerParams(
            dimension_semantics=("arbitrary",) * 3, vmem_limit_bytes=VMEM_LIMIT),
    )(qv, kv, kv, kv, vv, vv, vv)
    flat = lambda a: a.reshape(rows_total, a.shape[-1])
    return flat(o), flat(stats)


def _outproj_kernel(*refs, dilations, n_heads, n_tiles):
    n_pat = 1 + len(dilations)
    n_dil = len(dilations)
    x_ref = refs[0]
    o_refs = refs[1:1 + 2 * n_pat:2]
    s_refs = refs[2:2 + 2 * n_pat:2]
    ga_ref, gn_ref, w_ref, out_ref = refs[1 + 2 * n_pat:5 + 2 * n_pat]
    scratch = refs[5 + 2 * n_pat:]
    o_slabs = scratch[0:n_dil]
    s_slabs = scratch[n_dil:2 * n_dil]
    an_stages = scratch[2 * n_dil:2 * n_dil + 2]
    a_slab = scratch[2 * n_dil + 2]
    o_mids = scratch[2 * n_dil + 3:3 * n_dil + 2]
    s_mids = scratch[3 * n_dil + 2:4 * n_dil + 1]

    tm, d_model = x_ref.shape
    d_attn = n_heads * HEAD_DIM
    state = {}

    def unpermute_task(idx, level, r):
        d_hi = dilations[level]
        d_lo = dilations[level - 1] if level else 1
        n = tm // d_hi
        dst = pl.ds((r % d_lo) * (tm // d_lo) + r // d_lo, n, stride=d_hi // d_lo)
        o_dst, s_dst = (o_mids[level - 1], s_mids[level - 1]) if level else (o_slabs[idx], s_slabs[idx])
        if level == idx:
            s_dst[dst, :] = s_refs[idx + 1][:, r * LANES:(r + 1) * LANES]
            for hh in range(n_heads):
                o_dst[hh, dst, :] = o_refs[idx + 1][:, _head_cols(r * n_heads + hh)].astype(F32)
        else:
            s_dst[dst, :] = s_mids[level][r * n:(r + 1) * n, :]
            for hh in range(n_heads):
                o_dst[hh, dst, :] = o_mids[level][hh, r * n:(r + 1) * n, :]

    def weights_task():
        mxs = [s_refs[0][...]] + [s_slab[...] for s_slab in s_slabs]
        dens = [pltpu.roll(stats, LANES - n_heads, axis=1) for stats in mxs]
        top = functools.reduce(jnp.maximum, mxs)
        es = [jnp.exp2(mx - top) for mx in mxs]
        inv = 1.0 / functools.reduce(jnp.add, [e * den for e, den in zip(es, dens)])
        state["ws"] = [e * inv for e in es]
        state["ssq"] = jnp.zeros((tm, 1), F32)

    def combine_task(hh):
        parts = [o_refs[0][:, _head_cols(hh)].astype(F32)] + [o_slab[hh] for o_slab in o_slabs]
        a = functools.reduce(jnp.add, [w[:, hh:hh + 1] * p for w, p in zip(state["ws"], parts)])
        state["ssq"] = state["ssq"] + jnp.sum(a * a, axis=-1, keepdims=True)
        a_slab[hh] = a

    def norm_task(an_scr, hh):
        if hh == 0:
            state["scale"] = lax.rsqrt(state["ssq"] / d_attn + EPS)
        an_scr[:, _head_cols(hh)] = (
            a_slab[hh] * state["scale"] * ga_ref[:, _head_cols(hh)]).astype(BF16)

    def merge_tasks(an_scr):
        state.clear()
        tasks = [functools.partial(unpermute_task, idx, level, r) for idx in range(n_dil)
                 for level in range(idx, -1, -1) for r in range(dilations[level])]
        tasks.append(weights_task)
        tasks += [functools.partial(combine_task, hh) for hh in range(n_heads)]
        tasks += [functools.partial(norm_task, an_scr, hh) for hh in range(n_heads)]
        return tasks

    def dot_tasks(an_scr):
        def dot_task(col):
            cols = slice(col, col + MXU_COLS)
            acc = jnp.dot(gn_ref[...], w_ref[d_attn:, cols], preferred_element_type=F32)
            acc = acc + jnp.dot(an_scr[...], w_ref[0:d_attn, cols], preferred_element_type=F32)
            out_ref[:, cols] = x_ref[:, cols] + acc

        return [functools.partial(dot_task, col) for col in range(0, d_model, MXU_COLS)]

    i = pl.program_id(0)
    interior = (i > 0) & (i < n_tiles)

    @pl.when(i == 0)
    def _():
        _interleave(merge_tasks(an_stages[0]), [])

    for parity in range(2):
        @pl.when(interior & (i % 2 == parity))
        def _():
            _interleave(dot_tasks(an_stages[1 - parity]), merge_tasks(an_stages[parity]))

    @pl.when(i == n_tiles)
    def _():
        _interleave(dot_tasks(an_stages[(n_tiles - 1) % 2]), [])


def _outproj(x2, branches, ga, gn, w_out, *, layer, dilations, tm):
    m, d_model = x2.shape
    d_attn = branches[0][0].shape[1]
    n_heads = d_attn // HEAD_DIM
    n_tiles = m // tm
    assert all(d % d_prev == 0 for d, d_prev in zip(dilations, (1,) + tuple(dilations)))
    cur = lambda i: (jnp.minimum(i, n_tiles - 1), 0)
    lag = lambda i: (jnp.maximum(i - 1, 0), 0)
    args, in_specs = [x2], [pl.BlockSpec((tm, d_model), lag)]
    for d, branch in zip((1,) + tuple(dilations), branches):
        args += list(branch)
        in_specs += [pl.BlockSpec((tm // d, d * d_attn), cur),
                     pl.BlockSpec((tm // d, d * LANES), cur)]
    args += [ga, gn, w_out]
    in_specs += [
        pl.BlockSpec((None, 1, d_attn), lambda i: (layer, 0, 0)),
        pl.BlockSpec((tm, gn.shape[1]), lag),
        pl.BlockSpec((None,) + w_out.shape[1:], lambda i: (layer, 0, 0)),
    ]
    return pl.pallas_call(
        functools.partial(_outproj_kernel, dilations=dilations, n_heads=n_heads,
                          n_tiles=n_tiles),
        grid=(n_tiles + 1,),
        in_specs=in_specs,
        out_specs=pl.BlockSpec((tm, d_model), lag),
        out_shape=jax.ShapeDtypeStruct((m, d_model), F32),
        scratch_shapes=(
            [pltpu.VMEM((n_heads, tm, HEAD_DIM), F32) for _ in dilations]
            + [pltpu.VMEM((tm, LANES), F32) for _ in dilations]
            + [pltpu.VMEM((tm, d_attn), BF16)] * 2
            + [pltpu.VMEM((n_heads, tm, HEAD_DIM), F32)] * len(dilations)
            + [pltpu.VMEM((tm, LANES), F32)] * (len(dilations) - 1)),
        compiler_params=pltpu.CompilerParams(
            dimension_semantics=("arbitrary",), vmem_limit_bytes=VMEM_LIMIT),
    )(*args)


def _ffn_kernel(*refs, final_norm):
    if final_norm:
        x_ref, g_ref, wg_ref, wu_ref, wd_ref, fg_ref, o_ref, h_scr = refs
    else:
        x_ref, g_ref, wg_ref, wu_ref, wd_ref, o_ref, h_scr = refs
    f = pl.program_id(1)

    @pl.when(f == 0)
    def _():
        x = x_ref[...]
        h_scr[...] = (x * _rms_scale(x) * g_ref[...]).astype(BF16)
        o_ref[...] = x

    h = h_scr[...]
    gate = jnp.dot(h, wg_ref[...], preferred_element_type=F32)
    up = jnp.dot(h, wu_ref[...], preferred_element_type=F32)
    ff = (gate * (1.0 / (1.0 + jnp.exp(-gate))) * up).astype(BF16)
    o_ref[...] += jnp.dot(ff, wd_ref[...], preferred_element_type=F32)

    if final_norm:
        @pl.when(f == pl.num_programs(1) - 1)
        def _():
            y = o_ref[...]
            o_ref[...] = y * _rms_scale(y) * fg_ref[...]


def _ffn(x2, g, w_gate, w_up, w_down, final_g, *, layer, tm, tf):
    m, d_model = x2.shape
    d_ff = w_gate.shape[2]
    final_norm = final_g is not None
    args = [x2, g, w_gate, w_up, w_down]
    in_specs = [
        pl.BlockSpec((tm, d_model), lambda i, f: (i, 0)),
        pl.BlockSpec((None, 1, d_model), lambda i, f: (layer, 0, 0)),
        pl.BlockSpec((None, d_model, tf), lambda i, f: (layer, 0, f)),
        pl.BlockSpec((None, d_model, tf), lambda i, f: (layer, 0, f)),
        pl.BlockSpec((None, tf, d_model), lambda i, f: (layer, f, 0)),
    ]
    if final_norm:
        args.append(final_g)
        in_specs.append(pl.BlockSpec((1, d_model), lambda i, f: (0, 0)))
    return pl.pallas_call(
        functools.partial(_ffn_kernel, final_norm=final_norm),
        grid=(m // tm, d_ff // tf),
        in_specs=in_specs,
        out_specs=pl.BlockSpec((tm, d_model), lambda i, f: (i, 0)),
        out_shape=jax.ShapeDtypeStruct((m, d_model), F32),
        scratch_shapes=[pltpu.VMEM((tm, d_model), BF16)],
        input_output_aliases={0: 0},
        compiler_params=pltpu.CompilerParams(
            dimension_semantics=("arbitrary", "arbitrary"), vmem_limit_bytes=VMEM_LIMIT),
    )(*args)


def _rope_tables(seq):
    pos = jnp.arange(seq, dtype=F32)
    inv = ROPE_THETA ** (-jnp.arange(0, HEAD_DIM, 2, dtype=F32) / HEAD_DIM)
    ang = pos[:, None] * inv[None, :]
    cos, sin = jnp.cos(ang), jnp.sin(ang)
    return jnp.concatenate([cos, cos], axis=-1), jnp.concatenate([-sin, sin], axis=-1)


def kernel(x, norm1_g, w_in, gmlp_ln_g, w_spatial, b_spatial, mix_norm_attn_g,
           mix_norm_gmlp_g, w_out, norm2_g, w_gate, w_up, w_down, final_g):
    b, s, d_model = x.shape
    depth = w_in.shape[0]
    d_gmlp = gmlp_ln_g.shape[1]
    d_attn = mix_norm_attn_g.shape[1]
    n_groups = w_spatial.shape[1]
    d_ff = w_gate.shape[2]
    assert w_in.shape[2] == 3 * d_attn + 2 * d_gmlp and n_groups * GMLP_GROUP == d_gmlp
    assert w_spatial.shape[2] == CHUNK
    assert DILATED_PATTERNS[0][1] == 1
    dilations = tuple(d for _, d in DILATED_PATTERNS[1:])

    tm_in, tm_out, tm_ffn, tf = 256, 512, 1024, 512
    assert s % tm_in == 0 and s % tm_out == 0 and s % CHUNK == 0 and d_ff % tf == 0

    cos2, sin2 = _rope_tables(s)
    x2 = x.reshape(b * s, d_model)
    vec3 = lambda a: a.reshape(a.shape[0], 1, a.shape[1])

    w_in_b, w_out_b = w_in.astype(BF16), w_out.astype(BF16)
    w_gate_b, w_up_b, w_down_b = w_gate.astype(BF16), w_up.astype(BF16), w_down.astype(BF16)
    ws_b = w_spatial.astype(BF16)
    bsb = jnp.repeat(jnp.swapaxes(b_spatial, 1, 2), GMLP_GROUP, axis=2)
    g1, lng, g2, ga, gf = (vec3(a) for a in (norm1_g, gmlp_ln_g, mix_norm_gmlp_g,
                                              mix_norm_attn_g, norm2_g))

    for l in range(depth):
        outs = _inproj(x2, g1, w_in_b, cos2, sin2, lng, ws_b, bsb, g2, layer=l, seq=s,
                       d_attn=d_attn, d_gmlp=d_gmlp, dilations=dilations, tm=tm_in)
        gn = outs[-1]
        branches = [
            _attn_branch(*outs[3 * i:3 * i + 3], batch=b, window=window, dilation=dilation)
            for i, (window, dilation) in enumerate(DILATED_PATTERNS)]
        x2 = _outproj(x2, branches, ga, gn, w_out_b, layer=l, dilations=dilations, tm=tm_out)
        x2 = _ffn(x2, gf, w_gate_b, w_up_b, w_down_b,
                  final_g.reshape(1, -1) if l == depth - 1 else None,
                  layer=l, tm=tm_ffn, tf=tf)
    return x2.reshape(b, s, d_model)
```

```python
import functools
import math
from typing import NamedTuple

import jax
import jax.numpy as jnp
from jax import lax
from jax.experimental import pallas as pl
from jax.experimental.pallas import tpu as pltpu

HEAD_DIM = 128
GMLP_GROUP = 128
CHUNK = 128
DILATED_PATTERNS = ((128, 1), (512, 4), (2048, 16))
ROPE_THETA = 10000.0
EPS = 1e-6
NEG = -1e30

LANES = 128
BF16_SUBLANES = 16
MXU_COLS = 256
V7X_VMEM_BYTES = 64 * 1024 * 1024
VMEM_LIMIT = V7X_VMEM_BYTES * 7 // 8
COMPILER_VMEM_BYTES = 4 * 1024 * 1024
Q_TILE = 128

BF16 = jnp.bfloat16
F32 = jnp.float32


class _Tiles(NamedTuple):
    tm_in: int
    tm_out: int
    tm_ffn: int
    tf: int
    attn_tiles: int


def _plan_tiles(seq, d_model, d_attn, d_gmlp, d_ff, n_dil):
    budget = VMEM_LIMIT - COMPILER_VMEM_BYTES
    f32, bf16 = 4, 2
    d_in = 3 * d_attn + 2 * d_gmlp

    def inproj(tm):
        windows = 2 * (tm * d_model * f32 + 3 * (1 + n_dil) * tm * d_attn * bf16
                       + tm * d_gmlp * bf16 + 2 * tm * HEAD_DIM * f32 + CHUNK * d_gmlp * f32)
        stages = 2 * (3 * tm * d_attn + 2 * tm * d_gmlp) * f32
        scratch = (tm * d_model * bf16 + tm * d_gmlp * (bf16 + f32)
                   + 3 * (n_dil - 1) * tm * d_attn * f32)
        return d_model * d_in * bf16 + windows + stages + scratch

    def outproj(tm):
        windows = 2 * (2 * tm * d_model * f32 + (1 + n_dil) * tm * (d_attn * bf16 + LANES * f32)
                       + tm * d_gmlp * bf16)
        scratch = ((2 * n_dil) * tm * d_attn * f32 + (2 * n_dil - 1) * tm * LANES * f32
                   + 2 * tm * d_attn * bf16)
        return d_model * d_model * bf16 + windows + scratch

    def ffn(tm, tf):
        return 2 * (2 * tm * d_model * f32 + 3 * d_model * tf * bf16) + tm * d_model * bf16

    def attn(tiles):
        rows = tiles * Q_TILE
        return 2 * (4 * rows * d_attn * bf16 + 4 * (Q_TILE // 2) * d_attn * bf16
                    + rows * LANES * f32)

    def largest(candidates, cost):
        return next(c for c in candidates if cost(c) <= budget)

    row_tiles = [t for t in (2048, 1024, 512, 256, 128) if seq % t == 0]
    tf = largest([t for t in (1024, 512, 256, 128) if d_ff % t == 0], lambda t: ffn(128, t))
    return _Tiles(
        tm_in=largest(row_tiles, inproj), tm_out=largest(row_tiles, outproj),
        tm_ffn=largest(row_tiles, lambda t: ffn(t, tf)), tf=tf,
        attn_tiles=largest((32, 16, 8, 4, 2, 1), attn))


def _gelu(x):
    return 0.5 * x * (1.0 + lax.erf(x * (1.0 / math.sqrt(2.0))))


def _rms_scale(x):
    return lax.rsqrt(jnp.mean(x * x, axis=-1, keepdims=True) + EPS)


def _head_cols(hh):
    return slice(hh * HEAD_DIM, (hh + 1) * HEAD_DIM)


def _interleave(major, minor):
    done = 0
    for j, task in enumerate(major):
        task()
        upto = (j + 1) * len(minor) // len(major)
        for other in minor[done:upto]:
            other()
        done = upto


def _inproj_kernel(*refs, d_attn, d_gmlp, q_scale, dilations, n_tiles):
    (x_ref, g1_ref, w_ref, cos_ref, sin_ref, lng_ref, ws_ref, bs_ref, g2_ref) = refs[:9]
    n_res_out = 3 * len(dilations)
    nat_refs = refs[9:12]
    res_refs = refs[12:12 + n_res_out]
    gn_ref = refs[12 + n_res_out]
    scratch = refs[13 + n_res_out:]
    stages = (scratch[0:4], scratch[4:8])
    h_scr, vln_scr, gate_scr = scratch[8:11]
    mids = [scratch[11 + 3 * level:14 + 3 * level] for level in range(len(dilations) - 1)]

    tm = x_ref.shape[0]
    n_heads = d_attn // HEAD_DIM
    n_groups = d_gmlp // GMLP_GROUP
    heads_per_dot = MXU_COLS // HEAD_DIM

    def rope(t):
        return t * cos_ref[...] + pltpu.roll(t, HEAD_DIM // 2, axis=1) * sin_ref[...]

    head_fns = (lambda t: rope(t) * q_scale, rope, lambda t: t)

    def project_tasks(stage):
        slabs, uv_scr = stage[:3], stage[3]

        def norm():
            x = x_ref[...]
            h_scr[...] = (x * _rms_scale(x) * g1_ref[...]).astype(BF16)

        def project(col0):
            return jnp.dot(h_scr[...], w_ref[:, col0:col0 + MXU_COLS], preferred_element_type=F32)

        def uv_task(col):
            uv_scr[:, col:col + MXU_COLS] = project(3 * d_attn + col)

        def qkv_task(idx, first):
            proj = project(idx * d_attn + first * HEAD_DIM)
            for sub in range(heads_per_dot):
                val = head_fns[idx](proj[:, _head_cols(sub)])
                nat_refs[idx][:, _head_cols(first + sub)] = val.astype(BF16)
                slabs[idx][first + sub] = val

        tasks = [norm]
        tasks += [functools.partial(uv_task, col) for col in range(0, 2 * d_gmlp, MXU_COLS)]
        tasks += [functools.partial(qkv_task, idx, first) for idx in range(3)
                  for first in range(0, n_heads, heads_per_dot)]
        return tasks

    def finish_tasks(stage):
        slabs, uv_scr = stage[:3], stage[3]

        def copy_task(idx, level, r):
            d = dilations[level]
            d_prev = dilations[level - 1] if level else 1
            n = tm // d
            for hh in range(n_heads):
                if level:
                    start = (r % d_prev) * (tm // d_prev) + r // d_prev
                    val = mids[level - 1][idx][hh, pl.ds(start, n, stride=d // d_prev), :]
                else:
                    val = slabs[idx][hh, pl.ds(r, n, stride=d), :]
                res_refs[3 * level + idx][:, _head_cols(r * n_heads + hh)] = val.astype(BF16)
                if level + 1 < len(dilations):
                    mids[level][idx][hh, r * n:(r + 1) * n, :] = val

        def layernorm_task(rows):
            vv = _gelu(uv_scr[rows, d_gmlp:2 * d_gmlp])
            vc = vv - jnp.mean(vv, axis=-1, keepdims=True)
            vln = vc * lax.rsqrt(jnp.mean(vc * vc, axis=-1, keepdims=True) + EPS) * lng_ref[...]
            vln_scr[rows, :] = vln.astype(BF16)

        def mix_task(rows, g):
            cols = slice(g * GMLP_GROUP, (g + 1) * GMLP_GROUP)
            mixed = jnp.dot(ws_ref[g], vln_scr[rows, cols], preferred_element_type=F32)
            u = _gelu(uv_scr[rows, cols])
            gate_scr[rows, cols] = u * (mixed + bs_ref[:, cols])

        def gate_norm_task(rows):
            gate = gate_scr[rows, :]
            gn_ref[rows, :] = (gate * _rms_scale(gate) * g2_ref[...]).astype(BF16)

        tasks = []
        for c in range(tm // CHUNK):
            rows = slice(c * CHUNK, (c + 1) * CHUNK)
            tasks.append(functools.partial(layernorm_task, rows))
            tasks += [functools.partial(mix_task, rows, g) for g in range(n_groups)]
            tasks.append(functools.partial(gate_norm_task, rows))
        tasks += [functools.partial(copy_task, idx, level, r) for level, d in enumerate(dilations)
                  for idx in range(3) for r in range(d)]
        return tasks

    i = pl.program_id(0)
    interior = (i > 0) & (i < n_tiles)

    @pl.when(i == 0)
    def _():
        _interleave(project_tasks(stages[0]), [])

    for parity in range(2):
        @pl.when(interior & (i % 2 == parity))
        def _():
            _interleave(project_tasks(stages[parity]), finish_tasks(stages[1 - parity]))

    @pl.when(i == n_tiles)
    def _():
        _interleave(finish_tasks(stages[(n_tiles - 1) % 2]), [])


def _inproj(x2, g1, w_in, cos2, sin2, lng, ws, bsb, g2, *, layer, seq, d_attn, d_gmlp,
            dilations, tm):
    m, d_model = x2.shape
    d_in = w_in.shape[2]
    n_groups = d_gmlp // GMLP_GROUP
    n_heads = d_attn // HEAD_DIM
    pos_blocks = seq // tm
    n_tiles = m // tm
    assert all(tm % (d * BF16_SUBLANES) == 0 for d in dilations)
    assert all(d % d_prev == 0 for d, d_prev in zip(dilations, (1,) + tuple(dilations)))
    vec = lambda i: (layer, 0, 0)
    cur = lambda i: (jnp.minimum(i, n_tiles - 1), 0)
    lag = lambda i: (jnp.maximum(i - 1, 0), 0)
    pos = lambda i: (jnp.minimum(i, n_tiles - 1) % pos_blocks, 0)
    kern = functools.partial(_inproj_kernel, d_attn=d_attn, d_gmlp=d_gmlp,
                             q_scale=HEAD_DIM ** -0.5 * math.log2(math.e), dilations=dilations,
                             n_tiles=n_tiles)
    qkv_specs = [pl.BlockSpec((tm, d_attn), cur)] * 3
    qkv_shapes = [jax.ShapeDtypeStruct((m, d_attn), BF16)] * 3
    for d in dilations:
        qkv_specs += [pl.BlockSpec((tm // d, d * d_attn), lag)] * 3
        qkv_shapes += [jax.ShapeDtypeStruct((m // d, d * d_attn), BF16)] * 3
    stage = [pltpu.VMEM((n_heads, tm, HEAD_DIM), F32)] * 3 + [pltpu.VMEM((tm, 2 * d_gmlp), F32)]
    return pl.pallas_call(
        kern,
        grid=(n_tiles + 1,),
        in_specs=[
            pl.BlockSpec((tm, d_model), cur),
            pl.BlockSpec((None, 1, d_model), vec),
            pl.BlockSpec((None, d_model, d_in), vec, pipeline_mode=pl.Buffered(1)),
            pl.BlockSpec((tm, HEAD_DIM), pos),
            pl.BlockSpec((tm, HEAD_DIM), pos),
            pl.BlockSpec((None, 1, d_gmlp), vec),
            pl.BlockSpec((None, n_groups, CHUNK, CHUNK), lambda i: (layer, 0, 0, 0)),
            pl.BlockSpec((None, CHUNK, d_gmlp), vec),
            pl.BlockSpec((None, 1, d_gmlp), vec),
        ],
        out_specs=qkv_specs + [pl.BlockSpec((tm, d_gmlp), lag)],
        out_shape=qkv_shapes + [jax.ShapeDtypeStruct((m, d_gmlp), BF16)],
        scratch_shapes=stage * 2 + [
            pltpu.VMEM((tm, d_model), BF16),
            pltpu.VMEM((tm, d_gmlp), BF16),
            pltpu.VMEM((tm, d_gmlp), F32),
        ] + [pltpu.VMEM((n_heads, tm, HEAD_DIM), F32)] * (3 * (len(dilations) - 1)),
        compiler_params=pltpu.CompilerParams(
            dimension_semantics=("arbitrary",), vmem_limit_bytes=VMEM_LIMIT),
    )(x2, g1, w_in, cos2, sin2, lng, ws, bsb, g2)


def _attn_kernel(q_ref, kp_ref, k_ref, kn_ref, vp_ref, v_ref, vn_ref, o_ref, s_ref,
                 *, n_side, n_heads):
    tl = q_ref.shape[1]
    halo = kp_ref.shape[1]
    n_tiles = tl // Q_TILE
    t = pl.program_id(2)
    first_block = t == 0
    last_block = t == pl.num_programs(2) - 1

    kw = Q_TILE + 2 * halo
    row_i = lax.broadcasted_iota(jnp.int32, (Q_TILE, kw), 0)
    col_j = lax.broadcasted_iota(jnp.int32, (Q_TILE, kw), 1)
    rel = col_j - row_i - halo
    band = (rel >= -n_side) & (rel <= n_side)
    head_ok = (col_j >= halo) | jnp.logical_not(first_block)
    tail_ok = (col_j < halo + Q_TILE) | jnp.logical_not(last_block)
    lane = lax.broadcasted_iota(jnp.int32, (Q_TILE, LANES), 1)
    ones = jnp.ones((kw, HEAD_DIM), BF16)

    def window(prev_ref, main_ref, next_ref, i, cols):
        lo, hi = i * Q_TILE - halo, (i + 1) * Q_TILE + halo
        parts = []
        if lo < 0:
            parts.append(prev_ref[0, :, cols])
        parts.append(main_ref[0, max(lo, 0):min(hi, tl), cols])
        if hi > tl:
            parts.append(next_ref[0, :, cols])
        return parts[0] if len(parts) == 1 else jnp.concatenate(parts, axis=0)

    biases = {}
    for i in range(n_tiles):
        mask = band
        if i == 0:
            mask = mask & head_ok
        if i == n_tiles - 1:
            mask = mask & tail_ok
        biases[i] = jnp.where(mask, 0.0, NEG)

    n_res = q_ref.shape[2] // (n_heads * HEAD_DIM)
    for res, i in ((res, i) for res in range(n_res) for i in range(n_tiles)):
        rows = slice(i * Q_TILE, (i + 1) * Q_TILE)
        bias = biases[i]
        stats = jnp.ones((Q_TILE, LANES), F32)
        for hh in range(n_heads):
            cols = _head_cols(res * n_heads + hh)
            kh = window(kp_ref, k_ref, kn_ref, i, cols)
            vh = window(vp_ref, v_ref, vn_ref, i, cols)
            s = lax.dot_general(q_ref[0, rows, cols], kh, (((1,), (1,)), ((), ())),
                                preferred_element_type=F32) + bias
            mx = jnp.max(s, axis=-1, keepdims=True)
            p = jnp.exp2(s - mx).astype(BF16)
            ov = jnp.dot(p, jnp.concatenate([vh, ones], axis=1), preferred_element_type=F32)
            o_ref[0, rows, cols] = ov[:, :HEAD_DIM].astype(o_ref.dtype)
            stats = jnp.where(lane == hh, mx, stats)
            stats = jnp.where(lane == n_heads + hh, ov[:, HEAD_DIM:], stats)
        s_ref[0, rows, res * LANES:(res + 1) * LANES] = stats


def _attn_branch(q, k, v, *, batch, window, dilation, tiles_per_step):
    rows_total, width = q.shape
    c = width // dilation
    sub_len = rows_total // batch
    n_heads = c // HEAD_DIM
    n_side = window // (2 * dilation)
    halo = n_side
    assert halo * 2 == Q_TILE and 2 * n_heads <= LANES and sub_len % Q_TILE == 0
    tl = min(sub_len, tiles_per_step * Q_TILE)
    assert sub_len % tl == 0 and tl % halo == 0
    hb = tl // halo
    n_hb = sub_len // halo
    n_res = min(dilation, max(1, tiles_per_step * Q_TILE // tl))
    assert dilation % n_res == 0

    view = lambda a: a.reshape(batch, sub_len, a.shape[-1])
    qv, kv, vv = view(q), view(k), view(v)

    main = lambda bi, r, t: (bi, t, r)
    prev_halo = lambda bi, r, t: (bi, jnp.maximum(t * hb - 1, 0), r)
    next_halo = lambda bi, r, t: (bi, jnp.minimum((t + 1) * hb, n_hb - 1), r)
    big = pl.BlockSpec((1, tl, n_res * c), main)
    small_p = pl.BlockSpec((1, halo, n_res * c), prev_halo)
    small_n = pl.BlockSpec((1, halo, n_res * c), next_halo)
    stat_spec = pl.BlockSpec((1, tl, n_res * LANES), main)
    stat_shape = jax.ShapeDtypeStruct((batch, sub_len, dilation * LANES), F32)

    kern = functools.partial(_attn_kernel, n_side=n_side, n_heads=n_heads)
    o, stats = pl.pallas_call(
        kern,
        grid=(batch, dilation // n_res, sub_len // tl),
        in_specs=[big, small_p, big, small_n, small_p, big, small_n],
        out_specs=[big, stat_spec],
        out_shape=[jax.ShapeDtypeStruct((batch, sub_len, width), BF16), stat_shape],
        compiler_params=pltpu.CompilerParams(
            dimension_semantics=("arbitrary",) * 3, vmem_limit_bytes=VMEM_LIMIT),
    )(qv, kv, kv, kv, vv, vv, vv)
    flat = lambda a: a.reshape(rows_total, a.shape[-1])
    return flat(o), flat(stats)


def _outproj_kernel(*refs, dilations, n_heads, n_tiles):
    n_pat = 1 + len(dilations)
    n_dil = len(dilations)
    x_ref = refs[0]
    o_refs = refs[1:1 + 2 * n_pat:2]
    s_refs = refs[2:2 + 2 * n_pat:2]
    ga_ref, gn_ref, w_ref, out_ref = refs[1 + 2 * n_pat:5 + 2 * n_pat]
    scratch = refs[5 + 2 * n_pat:]
    o_slabs = scratch[0:n_dil]
    s_slabs = scratch[n_dil:2 * n_dil]
    an_stages = scratch[2 * n_dil:2 * n_dil + 2]
    a_slab = scratch[2 * n_dil + 2]
    o_mids = scratch[2 * n_dil + 3:3 * n_dil + 2]
    s_mids = scratch[3 * n_dil + 2:4 * n_dil + 1]

    tm, d_model = x_ref.shape
    d_attn = n_heads * HEAD_DIM
    state = {}

    def unpermute_task(idx, level, r):
        d_hi = dilations[level]
        d_lo = dilations[level - 1] if level else 1
        n = tm // d_hi
        dst = pl.ds((r % d_lo) * (tm // d_lo) + r // d_lo, n, stride=d_hi // d_lo)
        o_dst, s_dst = (o_mids[level - 1], s_mids[level - 1]) if level else (o_slabs[idx], s_slabs[idx])
        if level == idx:
            s_dst[dst, :] = s_refs[idx + 1][:, r * LANES:(r + 1) * LANES]
            for hh in range(n_heads):
                o_dst[hh, dst, :] = o_refs[idx + 1][:, _head_cols(r * n_heads + hh)].astype(F32)
        else:
            s_dst[dst, :] = s_mids[level][r * n:(r + 1) * n, :]
            for hh in range(n_heads):
                o_dst[hh, dst, :] = o_mids[level][hh, r * n:(r + 1) * n, :]

    def weights_task():
        mxs = [s_refs[0][...]] + [s_slab[...] for s_slab in s_slabs]
        dens = [pltpu.roll(stats, LANES - n_heads, axis=1) for stats in mxs]
        top = functools.reduce(jnp.maximum, mxs)
        es = [jnp.exp2(mx - top) for mx in mxs]
        inv = 1.0 / functools.reduce(jnp.add, [e * den for e, den in zip(es, dens)])
        state["ws"] = [e * inv for e in es]
        state["ssq"] = jnp.zeros((tm, 1), F32)

    def combine_task(hh):
        parts = [o_refs[0][:, _head_cols(hh)].astype(F32)] + [o_slab[hh] for o_slab in o_slabs]
        a = functools.reduce(jnp.add, [w[:, hh:hh + 1] * p for w, p in zip(state["ws"], parts)])
        state["ssq"] = state["ssq"] + jnp.sum(a * a, axis=-1, keepdims=True)
        a_slab[hh] = a

    def norm_task(an_scr, hh):
        if hh == 0:
            state["scale"] = lax.rsqrt(state["ssq"] / d_attn + EPS)
        an_scr[:, _head_cols(hh)] = (
            a_slab[hh] * state["scale"] * ga_ref[:, _head_cols(hh)]).astype(BF16)

    def merge_tasks(an_scr):
        state.clear()
        tasks = [functools.partial(unpermute_task, idx, level, r) for idx in range(n_dil)
                 for level in range(idx, -1, -1) for r in range(dilations[level])]
        tasks.append(weights_task)
        tasks += [functools.partial(combine_task, hh) for hh in range(n_heads)]
        tasks += [functools.partial(norm_task, an_scr, hh) for hh in range(n_heads)]
        return tasks

    def dot_tasks(an_scr):
        def dot_task(col):
            cols = slice(col, col + MXU_COLS)
            acc = jnp.dot(gn_ref[...], w_ref[d_attn:, cols], preferred_element_type=F32)
            acc = acc + jnp.dot(an_scr[...], w_ref[0:d_attn, cols], preferred_element_type=F32)
            out_ref[:, cols] = x_ref[:, cols] + acc

        return [functools.partial(dot_task, col) for col in range(0, d_model, MXU_COLS)]

    i = pl.program_id(0)
    interior = (i > 0) & (i < n_tiles)

    @pl.when(i == 0)
    def _():
        _interleave(merge_tasks(an_stages[0]), [])

    for parity in range(2):
        @pl.when(interior & (i % 2 == parity))
        def _():
            _interleave(dot_tasks(an_stages[1 - parity]), merge_tasks(an_stages[parity]))

    @pl.when(i == n_tiles)
    def _():
        _interleave(dot_tasks(an_stages[(n_tiles - 1) % 2]), [])


def _outproj(x2, branches, ga, gn, w_out, *, layer, dilations, tm):
    m, d_model = x2.shape
    d_attn = branches[0][0].shape[1]
    n_heads = d_attn // HEAD_DIM
    n_tiles = m // tm
    assert all(d % d_prev == 0 for d, d_prev in zip(dilations, (1,) + tuple(dilations)))
    cur = lambda i: (jnp.minimum(i, n_tiles - 1), 0)
    lag = lambda i: (jnp.maximum(i - 1, 0), 0)
    args, in_specs = [x2], [pl.BlockSpec((tm, d_model), lag)]
    for d, branch in zip((1,) + tuple(dilations), branches):
        args += list(branch)
        in_specs += [pl.BlockSpec((tm // d, d * d_attn), cur),
                     pl.BlockSpec((tm // d, d * LANES), cur)]
    args += [ga, gn, w_out]
    in_specs += [
        pl.BlockSpec((None, 1, d_attn), lambda i: (layer, 0, 0)),
        pl.BlockSpec((tm, gn.shape[1]), lag),
        pl.BlockSpec((None,) + w_out.shape[1:], lambda i: (layer, 0, 0),
                     pipeline_mode=pl.Buffered(1)),
    ]
    return pl.pallas_call(
        functools.partial(_outproj_kernel, dilations=dilations, n_heads=n_heads,
                          n_tiles=n_tiles),
        grid=(n_tiles + 1,),
        in_specs=in_specs,
        out_specs=pl.BlockSpec((tm, d_model), lag),
        out_shape=jax.ShapeDtypeStruct((m, d_model), F32),
        scratch_shapes=(
            [pltpu.VMEM((n_heads, tm, HEAD_DIM), F32) for _ in dilations]
            + [pltpu.VMEM((tm, LANES), F32) for _ in dilations]
            + [pltpu.VMEM((tm, d_attn), BF16)] * 2
            + [pltpu.VMEM((n_heads, tm, HEAD_DIM), F32)] * len(dilations)
            + [pltpu.VMEM((tm, LANES), F32)] * (len(dilations) - 1)),
        compiler_params=pltpu.CompilerParams(
            dimension_semantics=("arbitrary",), vmem_limit_bytes=VMEM_LIMIT),
    )(*args)


def _ffn_kernel(*refs, final_norm):
    if final_norm:
        x_ref, g_ref, wg_ref, wu_ref, wd_ref, fg_ref, o_ref, h_scr = refs
    else:
        x_ref, g_ref, wg_ref, wu_ref, wd_ref, o_ref, h_scr = refs
    f = pl.program_id(1)

    @pl.when(f == 0)
    def _():
        x = x_ref[...]
        h_scr[...] = (x * _rms_scale(x) * g_ref[...]).astype(BF16)
        o_ref[...] = x

    h = h_scr[...]
    gate = jnp.dot(h, wg_ref[...], preferred_element_type=F32)
    up = jnp.dot(h, wu_ref[...], preferred_element_type=F32)
    ff = (gate * (1.0 / (1.0 + jnp.exp(-gate))) * up).astype(BF16)
    o_ref[...] += jnp.dot(ff, wd_ref[...], preferred_element_type=F32)

    if final_norm:
        @pl.when(f == pl.num_programs(1) - 1)
        def _():
            y = o_ref[...]
            o_ref[...] = y * _rms_scale(y) * fg_ref[...]


def _ffn(x2, g, w_gate, w_up, w_down, final_g, *, layer, tm, tf):
    m, d_model = x2.shape
    d_ff = w_gate.shape[2]
    final_norm = final_g is not None
    args = [x2, g, w_gate, w_up, w_down]
    in_specs = [
        pl.BlockSpec((tm, d_model), lambda i, f: (i, 0)),
        pl.BlockSpec((None, 1, d_model), lambda i, f: (layer, 0, 0)),
        pl.BlockSpec((None, d_model, tf), lambda i, f: (layer, 0, f)),
        pl.BlockSpec((None, d_model, tf), lambda i, f: (layer, 0, f)),
        pl.BlockSpec((None, tf, d_model), lambda i, f: (layer, f, 0)),
    ]
    if final_norm:
        args.append(final_g)
        in_specs.append(pl.BlockSpec((1, d_model), lambda i, f: (0, 0)))
    return pl.pallas_call(
        functools.partial(_ffn_kernel, final_norm=final_norm),
        grid=(m // tm, d_ff // tf),
        in_specs=in_specs,
        out_specs=pl.BlockSpec((tm, d_model), lambda i, f: (i, 0)),
        out_shape=jax.ShapeDtypeStruct((m, d_model), F32),
        scratch_shapes=[pltpu.VMEM((tm, d_model), BF16)],
        input_output_aliases={0: 0},
        compiler_params=pltpu.CompilerParams(
            dimension_semantics=("arbitrary", "arbitrary"), vmem_limit_bytes=VMEM_LIMIT),
    )(*args)


def _rope_tables(seq):
    pos = jnp.arange(seq, dtype=F32)
    inv = ROPE_THETA ** (-jnp.arange(0, HEAD_DIM, 2, dtype=F32) / HEAD_DIM)
    ang = pos[:, None] * inv[None, :]
    cos, sin = jnp.cos(ang), jnp.sin(ang)
    return jnp.concatenate([cos, cos], axis=-1), jnp.concatenate([-sin, sin], axis=-1)


def kernel(x, norm1_g, w_in, gmlp_ln_g, w_spatial, b_spatial, mix_norm_attn_g,
           mix_norm_gmlp_g, w_out, norm2_g, w_gate, w_up, w_down, final_g):
    b, s, d_model = x.shape
    depth = w_in.shape[0]
    d_gmlp = gmlp_ln_g.shape[1]
    d_attn = mix_norm_attn_g.shape[1]
    n_groups = w_spatial.shape[1]
    d_ff = w_gate.shape[2]
    assert w_in.shape[2] == 3 * d_attn + 2 * d_gmlp and n_groups * GMLP_GROUP == d_gmlp
    assert w_spatial.shape[2] == CHUNK
    assert DILATED_PATTERNS[0][1] == 1
    dilations = tuple(d for _, d in DILATED_PATTERNS[1:])

    tiles = _plan_tiles(s, d_model, d_attn, d_gmlp, d_ff, len(dilations))
    assert s % CHUNK == 0 and tiles.tm_in % CHUNK == 0

    cos2, sin2 = _rope_tables(s)
    x2 = x.reshape(b * s, d_model)
    vec3 = lambda a: a.reshape(a.shape[0], 1, a.shape[1])

    w_in_b, w_out_b = w_in.astype(BF16), w_out.astype(BF16)
    w_gate_b, w_up_b, w_down_b = w_gate.astype(BF16), w_up.astype(BF16), w_down.astype(BF16)
    ws_b = w_spatial.astype(BF16)
    bsb = jnp.repeat(jnp.swapaxes(b_spatial, 1, 2), GMLP_GROUP, axis=2)
    g1, lng, g2, ga, gf = (vec3(a) for a in (norm1_g, gmlp_ln_g, mix_norm_gmlp_g,
                                              mix_norm_attn_g, norm2_g))

    for l in range(depth):
        outs = _inproj(x2, g1, w_in_b, cos2, sin2, lng, ws_b, bsb, g2, layer=l, seq=s,
                       d_attn=d_attn, d_gmlp=d_gmlp, dilations=dilations, tm=tiles.tm_in)
        gn = outs[-1]
        branches = [
            _attn_branch(*outs[3 * i:3 * i + 3], batch=b, window=window, dilation=dilation,
                         tiles_per_step=tiles.attn_tiles)
            for i, (window, dilation) in enumerate(DILATED_PATTERNS)]
        x2 = _outproj(x2, branches, ga, gn, w_out_b, layer=l, dilations=dilations,
                      tm=tiles.tm_out)
        x2 = _ffn(x2, gf, w_gate_b, w_up_b, w_down_b,
                  final_g.reshape(1, -1) if l == depth - 1 else None,
                  layer=l, tm=tiles.tm_ffn, tf=tiles.tf)
    return x2.reshape(b, s, d_model)
```

```python
import functools
import math
from typing import NamedTuple

import jax
import jax.numpy as jnp
from jax import lax
from jax.experimental import pallas as pl
from jax.experimental.pallas import tpu as pltpu

HEAD_DIM = 128
GMLP_GROUP = 128
CHUNK = 128
DILATED_PATTERNS = ((128, 1), (512, 4), (2048, 16))
ROPE_THETA = 10000.0
EPS = 1e-6
NEG = -1e30

LANES = 128
BF16_SUBLANES = 16
MXU_COLS = 256
V7X_VMEM_BYTES = 64 * 1024 * 1024
VMEM_LIMIT = V7X_VMEM_BYTES * 7 // 8
COMPILER_VMEM_BYTES = 4 * 1024 * 1024
Q_TILE = 128

BF16 = jnp.bfloat16
F32 = jnp.float32


class _Tiles(NamedTuple):
    tm_in: int
    tm_out: int
    tm_ffn: int
    tf: int
    attn_tiles: int


def _plan_tiles(seq, d_model, d_attn, d_gmlp, d_ff, n_dil):
    budget = VMEM_LIMIT - COMPILER_VMEM_BYTES
    f32, bf16 = 4, 2
    d_in = 3 * d_attn + 2 * d_gmlp

    def inproj(tm):
        windows = 2 * (tm * d_model * f32 + 3 * (1 + n_dil) * tm * d_attn * bf16
                       + tm * d_gmlp * bf16 + 2 * tm * HEAD_DIM * f32 + CHUNK * d_gmlp * f32)
        stages = 2 * (3 * tm * d_attn + 2 * tm * d_gmlp) * f32
        scratch = (tm * d_model * bf16 + tm * d_gmlp * (bf16 + f32)
                   + 3 * (n_dil - 1) * tm * d_attn * f32)
        return d_model * d_in * bf16 + windows + stages + scratch

    def outproj(tm):
        windows = 2 * (2 * tm * d_model * f32 + (1 + n_dil) * tm * (d_attn * bf16 + LANES * f32)
                       + tm * d_gmlp * bf16)
        scratch = ((2 * n_dil) * tm * d_attn * f32 + (2 * n_dil - 1) * tm * LANES * f32
                   + 2 * tm * d_attn * bf16)
        return d_model * d_model * bf16 + windows + scratch

    def ffn(tm, tf):
        return 2 * (2 * tm * d_model * f32 + 3 * d_model * tf * bf16) + tm * d_model * bf16

    def attn(tiles):
        rows = tiles * Q_TILE
        return 2 * (4 * rows * d_attn * bf16 + 4 * (Q_TILE // 2) * d_attn * bf16
                    + rows * LANES * f32)

    def largest(candidates, cost):
        return next(c for c in candidates if cost(c) <= budget)

    row_tiles = [t for t in (2048, 1024, 512, 256, 128) if seq % t == 0]
    tf = largest([t for t in (1024, 512, 256, 128) if d_ff % t == 0], lambda t: ffn(128, t))
    return _Tiles(
        tm_in=largest(row_tiles, inproj), tm_out=largest(row_tiles, outproj),
        tm_ffn=largest(row_tiles, lambda t: ffn(t, tf)), tf=tf,
        attn_tiles=largest((32, 16, 8, 4, 2, 1), attn))


def _gelu(x):
    return 0.5 * x * (1.0 + lax.erf(x * (1.0 / math.sqrt(2.0))))


def _rms_scale(x):
    return lax.rsqrt(jnp.mean(x * x, axis=-1, keepdims=True) + EPS)


def _head_cols(hh):
    return slice(hh * HEAD_DIM, (hh + 1) * HEAD_DIM)


def _interleave(major, minor):
    done = 0
    for j, task in enumerate(major):
        task()
        upto = (j + 1) * len(minor) // len(major)
        for other in minor[done:upto]:
            other()
        done = upto


def _inproj_kernel(*refs, d_attn, d_gmlp, q_scale, dilations, n_tiles):
    (x_ref, g1_ref, w_ref, cos_ref, sin_ref, lng_ref, ws_ref, bs_ref, g2_ref) = refs[:9]
    n_res_out = 3 * len(dilations)
    nat_refs = refs[9:12]
    res_refs = refs[12:12 + n_res_out]
    gn_ref = refs[12 + n_res_out]
    scratch = refs[13 + n_res_out:]
    stages = (scratch[0:4], scratch[4:8])
    h_scr, vln_scr, gate_scr = scratch[8:11]
    mids = [scratch[11 + 3 * level:14 + 3 * level] for level in range(len(dilations) - 1)]

    tm = x_ref.shape[0]
    n_heads = d_attn // HEAD_DIM
    n_groups = d_gmlp // GMLP_GROUP
    heads_per_dot = MXU_COLS // HEAD_DIM

    def rope(t):
        return t * cos_ref[...] + pltpu.roll(t, HEAD_DIM // 2, axis=1) * sin_ref[...]

    head_fns = (lambda t: rope(t) * q_scale, rope, lambda t: t)

    def project_tasks(stage):
        slabs, uv_scr = stage[:3], stage[3]

        def norm():
            x = x_ref[...]
            h_scr[...] = (x * _rms_scale(x) * g1_ref[...]).astype(BF16)

        def project(col0):
            return jnp.dot(h_scr[...], w_ref[:, col0:col0 + MXU_COLS], preferred_element_type=F32)

        def uv_task(col):
            uv_scr[:, col:col + MXU_COLS] = project(3 * d_attn + col)

        def qkv_task(idx, first):
            proj = project(idx * d_attn + first * HEAD_DIM)
            for sub in range(heads_per_dot):
                val = head_fns[idx](proj[:, _head_cols(sub)])
                nat_refs[idx][:, _head_cols(first + sub)] = val.astype(BF16)
                slabs[idx][first + sub] = val

        tasks = [norm]
        tasks += [functools.partial(uv_task, col) for col in range(0, 2 * d_gmlp, MXU_COLS)]
        tasks += [functools.partial(qkv_task, idx, first) for idx in range(3)
                  for first in range(0, n_heads, heads_per_dot)]
        return tasks

    def finish_tasks(stage):
        slabs, uv_scr = stage[:3], stage[3]

        def copy_task(idx, level, r):
            d = dilations[level]
            d_prev = dilations[level - 1] if level else 1
            n = tm // d
            for hh in range(n_heads):
                if level:
                    start = (r % d_prev) * (tm // d_prev) + r // d_prev
                    val = mids[level - 1][idx][hh, pl.ds(start, n, stride=d // d_prev), :]
                else:
                    val = slabs[idx][hh, pl.ds(r, n, stride=d), :]
                res_refs[3 * level + idx][:, _head_cols(r * n_heads + hh)] = val.astype(BF16)
                if level + 1 < len(dilations):
                    mids[level][idx][hh, r * n:(r + 1) * n, :] = val

        def layernorm_task(rows):
            vv = _gelu(uv_scr[rows, d_gmlp:2 * d_gmlp])
            vc = vv - jnp.mean(vv, axis=-1, keepdims=True)
            vln = vc * lax.rsqrt(jnp.mean(vc * vc, axis=-1, keepdims=True) + EPS) * lng_ref[...]
            vln_scr[rows, :] = vln.astype(BF16)

        def mix_task(rows, g):
            cols = slice(g * GMLP_GROUP, (g + 1) * GMLP_GROUP)
            mixed = jnp.dot(ws_ref[g], vln_scr[rows, cols], preferred_element_type=F32)
            u = _gelu(uv_scr[rows, cols])
            gate_scr[rows, cols] = u * (mixed + bs_ref[:, cols])

        def gate_norm_task(rows):
            gate = gate_scr[rows, :]
            gn_ref[rows, :] = (gate * _rms_scale(gate) * g2_ref[...]).astype(BF16)

        tasks = []
        for c in range(tm // CHUNK):
            rows = slice(c * CHUNK, (c + 1) * CHUNK)
            tasks.append(functools.partial(layernorm_task, rows))
            tasks += [functools.partial(mix_task, rows, g) for g in range(n_groups)]
            tasks.append(functools.partial(gate_norm_task, rows))
        tasks += [functools.partial(copy_task, idx, level, r) for level, d in enumerate(dilations)
                  for idx in range(3) for r in range(d)]
        return tasks

    i = pl.program_id(0)
    interior = (i > 0) & (i < n_tiles)

    @pl.when(i == 0)
    def _():
        _interleave(project_tasks(stages[0]), [])

    for parity in range(2):
        @pl.when(interior & (i % 2 == parity))
        def _():
            _interleave(project_tasks(stages[parity]), finish_tasks(stages[1 - parity]))

    @pl.when(i == n_tiles)
    def _():
        _interleave(finish_tasks(stages[(n_tiles - 1) % 2]), [])


def _inproj(x2, g1, w_in, cos2, sin2, lng, ws, bsb, g2, *, layer, seq, d_attn, d_gmlp,
            dilations, tm):
    m, d_model = x2.shape
    d_in = w_in.shape[2]
    n_groups = d_gmlp // GMLP_GROUP
    n_heads = d_attn // HEAD_DIM
    pos_blocks = seq // tm
    n_tiles = m // tm
    assert all(tm % (d * BF16_SUBLANES) == 0 for d in dilations)
    assert all(d % d_prev == 0 for d, d_prev in zip(dilations, (1,) + tuple(dilations)))
    vec = lambda i: (layer, 0, 0)
    cur = lambda i: (jnp.minimum(i, n_tiles - 1), 0)
    lag = lambda i: (jnp.maximum(i - 1, 0), 0)
    pos = lambda i: (jnp.minimum(i, n_tiles - 1) % pos_blocks, 0)
    kern = functools.partial(_inproj_kernel, d_attn=d_attn, d_gmlp=d_gmlp,
                             q_scale=HEAD_DIM ** -0.5 * math.log2(math.e), dilations=dilations,
                             n_tiles=n_tiles)
    qkv_specs = [pl.BlockSpec((tm, d_attn), cur)] * 3
    qkv_shapes = [jax.ShapeDtypeStruct((m, d_attn), BF16)] * 3
    for d in dilations:
        qkv_specs += [pl.BlockSpec((tm // d, d * d_attn), lag)] * 3
        qkv_shapes += [jax.ShapeDtypeStruct((m // d, d * d_attn), BF16)] * 3
    stage = [pltpu.VMEM((n_heads, tm, HEAD_DIM), F32)] * 3 + [pltpu.VMEM((tm, 2 * d_gmlp), F32)]
    return pl.pallas_call(
        kern,
        grid=(n_tiles + 1,),
        in_specs=[
            pl.BlockSpec((tm, d_model), cur),
            pl.BlockSpec((None, 1, d_model), vec),
            pl.BlockSpec((None, d_model, d_in), vec, pipeline_mode=pl.Buffered(1)),
            pl.BlockSpec((tm, HEAD_DIM), pos),
            pl.BlockSpec((tm, HEAD_DIM), pos),
            pl.BlockSpec((None, 1, d_gmlp), vec),
            pl.BlockSpec((None, n_groups, CHUNK, CHUNK), lambda i: (layer, 0, 0, 0)),
            pl.BlockSpec((None, CHUNK, d_gmlp), vec),
            pl.BlockSpec((None, 1, d_gmlp), vec),
        ],
        out_specs=qkv_specs + [pl.BlockSpec((tm, d_gmlp), lag)],
        out_shape=qkv_shapes + [jax.ShapeDtypeStruct((m, d_gmlp), BF16)],
        scratch_shapes=stage * 2 + [
            pltpu.VMEM((tm, d_model), BF16),
            pltpu.VMEM((tm, d_gmlp), BF16),
            pltpu.VMEM((tm, d_gmlp), F32),
        ] + [pltpu.VMEM((n_heads, tm, HEAD_DIM), F32)] * (3 * (len(dilations) - 1)),
        compiler_params=pltpu.CompilerParams(
            dimension_semantics=("arbitrary",), vmem_limit_bytes=VMEM_LIMIT),
    )(x2, g1, w_in, cos2, sin2, lng, ws, bsb, g2)


def _attn_kernel(*refs, n_side, n_heads, has_halo):
    if has_halo:
        q_ref, kp_ref, k_ref, kn_ref, vp_ref, v_ref, vn_ref, o_ref, s_ref = refs
    else:
        q_ref, k_ref, v_ref, o_ref, s_ref = refs
        kp_ref = kn_ref = vp_ref = vn_ref = None
    tl = q_ref.shape[1]
    halo = n_side
    n_tiles = tl // Q_TILE
    t = pl.program_id(2)
    first_block = t == 0
    last_block = t == pl.num_programs(2) - 1

    kw = Q_TILE + 2 * halo
    row_i = lax.broadcasted_iota(jnp.int32, (Q_TILE, kw), 0)
    col_j = lax.broadcasted_iota(jnp.int32, (Q_TILE, kw), 1)
    rel = col_j - row_i - halo
    band = (rel >= -n_side) & (rel <= n_side)
    head_ok = (col_j >= halo) | jnp.logical_not(first_block)
    tail_ok = (col_j < halo + Q_TILE) | jnp.logical_not(last_block)
    lane = lax.broadcasted_iota(jnp.int32, (Q_TILE, LANES), 1)
    ones = jnp.ones((kw, HEAD_DIM), BF16)

    def window(prev_ref, main_ref, next_ref, i, cols):
        lo, hi = i * Q_TILE - halo, (i + 1) * Q_TILE + halo
        outside = jnp.zeros((halo, HEAD_DIM), BF16)
        parts = []
        if lo < 0:
            parts.append(prev_ref[0, :, cols] if has_halo else outside)
        parts.append(main_ref[0, max(lo, 0):min(hi, tl), cols])
        if hi > tl:
            parts.append(next_ref[0, :, cols] if has_halo else outside)
        return parts[0] if len(parts) == 1 else jnp.concatenate(parts, axis=0)

    biases = {}
    for i in range(n_tiles):
        mask = band
        if i == 0:
            mask = mask & head_ok
        if i == n_tiles - 1:
            mask = mask & tail_ok
        biases[i] = jnp.where(mask, 0.0, NEG)

    n_res = q_ref.shape[2] // (n_heads * HEAD_DIM)
    for res, i in ((res, i) for res in range(n_res) for i in range(n_tiles)):
        rows = slice(i * Q_TILE, (i + 1) * Q_TILE)
        bias = biases[i]
        stats = jnp.ones((Q_TILE, LANES), F32)
        for hh in range(n_heads):
            cols = _head_cols(res * n_heads + hh)
            kh = window(kp_ref, k_ref, kn_ref, i, cols)
            vh = window(vp_ref, v_ref, vn_ref, i, cols)
            s = lax.dot_general(q_ref[0, rows, cols], kh, (((1,), (1,)), ((), ())),
                                preferred_element_type=F32) + bias
            mx = jnp.max(s, axis=-1, keepdims=True)
            p = jnp.exp2(s - mx).astype(BF16)
            ov = jnp.dot(p, jnp.concatenate([vh, ones], axis=1), preferred_element_type=F32)
            o_ref[0, rows, cols] = ov[:, :HEAD_DIM].astype(o_ref.dtype)
            stats = jnp.where(lane == hh, mx, stats)
            stats = jnp.where(lane == n_heads + hh, ov[:, HEAD_DIM:], stats)
        s_ref[0, rows, res * LANES:(res + 1) * LANES] = stats


def _attn_branch(q, k, v, *, batch, window, dilation, tiles_per_step):
    rows_total, width = q.shape
    c = width // dilation
    sub_len = rows_total // batch
    n_heads = c // HEAD_DIM
    n_side = window // (2 * dilation)
    halo = n_side
    assert halo * 2 == Q_TILE and 2 * n_heads <= LANES and sub_len % Q_TILE == 0
    tl = min(sub_len, tiles_per_step * Q_TILE)
    assert sub_len % tl == 0 and tl % halo == 0
    hb = tl // halo
    n_hb = sub_len // halo
    n_res = min(dilation, max(1, tiles_per_step * Q_TILE // tl))
    assert dilation % n_res == 0

    view = lambda a: a.reshape(batch, sub_len, a.shape[-1])
    qv, kv, vv = view(q), view(k), view(v)

    main = lambda bi, r, t: (bi, t, r)
    prev_halo = lambda bi, r, t: (bi, jnp.maximum(t * hb - 1, 0), r)
    next_halo = lambda bi, r, t: (bi, jnp.minimum((t + 1) * hb, n_hb - 1), r)
    big = pl.BlockSpec((1, tl, n_res * c), main)
    small_p = pl.BlockSpec((1, halo, n_res * c), prev_halo)
    small_n = pl.BlockSpec((1, halo, n_res * c), next_halo)
    stat_spec = pl.BlockSpec((1, tl, n_res * LANES), main)
    stat_shape = jax.ShapeDtypeStruct((batch, sub_len, dilation * LANES), F32)

    has_halo = sub_len > tl
    if has_halo:
        args, in_specs = (qv, kv, kv, kv, vv, vv, vv), [big, small_p, big, small_n] + [
            small_p, big, small_n]
    else:
        args, in_specs = (qv, kv, vv), [big, big, big]
    kern = functools.partial(_attn_kernel, n_side=n_side, n_heads=n_heads, has_halo=has_halo)
    o, stats = pl.pallas_call(
        kern,
        grid=(batch, dilation // n_res, sub_len // tl),
        in_specs=in_specs,
        out_specs=[big, stat_spec],
        out_shape=[jax.ShapeDtypeStruct((batch, sub_len, width), BF16), stat_shape],
        compiler_params=pltpu.CompilerParams(
            dimension_semantics=("arbitrary",) * 3, vmem_limit_bytes=VMEM_LIMIT),
    )(*args)
    flat = lambda a: a.reshape(rows_total, a.shape[-1])
    return flat(o), flat(stats)


def _outproj_kernel(*refs, dilations, n_heads, n_tiles):
    n_pat = 1 + len(dilations)
    n_dil = len(dilations)
    x_ref = refs[0]
    o_refs = refs[1:1 + 2 * n_pat:2]
    s_refs = refs[2:2 + 2 * n_pat:2]
    ga_ref, gn_ref, w_ref, out_ref = refs[1 + 2 * n_pat:5 + 2 * n_pat]
    scratch = refs[5 + 2 * n_pat:]
    o_slabs = scratch[0:n_dil]
    s_slabs = scratch[n_dil:2 * n_dil]
    an_stages = scratch[2 * n_dil:2 * n_dil + 2]
    a_slab = scratch[2 * n_dil + 2]
    o_mids = scratch[2 * n_dil + 3:3 * n_dil + 2]
    s_mids = scratch[3 * n_dil + 2:4 * n_dil + 1]

    tm, d_model = x_ref.shape
    d_attn = n_heads * HEAD_DIM
    state = {}

    def unpermute_task(idx, level, r):
        d_hi = dilations[level]
        d_lo = dilations[level - 1] if level else 1
        n = tm // d_hi
        dst = pl.ds((r % d_lo) * (tm // d_lo) + r // d_lo, n, stride=d_hi // d_lo)
        o_dst, s_dst = (o_mids[level - 1], s_mids[level - 1]) if level else (o_slabs[idx], s_slabs[idx])
        if level == idx:
            s_dst[dst, :] = s_refs[idx + 1][:, r * LANES:(r + 1) * LANES]
            for hh in range(n_heads):
                o_dst[hh, dst, :] = o_refs[idx + 1][:, _head_cols(r * n_heads + hh)].astype(F32)
        else:
            s_dst[dst, :] = s_mids[level][r * n:(r + 1) * n, :]
            for hh in range(n_heads):
                o_dst[hh, dst, :] = o_mids[level][hh, r * n:(r + 1) * n, :]

    def weights_task():
        mxs = [s_refs[0][...]] + [s_slab[...] for s_slab in s_slabs]
        dens = [pltpu.roll(stats, LANES - n_heads, axis=1) for stats in mxs]
        top = functools.reduce(jnp.maximum, mxs)
        es = [jnp.exp2(mx - top) for mx in mxs]
        inv = 1.0 / functools.reduce(jnp.add, [e * den for e, den in zip(es, dens)])
        state["ws"] = [e * inv for e in es]
        state["ssq"] = jnp.zeros((tm, 1), F32)

    def combine_task(hh):
        parts = [o_refs[0][:, _head_cols(hh)].astype(F32)] + [o_slab[hh] for o_slab in o_slabs]
        a = functools.reduce(jnp.add, [w[:, hh:hh + 1] * p for w, p in zip(state["ws"], parts)])
        state["ssq"] = state["ssq"] + jnp.sum(a * a, axis=-1, keepdims=True)
        a_slab[hh] = a

    def norm_task(an_scr, hh):
        if hh == 0:
            state["scale"] = lax.rsqrt(state["ssq"] / d_attn + EPS)
        an_scr[:, _head_cols(hh)] = (
            a_slab[hh] * state["scale"] * ga_ref[:, _head_cols(hh)]).astype(BF16)

    def merge_tasks(an_scr):
        state.clear()
        tasks = [functools.partial(unpermute_task, idx, level, r) for idx in range(n_dil)
                 for level in range(idx, -1, -1) for r in range(dilations[level])]
        tasks.append(weights_task)
        tasks += [functools.partial(combine_task, hh) for hh in range(n_heads)]
        tasks += [functools.partial(norm_task, an_scr, hh) for hh in range(n_heads)]
        return tasks

    def dot_tasks(an_scr):
        def dot_task(col):
            cols = slice(col, col + MXU_COLS)
            acc = jnp.dot(gn_ref[...], w_ref[d_attn:, cols], preferred_element_type=F32)
            acc = acc + jnp.dot(an_scr[...], w_ref[0:d_attn, cols], preferred_element_type=F32)
            out_ref[:, cols] = x_ref[:, cols] + acc

        return [functools.partial(dot_task, col) for col in range(0, d_model, MXU_COLS)]

    i = pl.program_id(0)
    interior = (i > 0) & (i < n_tiles)

    @pl.when(i == 0)
    def _():
        _interleave(merge_tasks(an_stages[0]), [])

    for parity in range(2):
        @pl.when(interior & (i % 2 == parity))
        def _():
            _interleave(dot_tasks(an_stages[1 - parity]), merge_tasks(an_stages[parity]))

    @pl.when(i == n_tiles)
    def _():
        _interleave(dot_tasks(an_stages[(n_tiles - 1) % 2]), [])


def _outproj(x2, branches, ga, gn, w_out, *, layer, dilations, tm):
    m, d_model = x2.shape
    d_attn = branches[0][0].shape[1]
    n_heads = d_attn // HEAD_DIM
    n_tiles = m // tm
    assert all(d % d_prev == 0 for d, d_prev in zip(dilations, (1,) + tuple(dilations)))
    cur = lambda i: (jnp.minimum(i, n_tiles - 1), 0)
    lag = lambda i: (jnp.maximum(i - 1, 0), 0)
    args, in_specs = [x2], [pl.BlockSpec((tm, d_model), lag)]
    for d, branch in zip((1,) + tuple(dilations), branches):
        args += list(branch)
        in_specs += [pl.BlockSpec((tm // d, d * d_attn), cur),
                     pl.BlockSpec((tm // d, d * LANES), cur)]
    args += [ga, gn, w_out]
    in_specs += [
        pl.BlockSpec((None, 1, d_attn), lambda i: (layer, 0, 0)),
        pl.BlockSpec((tm, gn.shape[1]), lag),
        pl.BlockSpec((None,) + w_out.shape[1:], lambda i: (layer, 0, 0),
                     pipeline_mode=pl.Buffered(1)),
    ]
    return pl.pallas_call(
        functools.partial(_outproj_kernel, dilations=dilations, n_heads=n_heads,
                          n_tiles=n_tiles),
        grid=(n_tiles + 1,),
        in_specs=in_specs,
        out_specs=pl.BlockSpec((tm, d_model), lag),
        out_shape=jax.ShapeDtypeStruct((m, d_model), F32),
        scratch_shapes=(
            [pltpu.VMEM((n_heads, tm, HEAD_DIM), F32) for _ in dilations]
            + [pltpu.VMEM((tm, LANES), F32) for _ in dilations]
            + [pltpu.VMEM((tm, d_attn), BF16)] * 2
            + [pltpu.VMEM((n_heads, tm, HEAD_DIM), F32)] * len(dilations)
            + [pltpu.VMEM((tm, LANES), F32)] * (len(dilations) - 1)),
        compiler_params=pltpu.CompilerParams(
            dimension_semantics=("arbitrary",), vmem_limit_bytes=VMEM_LIMIT),
    )(*args)


def _ffn_kernel(*refs, final_norm):
    if final_norm:
        x_ref, g_ref, wg_ref, wu_ref, wd_ref, fg_ref, o_ref, h_scr = refs
    else:
        x_ref, g_ref, wg_ref, wu_ref, wd_ref, o_ref, h_scr = refs
    f = pl.program_id(1)

    @pl.when(f == 0)
    def _():
        x = x_ref[...]
        h_scr[...] = (x * _rms_scale(x) * g_ref[...]).astype(BF16)
        o_ref[...] = x

    h = h_scr[...]
    gate = jnp.dot(h, wg_ref[...], preferred_element_type=F32)
    up = jnp.dot(h, wu_ref[...], preferred_element_type=F32)
    ff = (gate * (1.0 / (1.0 + jnp.exp(-gate))) * up).astype(BF16)
    o_ref[...] += jnp.dot(ff, wd_ref[...], preferred_element_type=F32)

    if final_norm:
        @pl.when(f == pl.num_programs(1) - 1)
        def _():
            y = o_ref[...]
            o_ref[...] = y * _rms_scale(y) * fg_ref[...]


def _ffn(x2, g, w_gate, w_up, w_down, final_g, *, layer, tm, tf):
    m, d_model = x2.shape
    d_ff = w_gate.shape[2]
    final_norm = final_g is not None
    args = [x2, g, w_gate, w_up, w_down]
    in_specs = [
        pl.BlockSpec((tm, d_model), lambda i, f: (i, 0)),
        pl.BlockSpec((None, 1, d_model), lambda i, f: (layer, 0, 0)),
        pl.BlockSpec((None, d_model, tf), lambda i, f: (layer, 0, f)),
        pl.BlockSpec((None, d_model, tf), lambda i, f: (layer, 0, f)),
        pl.BlockSpec((None, tf, d_model), lambda i, f: (layer, f, 0)),
    ]
    if final_norm:
        args.append(final_g)
        in_specs.append(pl.BlockSpec((1, d_model), lambda i, f: (0, 0)))
    return pl.pallas_call(
        functools.partial(_ffn_kernel, final_norm=final_norm),
        grid=(m // tm, d_ff // tf),
        in_specs=in_specs,
        out_specs=pl.BlockSpec((tm, d_model), lambda i, f: (i, 0)),
        out_shape=jax.ShapeDtypeStruct((m, d_model), F32),
        scratch_shapes=[pltpu.VMEM((tm, d_model), BF16)],
        input_output_aliases={0: 0},
        compiler_params=pltpu.CompilerParams(
            dimension_semantics=("arbitrary", "arbitrary"), vmem_limit_bytes=VMEM_LIMIT),
    )(*args)


def _rope_tables(seq):
    pos = jnp.arange(seq, dtype=F32)
    inv = ROPE_THETA ** (-jnp.arange(0, HEAD_DIM, 2, dtype=F32) / HEAD_DIM)
    ang = pos[:, None] * inv[None, :]
    cos, sin = jnp.cos(ang), jnp.sin(ang)
    return jnp.concatenate([cos, cos], axis=-1), jnp.concatenate([-sin, sin], axis=-1)


def kernel(x, norm1_g, w_in, gmlp_ln_g, w_spatial, b_spatial, mix_norm_attn_g,
           mix_norm_gmlp_g, w_out, norm2_g, w_gate, w_up, w_down, final_g):
    b, s, d_model = x.shape
    depth = w_in.shape[0]
    d_gmlp = gmlp_ln_g.shape[1]
    d_attn = mix_norm_attn_g.shape[1]
    n_groups = w_spatial.shape[1]
    d_ff = w_gate.shape[2]
    assert w_in.shape[2] == 3 * d_attn + 2 * d_gmlp and n_groups * GMLP_GROUP == d_gmlp
    assert w_spatial.shape[2] == CHUNK
    assert DILATED_PATTERNS[0][1] == 1
    dilations = tuple(d for _, d in DILATED_PATTERNS[1:])

    tiles = _plan_tiles(s, d_model, d_attn, d_gmlp, d_ff, len(dilations))
    assert s % CHUNK == 0 and tiles.tm_in % CHUNK == 0

    cos2, sin2 = _rope_tables(s)
    x2 = x.reshape(b * s, d_model)
    vec3 = lambda a: a.reshape(a.shape[0], 1, a.shape[1])

    w_in_b, w_out_b = w_in.astype(BF16), w_out.astype(BF16)
    w_gate_b, w_up_b, w_down_b = w_gate.astype(BF16), w_up.astype(BF16), w_down.astype(BF16)
    ws_b = w_spatial.astype(BF16)
    bsb = jnp.repeat(jnp.swapaxes(b_spatial, 1, 2), GMLP_GROUP, axis=2)
    g1, lng, g2, ga, gf = (vec3(a) for a in (norm1_g, gmlp_ln_g, mix_norm_gmlp_g,
                                              mix_norm_attn_g, norm2_g))

    for l in range(depth):
        outs = _inproj(x2, g1, w_in_b, cos2, sin2, lng, ws_b, bsb, g2, layer=l, seq=s,
                       d_attn=d_attn, d_gmlp=d_gmlp, dilations=dilations, tm=tiles.tm_in)
        gn = outs[-1]
        branches = [
            _attn_branch(*outs[3 * i:3 * i + 3], batch=b, window=window, dilation=dilation,
                         tiles_per_step=tiles.attn_tiles)
            for i, (window, dilation) in enumerate(DILATED_PATTERNS)]
        x2 = _outproj(x2, branches, ga, gn, w_out_b, layer=l, dilations=dilations,
                      tm=tiles.tm_out)
        x2 = _ffn(x2, gf, w_gate_b, w_up_b, w_down_b,
                  final_g.reshape(1, -1) if l == depth - 1 else None,
                  layer=l, tm=tiles.tm_ffn, tf=tiles.tf)
    return x2.reshape(b, s, d_model)
```

```python
import functools
import math
from typing import NamedTuple

import jax
import jax.numpy as jnp
from jax import lax
from jax.experimental import pallas as pl
from jax.experimental.pallas import tpu as pltpu

HEAD_DIM = 128
GMLP_GROUP = 128
CHUNK = 128
DILATED_PATTERNS = ((128, 1), (512, 4), (2048, 16))
ROPE_THETA = 10000.0
EPS = 1e-6
NEG = -1e30

LANES = 128
BF16_SUBLANES = 16
MXU_COLS = 256
V7X_VMEM_BYTES = 64 * 1024 * 1024
VMEM_LIMIT = V7X_VMEM_BYTES * 7 // 8
COMPILER_VMEM_BYTES = 4 * 1024 * 1024
Q_TILE = 128

BF16 = jnp.bfloat16
F32 = jnp.float32


class _Tiles(NamedTuple):
    tm_in: int
    tm_out: int
    tm_ffn: int
    tf: int
    attn_tiles: int


def _plan_tiles(seq, d_model, d_attn, d_gmlp, d_ff, n_dil):
    budget = VMEM_LIMIT - COMPILER_VMEM_BYTES
    f32, bf16 = 4, 2
    d_in = 3 * d_attn + 2 * d_gmlp

    def inproj(tm):
        windows = 2 * (tm * d_model * f32 + 3 * (1 + n_dil) * tm * d_attn * bf16
                       + tm * d_gmlp * bf16 + 2 * tm * HEAD_DIM * f32 + CHUNK * d_gmlp * f32)
        stages = 2 * (3 * tm * d_attn + 2 * tm * d_gmlp) * f32
        scratch = (tm * d_model * bf16 + tm * d_gmlp * (bf16 + f32)
                   + 3 * (n_dil - 1) * tm * d_attn * f32)
        return d_model * d_in * bf16 + windows + stages + scratch

    def outproj(tm):
        windows = 2 * (2 * tm * d_model * f32 + (1 + n_dil) * tm * (d_attn * bf16 + LANES * f32)
                       + tm * d_gmlp * bf16)
        scratch = ((2 * n_dil) * tm * d_attn * f32 + (2 * n_dil - 1) * tm * LANES * f32
                   + 2 * tm * d_attn * bf16)
        return d_model * d_model * bf16 + windows + scratch

    def ffn(tm, tf):
        return 2 * (2 * tm * d_model * f32 + 3 * d_model * tf * bf16) + tm * d_model * bf16

    def attn(tiles):
        rows = tiles * Q_TILE
        return 2 * (4 * rows * d_attn * bf16 + 4 * (Q_TILE // 2) * d_attn * bf16
                    + rows * LANES * f32)

    def largest(candidates, cost):
        return next(c for c in candidates if cost(c) <= budget)

    row_tiles = [t for t in (2048, 1024, 512, 256, 128) if seq % t == 0]
    tf = largest([t for t in (1024, 512, 256, 128) if d_ff % t == 0], lambda t: ffn(128, t))
    return _Tiles(
        tm_in=largest(row_tiles, inproj), tm_out=largest(row_tiles, outproj),
        tm_ffn=largest(row_tiles, lambda t: ffn(t, tf)), tf=tf,
        attn_tiles=largest((32, 16, 8, 4, 2, 1), attn))


def _gelu(x):
    return 0.5 * x * (1.0 + lax.erf(x * (1.0 / math.sqrt(2.0))))


def _rms_scale(x):
    return lax.rsqrt(jnp.mean(x * x, axis=-1, keepdims=True) + EPS)


def _head_cols(hh):
    return slice(hh * HEAD_DIM, (hh + 1) * HEAD_DIM)


def _interleave(major, minor):
    done = 0
    for j, task in enumerate(major):
        task()
        upto = (j + 1) * len(minor) // len(major)
        for other in minor[done:upto]:
            other()
        done = upto


def _inproj_kernel(*refs, d_attn, d_gmlp, q_scale, dilations, n_tiles):
    (x_ref, g1_ref, w_ref, cos_ref, sin_ref, lng_ref, ws_ref, bs_ref, g2_ref) = refs[:9]
    n_res_out = 3 * len(dilations)
    nat_refs = refs[9:12]
    res_refs = refs[12:12 + n_res_out]
    gn_ref = refs[12 + n_res_out]
    scratch = refs[13 + n_res_out:]
    stages = (scratch[0:4], scratch[4:8])
    h_scr, vln_scr, gate_scr = scratch[8:11]
    mids = [scratch[11 + 3 * level:14 + 3 * level] for level in range(len(dilations) - 1)]

    tm = x_ref.shape[0]
    n_heads = d_attn // HEAD_DIM
    n_groups = d_gmlp // GMLP_GROUP
    heads_per_dot = MXU_COLS // HEAD_DIM

    def rope(t):
        return t * cos_ref[...] + pltpu.roll(t, HEAD_DIM // 2, axis=1) * sin_ref[...]

    head_fns = (lambda t: rope(t) * q_scale, rope, lambda t: t)

    def project_tasks(stage):
        slabs, uv_scr = stage[:3], stage[3]

        def norm():
            x = x_ref[...]
            h_scr[...] = (x * _rms_scale(x) * g1_ref[...]).astype(BF16)

        def project(col0):
            return jnp.dot(h_scr[...], w_ref[:, col0:col0 + MXU_COLS], preferred_element_type=F32)

        def uv_task(col):
            uv_scr[:, col:col + MXU_COLS] = project(3 * d_attn + col)

        def qkv_task(idx, first):
            proj = project(idx * d_attn + first * HEAD_DIM)
            for sub in range(heads_per_dot):
                val = head_fns[idx](proj[:, _head_cols(sub)])
                nat_refs[idx][:, _head_cols(first + sub)] = val.astype(BF16)
                slabs[idx][first + sub] = val

        tasks = [norm]
        tasks += [functools.partial(uv_task, col) for col in range(0, 2 * d_gmlp, MXU_COLS)]
        tasks += [functools.partial(qkv_task, idx, first) for idx in range(3)
                  for first in range(0, n_heads, heads_per_dot)]
        return tasks

    def finish_tasks(stage):
        slabs, uv_scr = stage[:3], stage[3]

        def copy_task(idx, level, r):
            d = dilations[level]
            d_prev = dilations[level - 1] if level else 1
            n = tm // d
            for hh in range(n_heads):
                if level:
                    start = (r % d_prev) * (tm // d_prev) + r // d_prev
                    val = mids[level - 1][idx][hh, pl.ds(start, n, stride=d // d_prev), :]
                else:
                    val = slabs[idx][hh, pl.ds(r, n, stride=d), :]
                res_refs[3 * level + idx][:, _head_cols(r * n_heads + hh)] = val.astype(BF16)
                if level + 1 < len(dilations):
                    mids[level][idx][hh, r * n:(r + 1) * n, :] = val

        def layernorm_task(rows):
            vv = _gelu(uv_scr[rows, d_gmlp:2 * d_gmlp])
            vc = vv - jnp.mean(vv, axis=-1, keepdims=True)
            vln = vc * lax.rsqrt(jnp.mean(vc * vc, axis=-1, keepdims=True) + EPS) * lng_ref[...]
            vln_scr[rows, :] = vln.astype(BF16)

        def mix_task(rows, g):
            cols = slice(g * GMLP_GROUP, (g + 1) * GMLP_GROUP)
            mixed = jnp.dot(ws_ref[g], vln_scr[rows, cols], preferred_element_type=F32)
            u = _gelu(uv_scr[rows, cols])
            gate_scr[rows, cols] = u * (mixed + bs_ref[:, cols])

        def gate_norm_task(rows):
            gate = gate_scr[rows, :]
            gn_ref[rows, :] = (gate * _rms_scale(gate) * g2_ref[...]).astype(BF16)

        tasks = []
        for c in range(tm // CHUNK):
            rows = slice(c * CHUNK, (c + 1) * CHUNK)
            tasks.append(functools.partial(layernorm_task, rows))
            tasks += [functools.partial(mix_task, rows, g) for g in range(n_groups)]
            tasks.append(functools.partial(gate_norm_task, rows))
        tasks += [functools.partial(copy_task, idx, level, r) for level, d in enumerate(dilations)
                  for idx in range(3) for r in range(d)]
        return tasks

    i = pl.program_id(0)
    interior = (i > 0) & (i < n_tiles)

    @pl.when(i == 0)
    def _():
        _interleave(project_tasks(stages[0]), [])

    for parity in range(2):
        @pl.when(interior & (i % 2 == parity))
        def _():
            _interleave(project_tasks(stages[parity]), finish_tasks(stages[1 - parity]))

    @pl.when(i == n_tiles)
    def _():
        _interleave(finish_tasks(stages[(n_tiles - 1) % 2]), [])


def _inproj(x2, g1, w_in, cos2, sin2, lng, ws, bsb, g2, *, layer, seq, d_attn, d_gmlp,
            dilations, tm):
    m, d_model = x2.shape
    d_in = w_in.shape[2]
    n_groups = d_gmlp // GMLP_GROUP
    n_heads = d_attn // HEAD_DIM
    pos_blocks = seq // tm
    n_tiles = m // tm
    assert all(tm % (d * BF16_SUBLANES) == 0 for d in dilations)
    assert all(d % d_prev == 0 for d, d_prev in zip(dilations, (1,) + tuple(dilations)))
    vec = lambda i: (layer, 0, 0)
    cur = lambda i: (jnp.minimum(i, n_tiles - 1), 0)
    lag = lambda i: (jnp.maximum(i - 1, 0), 0)
    pos = lambda i: (jnp.minimum(i, n_tiles - 1) % pos_blocks, 0)
    kern = functools.partial(_inproj_kernel, d_attn=d_attn, d_gmlp=d_gmlp,
                             q_scale=HEAD_DIM ** -0.5 * math.log2(math.e), dilations=dilations,
                             n_tiles=n_tiles)
    qkv_specs = [pl.BlockSpec((tm, d_attn), cur)] * 3
    qkv_shapes = [jax.ShapeDtypeStruct((m, d_attn), BF16)] * 3
    for d in dilations:
        qkv_specs += [pl.BlockSpec((tm // d, d * d_attn), lag)] * 3
        qkv_shapes += [jax.ShapeDtypeStruct((m // d, d * d_attn), BF16)] * 3
    stage = [pltpu.VMEM((n_heads, tm, HEAD_DIM), F32)] * 3 + [pltpu.VMEM((tm, 2 * d_gmlp), F32)]
    return pl.pallas_call(
        kern,
        grid=(n_tiles + 1,),
        in_specs=[
            pl.BlockSpec((tm, d_model), cur),
            pl.BlockSpec((None, 1, d_model), vec),
            pl.BlockSpec((None, d_model, d_in), vec, pipeline_mode=pl.Buffered(1)),
            pl.BlockSpec((tm, HEAD_DIM), pos),
            pl.BlockSpec((tm, HEAD_DIM), pos),
            pl.BlockSpec((None, 1, d_gmlp), vec),
            pl.BlockSpec((None, n_groups, CHUNK, CHUNK), lambda i: (layer, 0, 0, 0)),
            pl.BlockSpec((None, CHUNK, d_gmlp), vec),
            pl.BlockSpec((None, 1, d_gmlp), vec),
        ],
        out_specs=qkv_specs + [pl.BlockSpec((tm, d_gmlp), lag)],
        out_shape=qkv_shapes + [jax.ShapeDtypeStruct((m, d_gmlp), BF16)],
        scratch_shapes=stage * 2 + [
            pltpu.VMEM((tm, d_model), BF16),
            pltpu.VMEM((tm, d_gmlp), BF16),
            pltpu.VMEM((tm, d_gmlp), F32),
        ] + [pltpu.VMEM((n_heads, tm, HEAD_DIM), F32)] * (3 * (len(dilations) - 1)),
        compiler_params=pltpu.CompilerParams(
            dimension_semantics=("arbitrary",), vmem_limit_bytes=VMEM_LIMIT),
    )(x2, g1, w_in, cos2, sin2, lng, ws, bsb, g2)


def _attn_kernel(*refs, n_side, n_heads, has_halo):
    if has_halo:
        q_ref, kp_ref, k_ref, kn_ref, vp_ref, v_ref, vn_ref, o_ref, s_ref = refs
    else:
        q_ref, k_ref, v_ref, o_ref, s_ref = refs
        kp_ref = kn_ref = vp_ref = vn_ref = None
    tl = q_ref.shape[1]
    halo = n_side
    n_tiles = tl // Q_TILE
    t = pl.program_id(2)
    first_block = t == 0
    last_block = t == pl.num_programs(2) - 1

    kw = Q_TILE + 2 * halo
    row_i = lax.broadcasted_iota(jnp.int32, (Q_TILE, kw), 0)
    col_j = lax.broadcasted_iota(jnp.int32, (Q_TILE, kw), 1)
    rel = col_j - row_i - halo
    band = (rel >= -n_side) & (rel <= n_side)
    head_ok = (col_j >= halo) | jnp.logical_not(first_block)
    tail_ok = (col_j < halo + Q_TILE) | jnp.logical_not(last_block)
    lane = lax.broadcasted_iota(jnp.int32, (Q_TILE, LANES), 1)
    ones = jnp.ones((kw, HEAD_DIM), BF16)

    def window(prev_ref, main_ref, next_ref, i, cols):
        lo, hi = i * Q_TILE - halo, (i + 1) * Q_TILE + halo
        parts = []
        if lo < 0:
            parts.append(prev_ref[0, :, cols] if has_halo else main_ref[0, 0:halo, cols])
        parts.append(main_ref[0, max(lo, 0):min(hi, tl), cols])
        if hi > tl:
            parts.append(next_ref[0, :, cols] if has_halo else main_ref[0, tl - halo:tl, cols])
        return parts[0] if len(parts) == 1 else jnp.concatenate(parts, axis=0)

    biases = {}
    for i in range(n_tiles):
        mask = band
        if i == 0:
            mask = mask & head_ok
        if i == n_tiles - 1:
            mask = mask & tail_ok
        biases[i] = jnp.where(mask, 0.0, NEG)

    n_res = q_ref.shape[2] // (n_heads * HEAD_DIM)
    for res, i in ((res, i) for res in range(n_res) for i in range(n_tiles)):
        rows = slice(i * Q_TILE, (i + 1) * Q_TILE)
        bias = biases[i]
        stats = jnp.ones((Q_TILE, LANES), F32)
        for hh in range(n_heads):
            cols = _head_cols(res * n_heads + hh)
            kh = window(kp_ref, k_ref, kn_ref, i, cols)
            vh = window(vp_ref, v_ref, vn_ref, i, cols)
            s = lax.dot_general(q_ref[0, rows, cols], kh, (((1,), (1,)), ((), ())),
                                preferred_element_type=F32) + bias
            mx = jnp.max(s, axis=-1, keepdims=True)
            p = jnp.exp2(s - mx).astype(BF16)
            ov = jnp.dot(p, jnp.concatenate([vh, ones], axis=1), preferred_element_type=F32)
            o_ref[0, rows, cols] = ov[:, :HEAD_DIM].astype(o_ref.dtype)
            stats = jnp.where(lane == hh, mx, stats)
            stats = jnp.where(lane == n_heads + hh, ov[:, HEAD_DIM:], stats)
        s_ref[0, rows, res * LANES:(res + 1) * LANES] = stats


def _attn_branch(q, k, v, *, batch, window, dilation, tiles_per_step):
    rows_total, width = q.shape
    c = width // dilation
    sub_len = rows_total // batch
    n_heads = c // HEAD_DIM
    n_side = window // (2 * dilation)
    halo = n_side
    assert halo * 2 == Q_TILE and 2 * n_heads <= LANES and sub_len % Q_TILE == 0
    tl = min(sub_len, tiles_per_step * Q_TILE)
    assert sub_len % tl == 0 and tl % halo == 0
    hb = tl // halo
    n_hb = sub_len // halo
    n_res = min(dilation, max(1, tiles_per_step * Q_TILE // tl))
    assert dilation % n_res == 0

    view = lambda a: a.reshape(batch, sub_len, a.shape[-1])
    qv, kv, vv = view(q), view(k), view(v)

    main = lambda bi, r, t: (bi, t, r)
    prev_halo = lambda bi, r, t: (bi, jnp.maximum(t * hb - 1, 0), r)
    next_halo = lambda bi, r, t: (bi, jnp.minimum((t + 1) * hb, n_hb - 1), r)
    big = pl.BlockSpec((1, tl, n_res * c), main)
    small_p = pl.BlockSpec((1, halo, n_res * c), prev_halo)
    small_n = pl.BlockSpec((1, halo, n_res * c), next_halo)
    stat_spec = pl.BlockSpec((1, tl, n_res * LANES), main)
    stat_shape = jax.ShapeDtypeStruct((batch, sub_len, dilation * LANES), F32)

    has_halo = sub_len > tl
    if has_halo:
        args, in_specs = (qv, kv, kv, kv, vv, vv, vv), [big, small_p, big, small_n] + [
            small_p, big, small_n]
    else:
        args, in_specs = (qv, kv, vv), [big, big, big]
    kern = functools.partial(_attn_kernel, n_side=n_side, n_heads=n_heads, has_halo=has_halo)
    o, stats = pl.pallas_call(
        kern,
        grid=(batch, dilation // n_res, sub_len // tl),
        in_specs=in_specs,
        out_specs=[big, stat_spec],
        out_shape=[jax.ShapeDtypeStruct((batch, sub_len, width), BF16), stat_shape],
        compiler_params=pltpu.CompilerParams(
            dimension_semantics=("arbitrary",) * 3, vmem_limit_bytes=VMEM_LIMIT),
    )(*args)
    flat = lambda a: a.reshape(rows_total, a.shape[-1])
    return flat(o), flat(stats)


def _outproj_kernel(*refs, dilations, n_heads, n_tiles):
    n_pat = 1 + len(dilations)
    n_dil = len(dilations)
    x_ref = refs[0]
    o_refs = refs[1:1 + 2 * n_pat:2]
    s_refs = refs[2:2 + 2 * n_pat:2]
    ga_ref, gn_ref, w_ref, out_ref = refs[1 + 2 * n_pat:5 + 2 * n_pat]
    scratch = refs[5 + 2 * n_pat:]
    o_slabs = scratch[0:n_dil]
    s_slabs = scratch[n_dil:2 * n_dil]
    an_stages = scratch[2 * n_dil:2 * n_dil + 2]
    a_slab = scratch[2 * n_dil + 2]
    o_mids = scratch[2 * n_dil + 3:3 * n_dil + 2]
    s_mids = scratch[3 * n_dil + 2:4 * n_dil + 1]

    tm, d_model = x_ref.shape
    d_attn = n_heads * HEAD_DIM
    state = {}

    def unpermute_task(idx, level, r):
        d_hi = dilations[level]
        d_lo = dilations[level - 1] if level else 1
        n = tm // d_hi
        dst = pl.ds((r % d_lo) * (tm // d_lo) + r // d_lo, n, stride=d_hi // d_lo)
        o_dst, s_dst = (o_mids[level - 1], s_mids[level - 1]) if level else (o_slabs[idx], s_slabs[idx])
        if level == idx:
            s_dst[dst, :] = s_refs[idx + 1][:, r * LANES:(r + 1) * LANES]
            for hh in range(n_heads):
                o_dst[hh, dst, :] = o_refs[idx + 1][:, _head_cols(r * n_heads + hh)].astype(F32)
        else:
            s_dst[dst, :] = s_mids[level][r * n:(r + 1) * n, :]
            for hh in range(n_heads):
                o_dst[hh, dst, :] = o_mids[level][hh, r * n:(r + 1) * n, :]

    def weights_task():
        mxs = [s_refs[0][...]] + [s_slab[...] for s_slab in s_slabs]
        dens = [pltpu.roll(stats, LANES - n_heads, axis=1) for stats in mxs]
        top = functools.reduce(jnp.maximum, mxs)
        es = [jnp.exp2(mx - top) for mx in mxs]
        inv = 1.0 / functools.reduce(jnp.add, [e * den for e, den in zip(es, dens)])
        state["ws"] = [e * inv for e in es]
        state["ssq"] = jnp.zeros((tm, 1), F32)

    def combine_task(hh):
        parts = [o_refs[0][:, _head_cols(hh)].astype(F32)] + [o_slab[hh] for o_slab in o_slabs]
        a = functools.reduce(jnp.add, [w[:, hh:hh + 1] * p for w, p in zip(state["ws"], parts)])
        state["ssq"] = state["ssq"] + jnp.sum(a * a, axis=-1, keepdims=True)
        a_slab[hh] = a

    def norm_task(an_scr, hh):
        if hh == 0:
            state["scale"] = lax.rsqrt(state["ssq"] / d_attn + EPS)
        an_scr[:, _head_cols(hh)] = (
            a_slab[hh] * state["scale"] * ga_ref[:, _head_cols(hh)]).astype(BF16)

    def merge_tasks(an_scr):
        state.clear()
        tasks = [functools.partial(unpermute_task, idx, level, r) for idx in range(n_dil)
                 for level in range(idx, -1, -1) for r in range(dilations[level])]
        tasks.append(weights_task)
        tasks += [functools.partial(combine_task, hh) for hh in range(n_heads)]
        tasks += [functools.partial(norm_task, an_scr, hh) for hh in range(n_heads)]
        return tasks

    def dot_tasks(an_scr):
        def dot_task(col):
            cols = slice(col, col + MXU_COLS)
            acc = jnp.dot(gn_ref[...], w_ref[d_attn:, cols], preferred_element_type=F32)
            acc = acc + jnp.dot(an_scr[...], w_ref[0:d_attn, cols], preferred_element_type=F32)
            out_ref[:, cols] = x_ref[:, cols] + acc

        return [functools.partial(dot_task, col) for col in range(0, d_model, MXU_COLS)]

    i = pl.program_id(0)
    interior = (i > 0) & (i < n_tiles)

    @pl.when(i == 0)
    def _():
        _interleave(merge_tasks(an_stages[0]), [])

    for parity in range(2):
        @pl.when(interior & (i % 2 == parity))
        def _():
            _interleave(dot_tasks(an_stages[1 - parity]), merge_tasks(an_stages[parity]))

    @pl.when(i == n_tiles)
    def _():
        _interleave(dot_tasks(an_stages[(n_tiles - 1) % 2]), [])


def _outproj(x2, branches, ga, gn, w_out, *, layer, dilations, tm):
    m, d_model = x2.shape
    d_attn = branches[0][0].shape[1]
    n_heads = d_attn // HEAD_DIM
    n_tiles = m // tm
    assert all(d % d_prev == 0 for d, d_prev in zip(dilations, (1,) + tuple(dilations)))
    cur = lambda i: (jnp.minimum(i, n_tiles - 1), 0)
    lag = lambda i: (jnp.maximum(i - 1, 0), 0)
    args, in_specs = [x2], [pl.BlockSpec((tm, d_model), lag)]
    for d, branch in zip((1,) + tuple(dilations), branches):
        args += list(branch)
        in_specs += [pl.BlockSpec((tm // d, d * d_attn), cur),
                     pl.BlockSpec((tm // d, d * LANES), cur)]
    args += [ga, gn, w_out]
    in_specs += [
        pl.BlockSpec((None, 1, d_attn), lambda i: (layer, 0, 0)),
        pl.BlockSpec((tm, gn.shape[1]), lag),
        pl.BlockSpec((None,) + w_out.shape[1:], lambda i: (layer, 0, 0),
                     pipeline_mode=pl.Buffered(1)),
    ]
    return pl.pallas_call(
        functools.partial(_outproj_kernel, dilations=dilations, n_heads=n_heads,
                          n_tiles=n_tiles),
        grid=(n_tiles + 1,),
        in_specs=in_specs,
        out_specs=pl.BlockSpec((tm, d_model), lag),
        out_shape=jax.ShapeDtypeStruct((m, d_model), F32),
        scratch_shapes=(
            [pltpu.VMEM((n_heads, tm, HEAD_DIM), F32) for _ in dilations]
            + [pltpu.VMEM((tm, LANES), F32) for _ in dilations]
            + [pltpu.VMEM((tm, d_attn), BF16)] * 2
            + [pltpu.VMEM((n_heads, tm, HEAD_DIM), F32)] * len(dilations)
            + [pltpu.VMEM((tm, LANES), F32)] * (len(dilations) - 1)),
        compiler_params=pltpu.CompilerParams(
            dimension_semantics=("arbitrary",), vmem_limit_bytes=VMEM_LIMIT),
    )(*args)


def _ffn_kernel(*refs, final_norm):
    if final_norm:
        x_ref, g_ref, wg_ref, wu_ref, wd_ref, fg_ref, o_ref, h_scr = refs
    else:
        x_ref, g_ref, wg_ref, wu_ref, wd_ref, o_ref, h_scr = refs
    f = pl.program_id(1)

    @pl.when(f == 0)
    def _():
        x = x_ref[...]
        h_scr[...] = (x * _rms_scale(x) * g_ref[...]).astype(BF16)
        o_ref[...] = x

    h = h_scr[...]
    gate = jnp.dot(h, wg_ref[...], preferred_element_type=F32)
    up = jnp.dot(h, wu_ref[...], preferred_element_type=F32)
    ff = (gate * (1.0 / (1.0 + jnp.exp(-gate))) * up).astype(BF16)
    o_ref[...] += jnp.dot(ff, wd_ref[...], preferred_element_type=F32)

    if final_norm:
        @pl.when(f == pl.num_programs(1) - 1)
        def _():
            y = o_ref[...]
            o_ref[...] = y * _rms_scale(y) * fg_ref[...]


def _ffn(x2, g, w_gate, w_up, w_down, final_g, *, layer, tm, tf):
    m, d_model = x2.shape
    d_ff = w_gate.shape[2]
    final_norm = final_g is not None
    args = [x2, g, w_gate, w_up, w_down]
    in_specs = [
        pl.BlockSpec((tm, d_model), lambda i, f: (i, 0)),
        pl.BlockSpec((None, 1, d_model), lambda i, f: (layer, 0, 0)),
        pl.BlockSpec((None, d_model, tf), lambda i, f: (layer, 0, f)),
        pl.BlockSpec((None, d_model, tf), lambda i, f: (layer, 0, f)),
        pl.BlockSpec((None, tf, d_model), lambda i, f: (layer, f, 0)),
    ]
    if final_norm:
        args.append(final_g)
        in_specs.append(pl.BlockSpec((1, d_model), lambda i, f: (0, 0)))
    return pl.pallas_call(
        functools.partial(_ffn_kernel, final_norm=final_norm),
        grid=(m // tm, d_ff // tf),
        in_specs=in_specs,
        out_specs=pl.BlockSpec((tm, d_model), lambda i, f: (i, 0)),
        out_shape=jax.ShapeDtypeStruct((m, d_model), F32),
        scratch_shapes=[pltpu.VMEM((tm, d_model), BF16)],
        input_output_aliases={0: 0},
        compiler_params=pltpu.CompilerParams(
            dimension_semantics=("arbitrary", "arbitrary"), vmem_limit_bytes=VMEM_LIMIT),
    )(*args)


def _rope_tables(seq):
    pos = jnp.arange(seq, dtype=F32)
    inv = ROPE_THETA ** (-jnp.arange(0, HEAD_DIM, 2, dtype=F32) / HEAD_DIM)
    ang = pos[:, None] * inv[None, :]
    cos, sin = jnp.cos(ang), jnp.sin(ang)
    return jnp.concatenate([cos, cos], axis=-1), jnp.concatenate([-sin, sin], axis=-1)


def kernel(x, norm1_g, w_in, gmlp_ln_g, w_spatial, b_spatial, mix_norm_attn_g,
           mix_norm_gmlp_g, w_out, norm2_g, w_gate, w_up, w_down, final_g):
    b, s, d_model = x.shape
    depth = w_in.shape[0]
    d_gmlp = gmlp_ln_g.shape[1]
    d_attn = mix_norm_attn_g.shape[1]
    n_groups = w_spatial.shape[1]
    d_ff = w_gate.shape[2]
    assert w_in.shape[2] == 3 * d_attn + 2 * d_gmlp and n_groups * GMLP_GROUP == d_gmlp
    assert w_spatial.shape[2] == CHUNK
    assert DILATED_PATTERNS[0][1] == 1
    dilations = tuple(d for _, d in DILATED_PATTERNS[1:])

    tiles = _plan_tiles(s, d_model, d_attn, d_gmlp, d_ff, len(dilations))
    assert s % CHUNK == 0 and tiles.tm_in % CHUNK == 0

    cos2, sin2 = _rope_tables(s)
    x2 = x.reshape(b * s, d_model)
    vec3 = lambda a: a.reshape(a.shape[0], 1, a.shape[1])

    w_in_b, w_out_b = w_in.astype(BF16), w_out.astype(BF16)
    w_gate_b, w_up_b, w_down_b = w_gate.astype(BF16), w_up.astype(BF16), w_down.astype(BF16)
    ws_b = w_spatial.astype(BF16)
    bsb = jnp.repeat(jnp.swapaxes(b_spatial, 1, 2), GMLP_GROUP, axis=2)
    g1, lng, g2, ga, gf = (vec3(a) for a in (norm1_g, gmlp_ln_g, mix_norm_gmlp_g,
                                              mix_norm_attn_g, norm2_g))

    for l in range(depth):
        outs = _inproj(x2, g1, w_in_b, cos2, sin2, lng, ws_b, bsb, g2, layer=l, seq=s,
                       d_attn=d_attn, d_gmlp=d_gmlp, dilations=dilations, tm=tiles.tm_in)
        gn = outs[-1]
        branches = [
            _attn_branch(*outs[3 * i:3 * i + 3], batch=b, window=window, dilation=dilation,
                         tiles_per_step=tiles.attn_tiles)
            for i, (window, dilation) in enumerate(DILATED_PATTERNS)]
        x2 = _outproj(x2, branches, ga, gn, w_out_b, layer=l, dilations=dilations,
                      tm=tiles.tm_out)
        x2 = _ffn(x2, gf, w_gate_b, w_up_b, w_down_b,
                  final_g.reshape(1, -1) if l == depth - 1 else None,
                  layer=l, tm=tiles.tm_ffn, tf=tiles.tf)
    return x2.reshape(b, s, d_model)
```

```python
import functools
import math
from typing import NamedTuple

import jax
import jax.numpy as jnp
from jax import lax
from jax.experimental import pallas as pl
from jax.experimental.pallas import tpu as pltpu

HEAD_DIM = 128
GMLP_GROUP = 128
CHUNK = 128
DILATED_PATTERNS = ((128, 1), (512, 4), (2048, 16))
ROPE_THETA = 10000.0
EPS = 1e-6
NEG = -1e30

LANES = 128
BF16_SUBLANES = 16
MXU_COLS = 256
V7X_VMEM_BYTES = 64 * 1024 * 1024
VMEM_LIMIT = V7X_VMEM_BYTES * 7 // 8
COMPILER_VMEM_BYTES = 4 * 1024 * 1024
Q_TILE = 128

BF16 = jnp.bfloat16
F32 = jnp.float32


class _Tiles(NamedTuple):
    tm_in: int
    tm_out: int
    tm_ffn: int
    tf: int
    attn_tiles: int


def _plan_tiles(seq, d_model, d_attn, d_gmlp, d_ff, n_dil):
    budget = VMEM_LIMIT - COMPILER_VMEM_BYTES
    f32, bf16 = 4, 2
    d_in = 3 * d_attn + 2 * d_gmlp

    def inproj(tm):
        windows = 2 * (tm * d_model * f32 + 3 * (1 + n_dil) * tm * d_attn * bf16
                       + tm * d_gmlp * bf16 + 2 * tm * HEAD_DIM * f32 + CHUNK * d_gmlp * f32)
        stages = 2 * (3 * tm * d_attn + 2 * tm * d_gmlp) * f32
        scratch = (tm * d_model * bf16 + tm * d_gmlp * (bf16 + f32)
                   + 3 * (n_dil - 1) * tm * d_attn * f32)
        return d_model * d_in * bf16 + windows + stages + scratch

    def outproj(tm):
        windows = 2 * (2 * tm * d_model * f32 + (1 + n_dil) * tm * (d_attn * bf16 + LANES * f32)
                       + tm * d_gmlp * bf16)
        scratch = ((2 * n_dil) * tm * d_attn * f32 + (2 * n_dil - 1) * tm * LANES * f32
                   + 2 * tm * d_attn * bf16)
        return d_model * d_model * bf16 + windows + scratch

    def ffn(tm, tf):
        return 2 * (2 * tm * d_model * f32 + 3 * d_model * tf * bf16) + tm * d_model * bf16

    def attn(tiles):
        rows = tiles * Q_TILE
        return 2 * (4 * rows * d_attn * bf16 + 4 * (Q_TILE // 2) * d_attn * bf16
                    + rows * LANES * f32)

    def largest(candidates, cost):
        return next(c for c in candidates if cost(c) <= budget)

    row_tiles = [t for t in (2048, 1024, 512, 256, 128) if seq % t == 0]
    tf = largest([t for t in (1024, 512, 256, 128) if d_ff % t == 0], lambda t: ffn(128, t))
    return _Tiles(
        tm_in=largest(row_tiles, inproj), tm_out=largest(row_tiles, outproj),
        tm_ffn=largest(row_tiles, lambda t: ffn(t, tf)), tf=tf,
        attn_tiles=largest((32, 16, 8, 4, 2, 1), attn))


def _gelu(x):
    return 0.5 * x * (1.0 + lax.erf(x * (1.0 / math.sqrt(2.0))))


def _rms_scale(x):
    return lax.rsqrt(jnp.mean(x * x, axis=-1, keepdims=True) + EPS)


def _head_cols(hh):
    return slice(hh * HEAD_DIM, (hh + 1) * HEAD_DIM)


def _interleave(major, minor):
    done = 0
    for j, task in enumerate(major):
        task()
        upto = (j + 1) * len(minor) // len(major)
        for other in minor[done:upto]:
            other()
        done = upto


def _inproj_kernel(*refs, d_attn, d_gmlp, q_scale, dilations, n_tiles):
    (x_ref, g1_ref, w_ref, cos_ref, sin_ref, lng_ref, ws_ref, bs_ref, g2_ref) = refs[:9]
    n_res_out = 3 * len(dilations)
    nat_refs = refs[9:12]
    res_refs = refs[12:12 + n_res_out]
    gn_ref = refs[12 + n_res_out]
    scratch = refs[13 + n_res_out:]
    stages = (scratch[0:4], scratch[4:8])
    h_scr, vln_scr, gate_scr = scratch[8:11]
    mids = [scratch[11 + 3 * level:14 + 3 * level] for level in range(len(dilations) - 1)]

    tm = x_ref.shape[0]
    n_heads = d_attn // HEAD_DIM
    n_groups = d_gmlp // GMLP_GROUP
    heads_per_dot = MXU_COLS // HEAD_DIM

    def rope(t):
        return t * cos_ref[...] + pltpu.roll(t, HEAD_DIM // 2, axis=1) * sin_ref[...]

    head_fns = (lambda t: rope(t) * q_scale, rope, lambda t: t)

    def project_tasks(stage):
        slabs, uv_scr = stage[:3], stage[3]

        def norm():
            x = x_ref[...]
            h_scr[...] = (x * _rms_scale(x) * g1_ref[...]).astype(BF16)

        def project(col0):
            return jnp.dot(h_scr[...], w_ref[:, col0:col0 + MXU_COLS], preferred_element_type=F32)

        def uv_task(col):
            uv_scr[:, col:col + MXU_COLS] = project(3 * d_attn + col)

        def qkv_task(idx, first):
            proj = project(idx * d_attn + first * HEAD_DIM)
            for sub in range(heads_per_dot):
                val = head_fns[idx](proj[:, _head_cols(sub)])
                nat_refs[idx][:, _head_cols(first + sub)] = val.astype(BF16)
                slabs[idx][first + sub] = val

        tasks = [norm]
        tasks += [functools.partial(uv_task, col) for col in range(0, 2 * d_gmlp, MXU_COLS)]
        tasks += [functools.partial(qkv_task, idx, first) for idx in range(3)
                  for first in range(0, n_heads, heads_per_dot)]
        return tasks

    def finish_tasks(stage):
        slabs, uv_scr = stage[:3], stage[3]

        def copy_task(idx, level, r):
            d = dilations[level]
            d_prev = dilations[level - 1] if level else 1
            n = tm // d
            for hh in range(n_heads):
                if level:
                    start = (r % d_prev) * (tm // d_prev) + r // d_prev
                    val = mids[level - 1][idx][hh, pl.ds(start, n, stride=d // d_prev), :]
                else:
                    val = slabs[idx][hh, pl.ds(r, n, stride=d), :]
                res_refs[3 * level + idx][:, _head_cols(r * n_heads + hh)] = val.astype(BF16)
                if level + 1 < len(dilations):
                    mids[level][idx][hh, r * n:(r + 1) * n, :] = val

        def layernorm_task(rows):
            vv = _gelu(uv_scr[rows, d_gmlp:2 * d_gmlp])
            vc = vv - jnp.mean(vv, axis=-1, keepdims=True)
            vln = vc * lax.rsqrt(jnp.mean(vc * vc, axis=-1, keepdims=True) + EPS) * lng_ref[...]
            vln_scr[rows, :] = vln.astype(BF16)

        def mix_task(rows, g):
            cols = slice(g * GMLP_GROUP, (g + 1) * GMLP_GROUP)
            mixed = jnp.dot(ws_ref[g], vln_scr[rows, cols], preferred_element_type=F32)
            u = _gelu(uv_scr[rows, cols])
            gate_scr[rows, cols] = u * (mixed + bs_ref[:, cols])

        def gate_norm_task(rows):
            gate = gate_scr[rows, :]
            gn_ref[rows, :] = (gate * _rms_scale(gate) * g2_ref[...]).astype(BF16)

        tasks = []
        for c in range(tm // CHUNK):
            rows = slice(c * CHUNK, (c + 1) * CHUNK)
            halves = [slice(r0, r0 + CHUNK // 2) for r0 in (c * CHUNK, c * CHUNK + CHUNK // 2)]
            tasks += [functools.partial(layernorm_task, half) for half in halves]
            tasks += [functools.partial(mix_task, rows, g) for g in range(n_groups)]
            tasks += [functools.partial(gate_norm_task, half) for half in halves]
        tasks += [functools.partial(copy_task, idx, level, r) for level, d in enumerate(dilations)
                  for idx in range(3) for r in range(d)]
        return tasks

    i = pl.program_id(0)
    interior = (i > 0) & (i < n_tiles)

    @pl.when(i == 0)
    def _():
        _interleave(project_tasks(stages[0]), [])

    for parity in range(2):
        @pl.when(interior & (i % 2 == parity))
        def _():
            _interleave(project_tasks(stages[parity]), finish_tasks(stages[1 - parity]))

    @pl.when(i == n_tiles)
    def _():
        _interleave(finish_tasks(stages[(n_tiles - 1) % 2]), [])


def _inproj(x2, g1, w_in, cos2, sin2, lng, ws, bsb, g2, *, layer, seq, d_attn, d_gmlp,
            dilations, tm):
    m, d_model = x2.shape
    d_in = w_in.shape[2]
    n_groups = d_gmlp // GMLP_GROUP
    n_heads = d_attn // HEAD_DIM
    pos_blocks = seq // tm
    n_tiles = m // tm
    assert all(tm % (d * BF16_SUBLANES) == 0 for d in dilations)
    assert all(d % d_prev == 0 for d, d_prev in zip(dilations, (1,) + tuple(dilations)))
    vec = lambda i: (layer, 0, 0)
    cur = lambda i: (jnp.minimum(i, n_tiles - 1), 0)
    lag = lambda i: (jnp.maximum(i - 1, 0), 0)
    pos = lambda i: (jnp.minimum(i, n_tiles - 1) % pos_blocks, 0)
    kern = functools.partial(_inproj_kernel, d_attn=d_attn, d_gmlp=d_gmlp,
                             q_scale=HEAD_DIM ** -0.5 * math.log2(math.e), dilations=dilations,
                             n_tiles=n_tiles)
    qkv_specs = [pl.BlockSpec((tm, d_attn), cur)] * 3
    qkv_shapes = [jax.ShapeDtypeStruct((m, d_attn), BF16)] * 3
    for d in dilations:
        qkv_specs += [pl.BlockSpec((tm // d, d * d_attn), lag)] * 3
        qkv_shapes += [jax.ShapeDtypeStruct((m // d, d * d_attn), BF16)] * 3
    stage = [pltpu.VMEM((n_heads, tm, HEAD_DIM), F32)] * 3 + [pltpu.VMEM((tm, 2 * d_gmlp), F32)]
    return pl.pallas_call(
        kern,
        grid=(n_tiles + 1,),
        in_specs=[
            pl.BlockSpec((tm, d_model), cur),
            pl.BlockSpec((None, 1, d_model), vec),
            pl.BlockSpec((None, d_model, d_in), vec, pipeline_mode=pl.Buffered(1)),
            pl.BlockSpec((tm, HEAD_DIM), pos),
            pl.BlockSpec((tm, HEAD_DIM), pos),
            pl.BlockSpec((None, 1, d_gmlp), vec),
            pl.BlockSpec((None, n_groups, CHUNK, CHUNK), lambda i: (layer, 0, 0, 0)),
            pl.BlockSpec((None, CHUNK, d_gmlp), vec),
            pl.BlockSpec((None, 1, d_gmlp), vec),
        ],
        out_specs=qkv_specs + [pl.BlockSpec((tm, d_gmlp), lag)],
        out_shape=qkv_shapes + [jax.ShapeDtypeStruct((m, d_gmlp), BF16)],
        scratch_shapes=stage * 2 + [
            pltpu.VMEM((tm, d_model), BF16),
            pltpu.VMEM((tm, d_gmlp), BF16),
            pltpu.VMEM((tm, d_gmlp), F32),
        ] + [pltpu.VMEM((n_heads, tm, HEAD_DIM), F32)] * (3 * (len(dilations) - 1)),
        compiler_params=pltpu.CompilerParams(
            dimension_semantics=("arbitrary",), vmem_limit_bytes=VMEM_LIMIT),
    )(x2, g1, w_in, cos2, sin2, lng, ws, bsb, g2)


def _attn_kernel(*refs, n_side, n_heads, has_halo):
    if has_halo:
        q_ref, kp_ref, k_ref, kn_ref, vp_ref, v_ref, vn_ref, o_ref, s_ref = refs
    else:
        q_ref, k_ref, v_ref, o_ref, s_ref = refs
        kp_ref = kn_ref = vp_ref = vn_ref = None
    tl = q_ref.shape[1]
    halo = n_side
    n_tiles = tl // Q_TILE
    t = pl.program_id(2)
    first_block = t == 0
    last_block = t == pl.num_programs(2) - 1

    kw = Q_TILE + 2 * halo
    row_i = lax.broadcasted_iota(jnp.int32, (Q_TILE, kw), 0)
    col_j = lax.broadcasted_iota(jnp.int32, (Q_TILE, kw), 1)
    rel = col_j - row_i - halo
    band = (rel >= -n_side) & (rel <= n_side)
    head_ok = (col_j >= halo) | jnp.logical_not(first_block)
    tail_ok = (col_j < halo + Q_TILE) | jnp.logical_not(last_block)
    lane = lax.broadcasted_iota(jnp.int32, (Q_TILE, LANES), 1)
    ones = jnp.ones((kw, HEAD_DIM), BF16)

    def window(prev_ref, main_ref, next_ref, i, cols):
        lo, hi = i * Q_TILE - halo, (i + 1) * Q_TILE + halo
        parts = []
        if lo < 0:
            parts.append(prev_ref[0, :, cols] if has_halo else main_ref[0, 0:halo, cols])
        parts.append(main_ref[0, max(lo, 0):min(hi, tl), cols])
        if hi > tl:
            parts.append(next_ref[0, :, cols] if has_halo else main_ref[0, tl - halo:tl, cols])
        return parts[0] if len(parts) == 1 else jnp.concatenate(parts, axis=0)

    biases = {}
    for i in range(n_tiles):
        mask = band
        if i == 0:
            mask = mask & head_ok
        if i == n_tiles - 1:
            mask = mask & tail_ok
        biases[i] = jnp.where(mask, 0.0, NEG)

    n_res = q_ref.shape[2] // (n_heads * HEAD_DIM)
    for res, i in ((res, i) for res in range(n_res) for i in range(n_tiles)):
        rows = slice(i * Q_TILE, (i + 1) * Q_TILE)
        bias = biases[i]
        stats = jnp.ones((Q_TILE, LANES), F32)
        for hh in range(n_heads):
            cols = _head_cols(res * n_heads + hh)
            kh = window(kp_ref, k_ref, kn_ref, i, cols)
            vh = window(vp_ref, v_ref, vn_ref, i, cols)
            s = lax.dot_general(q_ref[0, rows, cols], kh, (((1,), (1,)), ((), ())),
                                preferred_element_type=F32) + bias
            mx = jnp.max(s, axis=-1, keepdims=True)
            p = jnp.exp2(s - mx).astype(BF16)
            ov = jnp.dot(p, jnp.concatenate([vh, ones], axis=1), preferred_element_type=F32)
            o_ref[0, rows, cols] = ov[:, :HEAD_DIM].astype(o_ref.dtype)
            stats = jnp.where(lane == hh, mx, stats)
            stats = jnp.where(lane == n_heads + hh, ov[:, HEAD_DIM:], stats)
        s_ref[0, rows, res * LANES:(res + 1) * LANES] = stats


def _attn_branch(q, k, v, *, batch, window, dilation, tiles_per_step):
    rows_total, width = q.shape
    c = width // dilation
    sub_len = rows_total // batch
    n_heads = c // HEAD_DIM
    n_side = window // (2 * dilation)
    halo = n_side
    assert halo * 2 == Q_TILE and 2 * n_heads <= LANES and sub_len % Q_TILE == 0
    tl = min(sub_len, tiles_per_step * Q_TILE)
    assert sub_len % tl == 0 and tl % halo == 0
    hb = tl // halo
    n_hb = sub_len // halo
    n_res = min(dilation, max(1, tiles_per_step * Q_TILE // tl))
    assert dilation % n_res == 0

    view = lambda a: a.reshape(batch, sub_len, a.shape[-1])
    qv, kv, vv = view(q), view(k), view(v)

    main = lambda bi, r, t: (bi, t, r)
    prev_halo = lambda bi, r, t: (bi, jnp.maximum(t * hb - 1, 0), r)
    next_halo = lambda bi, r, t: (bi, jnp.minimum((t + 1) * hb, n_hb - 1), r)
    big = pl.BlockSpec((1, tl, n_res * c), main)
    small_p = pl.BlockSpec((1, halo, n_res * c), prev_halo)
    small_n = pl.BlockSpec((1, halo, n_res * c), next_halo)
    stat_spec = pl.BlockSpec((1, tl, n_res * LANES), main)
    stat_shape = jax.ShapeDtypeStruct((batch, sub_len, dilation * LANES), F32)

    has_halo = sub_len > tl
    if has_halo:
        args, in_specs = (qv, kv, kv, kv, vv, vv, vv), [big, small_p, big, small_n] + [
            small_p, big, small_n]
    else:
        args, in_specs = (qv, kv, vv), [big, big, big]
    kern = functools.partial(_attn_kernel, n_side=n_side, n_heads=n_heads, has_halo=has_halo)
    o, stats = pl.pallas_call(
        kern,
        grid=(batch, dilation // n_res, sub_len // tl),
        in_specs=in_specs,
        out_specs=[big, stat_spec],
        out_shape=[jax.ShapeDtypeStruct((batch, sub_len, width), BF16), stat_shape],
        compiler_params=pltpu.CompilerParams(
            dimension_semantics=("arbitrary",) * 3, vmem_limit_bytes=VMEM_LIMIT),
    )(*args)
    flat = lambda a: a.reshape(rows_total, a.shape[-1])
    return flat(o), flat(stats)


def _outproj_kernel(*refs, dilations, n_heads, n_tiles):
    n_pat = 1 + len(dilations)
    n_dil = len(dilations)
    x_ref = refs[0]
    o_refs = refs[1:1 + 2 * n_pat:2]
    s_refs = refs[2:2 + 2 * n_pat:2]
    ga_ref, gn_ref, w_ref, out_ref = refs[1 + 2 * n_pat:5 + 2 * n_pat]
    scratch = refs[5 + 2 * n_pat:]
    o_slabs = scratch[0:n_dil]
    s_slabs = scratch[n_dil:2 * n_dil]
    an_stages = scratch[2 * n_dil:2 * n_dil + 2]
    a_slab = scratch[2 * n_dil + 2]
    o_mids = scratch[2 * n_dil + 3:3 * n_dil + 2]
    s_mids = scratch[3 * n_dil + 2:4 * n_dil + 1]

    tm, d_model = x_ref.shape
    d_attn = n_heads * HEAD_DIM
    state = {}

    def unpermute_task(idx, level, r):
        d_hi = dilations[level]
        d_lo = dilations[level - 1] if level else 1
        n = tm // d_hi
        dst = pl.ds((r % d_lo) * (tm // d_lo) + r // d_lo, n, stride=d_hi // d_lo)
        o_dst, s_dst = (o_mids[level - 1], s_mids[level - 1]) if level else (o_slabs[idx], s_slabs[idx])
        if level == idx:
            s_dst[dst, :] = s_refs[idx + 1][:, r * LANES:(r + 1) * LANES]
            for hh in range(n_heads):
                o_dst[hh, dst, :] = o_refs[idx + 1][:, _head_cols(r * n_heads + hh)].astype(F32)
        else:
            s_dst[dst, :] = s_mids[level][r * n:(r + 1) * n, :]
            for hh in range(n_heads):
                o_dst[hh, dst, :] = o_mids[level][hh, r * n:(r + 1) * n, :]

    def weights_task():
        mxs = [s_refs[0][...]] + [s_slab[...] for s_slab in s_slabs]
        dens = [pltpu.roll(stats, LANES - n_heads, axis=1) for stats in mxs]
        top = functools.reduce(jnp.maximum, mxs)
        es = [jnp.exp2(mx - top) for mx in mxs]
        inv = 1.0 / functools.reduce(jnp.add, [e * den for e, den in zip(es, dens)])
        state["ws"] = [e * inv for e in es]
        state["ssq"] = jnp.zeros((tm, 1), F32)

    def combine_task(hh):
        parts = [o_refs[0][:, _head_cols(hh)].astype(F32)] + [o_slab[hh] for o_slab in o_slabs]
        a = functools.reduce(jnp.add, [w[:, hh:hh + 1] * p for w, p in zip(state["ws"], parts)])
        state["ssq"] = state["ssq"] + jnp.sum(a * a, axis=-1, keepdims=True)
        a_slab[hh] = a

    def norm_task(an_scr, hh):
        if hh == 0:
            state["scale"] = lax.rsqrt(state["ssq"] / d_attn + EPS)
        an_scr[:, _head_cols(hh)] = (
            a_slab[hh] * state["scale"] * ga_ref[:, _head_cols(hh)]).astype(BF16)

    def merge_tasks(an_scr):
        state.clear()
        tasks = [functools.partial(unpermute_task, idx, level, r) for idx in range(n_dil)
                 for level in range(idx, -1, -1) for r in range(dilations[level])]
        tasks.append(weights_task)
        tasks += [functools.partial(combine_task, hh) for hh in range(n_heads)]
        tasks += [functools.partial(norm_task, an_scr, hh) for hh in range(n_heads)]
        return tasks

    def dot_tasks(an_scr):
        def dot_task(col):
            cols = slice(col, col + MXU_COLS)
            acc = jnp.dot(gn_ref[...], w_ref[d_attn:, cols], preferred_element_type=F32)
            acc = acc + jnp.dot(an_scr[...], w_ref[0:d_attn, cols], preferred_element_type=F32)
            out_ref[:, cols] = x_ref[:, cols] + acc

        return [functools.partial(dot_task, col) for col in range(0, d_model, MXU_COLS)]

    i = pl.program_id(0)
    interior = (i > 0) & (i < n_tiles)

    @pl.when(i == 0)
    def _():
        _interleave(merge_tasks(an_stages[0]), [])

    for parity in range(2):
        @pl.when(interior & (i % 2 == parity))
        def _():
            _interleave(dot_tasks(an_stages[1 - parity]), merge_tasks(an_stages[parity]))

    @pl.when(i == n_tiles)
    def _():
        _interleave(dot_tasks(an_stages[(n_tiles - 1) % 2]), [])


def _outproj(x2, branches, ga, gn, w_out, *, layer, dilations, tm):
    m, d_model = x2.shape
    d_attn = branches[0][0].shape[1]
    n_heads = d_attn // HEAD_DIM
    n_tiles = m // tm
    assert all(d % d_prev == 0 for d, d_prev in zip(dilations, (1,) + tuple(dilations)))
    cur = lambda i: (jnp.minimum(i, n_tiles - 1), 0)
    lag = lambda i: (jnp.maximum(i - 1, 0), 0)
    args, in_specs = [x2], [pl.BlockSpec((tm, d_model), lag)]
    for d, branch in zip((1,) + tuple(dilations), branches):
        args += list(branch)
        in_specs += [pl.BlockSpec((tm // d, d * d_attn), cur),
                     pl.BlockSpec((tm // d, d * LANES), cur)]
    args += [ga, gn, w_out]
    in_specs += [
        pl.BlockSpec((None, 1, d_attn), lambda i: (layer, 0, 0)),
        pl.BlockSpec((tm, gn.shape[1]), lag),
        pl.BlockSpec((None,) + w_out.shape[1:], lambda i: (layer, 0, 0),
                     pipeline_mode=pl.Buffered(1)),
    ]
    return pl.pallas_call(
        functools.partial(_outproj_kernel, dilations=dilations, n_heads=n_heads,
                          n_tiles=n_tiles),
        grid=(n_tiles + 1,),
        in_specs=in_specs,
        out_specs=pl.BlockSpec((tm, d_model), lag),
        out_shape=jax.ShapeDtypeStruct((m, d_model), F32),
        scratch_shapes=(
            [pltpu.VMEM((n_heads, tm, HEAD_DIM), F32) for _ in dilations]
            + [pltpu.VMEM((tm, LANES), F32) for _ in dilations]
            + [pltpu.VMEM((tm, d_attn), BF16)] * 2
            + [pltpu.VMEM((n_heads, tm, HEAD_DIM), F32)] * len(dilations)
            + [pltpu.VMEM((tm, LANES), F32)] * (len(dilations) - 1)),
        compiler_params=pltpu.CompilerParams(
            dimension_semantics=("arbitrary",), vmem_limit_bytes=VMEM_LIMIT),
    )(*args)


def _ffn_kernel(*refs, final_norm):
    if final_norm:
        x_ref, g_ref, wg_ref, wu_ref, wd_ref, fg_ref, o_ref, h_scr = refs
    else:
        x_ref, g_ref, wg_ref, wu_ref, wd_ref, o_ref, h_scr = refs
    f = pl.program_id(1)

    @pl.when(f == 0)
    def _():
        x = x_ref[...]
        h_scr[...] = (x * _rms_scale(x) * g_ref[...]).astype(BF16)
        o_ref[...] = x

    h = h_scr[...]
    gate = jnp.dot(h, wg_ref[...], preferred_element_type=F32)
    up = jnp.dot(h, wu_ref[...], preferred_element_type=F32)
    ff = (gate * (1.0 / (1.0 + jnp.exp(-gate))) * up).astype(BF16)
    o_ref[...] += jnp.dot(ff, wd_ref[...], preferred_element_type=F32)

    if final_norm:
        @pl.when(f == pl.num_programs(1) - 1)
        def _():
            y = o_ref[...]
            o_ref[...] = y * _rms_scale(y) * fg_ref[...]


def _ffn(x2, g, w_gate, w_up, w_down, final_g, *, layer, tm, tf):
    m, d_model = x2.shape
    d_ff = w_gate.shape[2]
    final_norm = final_g is not None
    args = [x2, g, w_gate, w_up, w_down]
    in_specs = [
        pl.BlockSpec((tm, d_model), lambda i, f: (i, 0)),
        pl.BlockSpec((None, 1, d_model), lambda i, f: (layer, 0, 0)),
        pl.BlockSpec((None, d_model, tf), lambda i, f: (layer, 0, f)),
        pl.BlockSpec((None, d_model, tf), lambda i, f: (layer, 0, f)),
        pl.BlockSpec((None, tf, d_model), lambda i, f: (layer, f, 0)),
    ]
    if final_norm:
        args.append(final_g)
        in_specs.append(pl.BlockSpec((1, d_model), lambda i, f: (0, 0)))
    return pl.pallas_call(
        functools.partial(_ffn_kernel, final_norm=final_norm),
        grid=(m // tm, d_ff // tf),
        in_specs=in_specs,
        out_specs=pl.BlockSpec((tm, d_model), lambda i, f: (i, 0)),
        out_shape=jax.ShapeDtypeStruct((m, d_model), F32),
        scratch_shapes=[pltpu.VMEM((tm, d_model), BF16)],
        input_output_aliases={0: 0},
        compiler_params=pltpu.CompilerParams(
            dimension_semantics=("arbitrary", "arbitrary"), vmem_limit_bytes=VMEM_LIMIT),
    )(*args)


def _rope_tables(seq):
    pos = jnp.arange(seq, dtype=F32)
    inv = ROPE_THETA ** (-jnp.arange(0, HEAD_DIM, 2, dtype=F32) / HEAD_DIM)
    ang = pos[:, None] * inv[None, :]
    cos, sin = jnp.cos(ang), jnp.sin(ang)
    return jnp.concatenate([cos, cos], axis=-1), jnp.concatenate([-sin, sin], axis=-1)


def kernel(x, norm1_g, w_in, gmlp_ln_g, w_spatial, b_spatial, mix_norm_attn_g,
           mix_norm_gmlp_g, w_out, norm2_g, w_gate, w_up, w_down, final_g):
    b, s, d_model = x.shape
    depth = w_in.shape[0]
    d_gmlp = gmlp_ln_g.shape[1]
    d_attn = mix_norm_attn_g.shape[1]
    n_groups = w_spatial.shape[1]
    d_ff = w_gate.shape[2]
    assert w_in.shape[2] == 3 * d_attn + 2 * d_gmlp and n_groups * GMLP_GROUP == d_gmlp
    assert w_spatial.shape[2] == CHUNK
    assert DILATED_PATTERNS[0][1] == 1
    dilations = tuple(d for _, d in DILATED_PATTERNS[1:])

    tiles = _plan_tiles(s, d_model, d_attn, d_gmlp, d_ff, len(dilations))
    assert s % CHUNK == 0 and tiles.tm_in % CHUNK == 0

    cos2, sin2 = _rope_tables(s)
    x2 = x.reshape(b * s, d_model)
    vec3 = lambda a: a.reshape(a.shape[0], 1, a.shape[1])

    w_in_b, w_out_b = w_in.astype(BF16), w_out.astype(BF16)
    w_gate_b, w_up_b, w_down_b = w_gate.astype(BF16), w_up.astype(BF16), w_down.astype(BF16)
    ws_b = w_spatial.astype(BF16)
    bsb = jnp.repeat(jnp.swapaxes(b_spatial, 1, 2), GMLP_GROUP, axis=2)
    g1, lng, g2, ga, gf = (vec3(a) for a in (norm1_g, gmlp_ln_g, mix_norm_gmlp_g,
                                              mix_norm_attn_g, norm2_g))

    for l in range(depth):
        outs = _inproj(x2, g1, w_in_b, cos2, sin2, lng, ws_b, bsb, g2, layer=l, seq=s,
                       d_attn=d_attn, d_gmlp=d_gmlp, dilations=dilations, tm=tiles.tm_in)
        gn = outs[-1]
        branches = [
            _attn_branch(*outs[3 * i:3 * i + 3], batch=b, window=window, dilation=dilation,
                         tiles_per_step=tiles.attn_tiles)
            for i, (window, dilation) in enumerate(DILATED_PATTERNS)]
        x2 = _outproj(x2, branches, ga, gn, w_out_b, layer=l, dilations=dilations,
                      tm=tiles.tm_out)
        x2 = _ffn(x2, gf, w_gate_b, w_up_b, w_down_b,
                  final_g.reshape(1, -1) if l == depth - 1 else None,
                  layer=l, tm=tiles.tm_ffn, tf=tiles.tf)
    return x2.reshape(b, s, d_model)
```

```python
import functools
import math
from typing import NamedTuple

import jax
import jax.numpy as jnp
from jax import lax
from jax.experimental import pallas as pl
from jax.experimental.pallas import tpu as pltpu

HEAD_DIM = 128
GMLP_GROUP = 128
CHUNK = 128
DILATED_PATTERNS = ((128, 1), (512, 4), (2048, 16))
ROPE_THETA = 10000.0
EPS = 1e-6
NEG = -1e30

LANES = 128
BF16_SUBLANES = 16
MXU_COLS = 256
V7X_VMEM_BYTES = 64 * 1024 * 1024
VMEM_LIMIT = V7X_VMEM_BYTES * 7 // 8
COMPILER_VMEM_BYTES = 4 * 1024 * 1024
Q_TILE = 128

BF16 = jnp.bfloat16
F32 = jnp.float32


class _Tiles(NamedTuple):
    tm_in: int
    tm_out: int
    tm_ffn: int
    tf: int
    attn_tiles: int


def _plan_tiles(seq, d_model, d_attn, d_gmlp, d_ff, n_dil):
    budget = VMEM_LIMIT - COMPILER_VMEM_BYTES
    f32, bf16 = 4, 2
    d_in = 3 * d_attn + 2 * d_gmlp

    def inproj(tm):
        windows = 2 * (tm * d_model * f32 + 3 * (1 + n_dil) * tm * d_attn * bf16
                       + tm * d_gmlp * bf16 + 2 * tm * HEAD_DIM * f32 + CHUNK * d_gmlp * f32)
        stages = 2 * (3 * tm * d_attn + 2 * tm * d_gmlp) * f32
        scratch = (tm * d_model * bf16 + tm * d_gmlp * (bf16 + f32)
                   + 3 * (n_dil - 1) * tm * d_attn * f32)
        return d_model * d_in * bf16 + windows + stages + scratch

    def outproj(tm):
        windows = 2 * (2 * tm * d_model * f32 + (1 + n_dil) * tm * (d_attn * bf16 + LANES * f32)
                       + tm * d_gmlp * bf16)
        scratch = ((2 * n_dil) * tm * d_attn * f32 + (2 * n_dil - 1) * tm * LANES * f32
                   + 2 * tm * d_attn * bf16)
        return d_model * d_model * bf16 + windows + scratch

    def ffn(tm, tf):
        return 2 * (2 * tm * d_model * f32 + 3 * d_model * tf * bf16) + tm * d_model * bf16

    def attn(tiles):
        rows = tiles * Q_TILE
        return 2 * (4 * rows * d_attn * bf16 + 4 * (Q_TILE // 2) * d_attn * bf16
                    + rows * LANES * f32)

    def largest(candidates, cost):
        return next(c for c in candidates if cost(c) <= budget)

    row_tiles = [t for t in (2048, 1024, 512, 256, 128) if seq % t == 0]
    tf = largest([t for t in (1024, 512, 256, 128) if d_ff % t == 0], lambda t: ffn(128, t))
    return _Tiles(
        tm_in=largest(row_tiles, inproj), tm_out=largest(row_tiles, outproj),
        tm_ffn=largest(row_tiles, lambda t: ffn(t, tf)), tf=tf,
        attn_tiles=largest((32, 16, 8, 4, 2, 1), attn))


def _gelu(x):
    return 0.5 * x * (1.0 + lax.erf(x * (1.0 / math.sqrt(2.0))))


def _rms_scale(x):
    return lax.rsqrt(jnp.mean(x * x, axis=-1, keepdims=True) + EPS)


def _head_cols(hh):
    return slice(hh * HEAD_DIM, (hh + 1) * HEAD_DIM)


def _interleave(major, minor):
    done = 0
    for j, task in enumerate(major):
        task()
        upto = (j + 1) * len(minor) // len(major)
        for other in minor[done:upto]:
            other()
        done = upto


def _inproj_kernel(*refs, d_attn, d_gmlp, q_scale, dilations, n_tiles):
    (x_ref, g1_ref, w_ref, cos_ref, sin_ref, lng_ref, ws_ref, bs_ref, g2_ref) = refs[:9]
    n_res_out = 3 * len(dilations)
    nat_refs = refs[9:12]
    res_refs = refs[12:12 + n_res_out]
    gn_ref = refs[12 + n_res_out]
    scratch = refs[13 + n_res_out:]
    stages = (scratch[0:4], scratch[4:8])
    h_scr, vln_scr, gate_scr = scratch[8:11]
    mids = [scratch[11 + 3 * level:14 + 3 * level] for level in range(len(dilations) - 1)]

    tm = x_ref.shape[0]
    n_heads = d_attn // HEAD_DIM
    n_groups = d_gmlp // GMLP_GROUP
    heads_per_dot = MXU_COLS // HEAD_DIM

    def rope(t):
        return t * cos_ref[...] + pltpu.roll(t, HEAD_DIM // 2, axis=1) * sin_ref[...]

    head_fns = (lambda t: rope(t) * q_scale, rope, lambda t: t)

    def project_tasks(stage):
        slabs, uv_scr = stage[:3], stage[3]

        def norm():
            x = x_ref[...]
            h_scr[...] = (x * _rms_scale(x) * g1_ref[...]).astype(BF16)

        def project(col0):
            return jnp.dot(h_scr[...], w_ref[:, col0:col0 + MXU_COLS], preferred_element_type=F32)

        def uv_task(col):
            uv_scr[:, col:col + MXU_COLS] = project(3 * d_attn + col)

        def qkv_task(idx, first):
            proj = project(idx * d_attn + first * HEAD_DIM)
            for sub in range(heads_per_dot):
                val = head_fns[idx](proj[:, _head_cols(sub)])
                nat_refs[idx][:, _head_cols(first + sub)] = val.astype(BF16)
                slabs[idx][first + sub] = val

        tasks = [norm]
        tasks += [functools.partial(uv_task, col) for col in range(0, 2 * d_gmlp, MXU_COLS)]
        tasks += [functools.partial(qkv_task, idx, first) for idx in range(3)
                  for first in range(0, n_heads, heads_per_dot)]
        return tasks

    def finish_tasks(stage):
        slabs, uv_scr = stage[:3], stage[3]

        def copy_task(idx, level, r):
            d = dilations[level]
            d_prev = dilations[level - 1] if level else 1
            n = tm // d
            for hh in range(n_heads):
                if level:
                    start = (r % d_prev) * (tm // d_prev) + r // d_prev
                    val = mids[level - 1][idx][hh, pl.ds(start, n, stride=d // d_prev), :]
                else:
                    val = slabs[idx][hh, pl.ds(r, n, stride=d), :]
                res_refs[3 * level + idx][:, _head_cols(r * n_heads + hh)] = val.astype(BF16)
                if level + 1 < len(dilations):
                    mids[level][idx][hh, r * n:(r + 1) * n, :] = val

        def layernorm_task(rows):
            vv = _gelu(uv_scr[rows, d_gmlp:2 * d_gmlp])
            vc = vv - jnp.mean(vv, axis=-1, keepdims=True)
            vln = vc * lax.rsqrt(jnp.mean(vc * vc, axis=-1, keepdims=True) + EPS) * lng_ref[...]
            vln_scr[rows, :] = vln.astype(BF16)

        def mix_task(rows, g):
            cols = slice(g * GMLP_GROUP, (g + 1) * GMLP_GROUP)
            mixed = jnp.dot(ws_ref[g], vln_scr[rows, cols], preferred_element_type=F32)
            u = _gelu(uv_scr[rows, cols])
            gate_scr[rows, cols] = u * (mixed + bs_ref[:, cols])

        def gate_norm_task(rows):
            gate = gate_scr[rows, :]
            gn_ref[rows, :] = (gate * _rms_scale(gate) * g2_ref[...]).astype(BF16)

        tasks = []
        for c in range(tm // CHUNK):
            rows = slice(c * CHUNK, (c + 1) * CHUNK)
            tasks.append(functools.partial(layernorm_task, rows))
            tasks += [functools.partial(mix_task, rows, g) for g in range(n_groups)]
            tasks.append(functools.partial(gate_norm_task, rows))
        tasks += [functools.partial(copy_task, idx, level, r) for level, d in enumerate(dilations)
                  for idx in range(3) for r in range(d)]
        return tasks

    i = pl.program_id(0)
    interior = (i > 0) & (i < n_tiles)

    @pl.when(i == 0)
    def _():
        _interleave(project_tasks(stages[0]), [])

    for parity in range(2):
        @pl.when(interior & (i % 2 == parity))
        def _():
            _interleave(project_tasks(stages[parity]), finish_tasks(stages[1 - parity]))

    @pl.when(i == n_tiles)
    def _():
        _interleave(finish_tasks(stages[(n_tiles - 1) % 2]), [])


def _inproj(x2, g1, w_in, cos2, sin2, lng, ws, bsb, g2, *, layer, seq, d_attn, d_gmlp,
            dilations, tm):
    m, d_model = x2.shape
    d_in = w_in.shape[2]
    n_groups = d_gmlp // GMLP_GROUP
    n_heads = d_attn // HEAD_DIM
    pos_blocks = seq // tm
    n_tiles = m // tm
    assert all(tm % (d * BF16_SUBLANES) == 0 for d in dilations)
    assert all(d % d_prev == 0 for d, d_prev in zip(dilations, (1,) + tuple(dilations)))
    vec = lambda i: (layer, 0, 0)
    cur = lambda i: (jnp.minimum(i, n_tiles - 1), 0)
    lag = lambda i: (jnp.maximum(i - 1, 0), 0)
    pos = lambda i: (jnp.minimum(i, n_tiles - 1) % pos_blocks, 0)
    kern = functools.partial(_inproj_kernel, d_attn=d_attn, d_gmlp=d_gmlp,
                             q_scale=HEAD_DIM ** -0.5 * math.log2(math.e), dilations=dilations,
                             n_tiles=n_tiles)
    qkv_specs = [pl.BlockSpec((tm, d_attn), cur)] * 3
    qkv_shapes = [jax.ShapeDtypeStruct((m, d_attn), BF16)] * 3
    for d in dilations:
        qkv_specs += [pl.BlockSpec((tm // d, d * d_attn), lag)] * 3
        qkv_shapes += [jax.ShapeDtypeStruct((m // d, d * d_attn), BF16)] * 3
    stage = [pltpu.VMEM((n_heads, tm, HEAD_DIM), F32)] * 3 + [pltpu.VMEM((tm, 2 * d_gmlp), F32)]
    return pl.pallas_call(
        kern,
        grid=(n_tiles + 1,),
        in_specs=[
            pl.BlockSpec((tm, d_model), cur),
            pl.BlockSpec((None, 1, d_model), vec),
            pl.BlockSpec((None, d_model, d_in), vec, pipeline_mode=pl.Buffered(1)),
            pl.BlockSpec((tm, HEAD_DIM), pos),
            pl.BlockSpec((tm, HEAD_DIM), pos),
            pl.BlockSpec((None, 1, d_gmlp), vec),
            pl.BlockSpec((None, n_groups, CHUNK, CHUNK), lambda i: (layer, 0, 0, 0)),
            pl.BlockSpec((None, CHUNK, d_gmlp), vec),
            pl.BlockSpec((None, 1, d_gmlp), vec),
        ],
        out_specs=qkv_specs + [pl.BlockSpec((tm, d_gmlp), lag)],
        out_shape=qkv_shapes + [jax.ShapeDtypeStruct((m, d_gmlp), BF16)],
        scratch_shapes=stage * 2 + [
            pltpu.VMEM((tm, d_model), BF16),
            pltpu.VMEM((tm, d_gmlp), BF16),
            pltpu.VMEM((tm, d_gmlp), F32),
        ] + [pltpu.VMEM((n_heads, tm, HEAD_DIM), F32)] * (3 * (len(dilations) - 1)),
        compiler_params=pltpu.CompilerParams(
            dimension_semantics=("arbitrary",), vmem_limit_bytes=VMEM_LIMIT),
    )(x2, g1, w_in, cos2, sin2, lng, ws, bsb, g2)


def _attn_kernel(*refs, n_side, n_heads, has_halo):
    if has_halo:
        q_ref, kp_ref, k_ref, kn_ref, vp_ref, v_ref, vn_ref, o_ref, s_ref = refs
    else:
        q_ref, k_ref, v_ref, o_ref, s_ref = refs
        kp_ref = kn_ref = vp_ref = vn_ref = None
    tl = q_ref.shape[1]
    halo = n_side
    n_tiles = tl // Q_TILE
    t = pl.program_id(2)
    first_block = t == 0
    last_block = t == pl.num_programs(2) - 1

    kw = Q_TILE + 2 * halo
    row_i = lax.broadcasted_iota(jnp.int32, (Q_TILE, kw), 0)
    col_j = lax.broadcasted_iota(jnp.int32, (Q_TILE, kw), 1)
    rel = col_j - row_i - halo
    band = (rel >= -n_side) & (rel <= n_side)
    head_ok = (col_j >= halo) | jnp.logical_not(first_block)
    tail_ok = (col_j < halo + Q_TILE) | jnp.logical_not(last_block)
    lane = lax.broadcasted_iota(jnp.int32, (Q_TILE, LANES), 1)
    ones = jnp.ones((kw, HEAD_DIM), BF16)

    def window(prev_ref, main_ref, next_ref, i, cols):
        lo, hi = i * Q_TILE - halo, (i + 1) * Q_TILE + halo
        parts = []
        if lo < 0:
            parts.append(prev_ref[0, :, cols] if has_halo else main_ref[0, 0:halo, cols])
        parts.append(main_ref[0, max(lo, 0):min(hi, tl), cols])
        if hi > tl:
            parts.append(next_ref[0, :, cols] if has_halo else main_ref[0, tl - halo:tl, cols])
        return parts[0] if len(parts) == 1 else jnp.concatenate(parts, axis=0)

    biases = {}
    for i in range(n_tiles):
        mask = band
        if i == 0:
            mask = mask & head_ok
        if i == n_tiles - 1:
            mask = mask & tail_ok
        biases[i] = jnp.where(mask, 0.0, NEG)

    n_res = q_ref.shape[2] // (n_heads * HEAD_DIM)
    for res, i in ((res, i) for res in range(n_res) for i in range(n_tiles)):
        rows = slice(i * Q_TILE, (i + 1) * Q_TILE)
        bias = biases[i]
        stats = jnp.ones((Q_TILE, LANES), F32)
        for hh in range(n_heads):
            cols = _head_cols(res * n_heads + hh)
            kh = window(kp_ref, k_ref, kn_ref, i, cols)
            vh = window(vp_ref, v_ref, vn_ref, i, cols)
            s = lax.dot_general(q_ref[0, rows, cols], kh, (((1,), (1,)), ((), ())),
                                preferred_element_type=F32) + bias
            mx = jnp.max(s, axis=-1, keepdims=True)
            p = jnp.exp2(s - mx).astype(BF16)
            ov = jnp.dot(p, jnp.concatenate([vh, ones], axis=1), preferred_element_type=F32)
            o_ref[0, rows, cols] = ov[:, :HEAD_DIM].astype(o_ref.dtype)
            stats = jnp.where(lane == hh, mx, stats)
            stats = jnp.where(lane == n_heads + hh, ov[:, HEAD_DIM:], stats)
        s_ref[0, rows, res * LANES:(res + 1) * LANES] = stats


def _attn_branch(q, k, v, *, batch, window, dilation, tiles_per_step):
    rows_total, width = q.shape
    c = width // dilation
    sub_len = rows_total // batch
    n_heads = c // HEAD_DIM
    n_side = window // (2 * dilation)
    halo = n_side
    assert halo * 2 == Q_TILE and 2 * n_heads <= LANES and sub_len % Q_TILE == 0
    tl = min(sub_len, tiles_per_step * Q_TILE)
    assert sub_len % tl == 0 and tl % halo == 0
    hb = tl // halo
    n_hb = sub_len // halo
    n_res = min(dilation, max(1, tiles_per_step * Q_TILE // tl))
    assert dilation % n_res == 0

    view = lambda a: a.reshape(batch, sub_len, a.shape[-1])
    qv, kv, vv = view(q), view(k), view(v)

    main = lambda bi, r, t: (bi, t, r)
    prev_halo = lambda bi, r, t: (bi, jnp.maximum(t * hb - 1, 0), r)
    next_halo = lambda bi, r, t: (bi, jnp.minimum((t + 1) * hb, n_hb - 1), r)
    big = pl.BlockSpec((1, tl, n_res * c), main)
    small_p = pl.BlockSpec((1, halo, n_res * c), prev_halo)
    small_n = pl.BlockSpec((1, halo, n_res * c), next_halo)
    stat_spec = pl.BlockSpec((1, tl, n_res * LANES), main)
    stat_shape = jax.ShapeDtypeStruct((batch, sub_len, dilation * LANES), F32)

    has_halo = sub_len > tl
    if has_halo:
        args, in_specs = (qv, kv, kv, kv, vv, vv, vv), [big, small_p, big, small_n] + [
            small_p, big, small_n]
    else:
        args, in_specs = (qv, kv, vv), [big, big, big]
    kern = functools.partial(_attn_kernel, n_side=n_side, n_heads=n_heads, has_halo=has_halo)
    o, stats = pl.pallas_call(
        kern,
        grid=(batch, dilation // n_res, sub_len // tl),
        in_specs=in_specs,
        out_specs=[big, stat_spec],
        out_shape=[jax.ShapeDtypeStruct((batch, sub_len, width), BF16), stat_shape],
        compiler_params=pltpu.CompilerParams(
            dimension_semantics=("arbitrary",) * 3, vmem_limit_bytes=VMEM_LIMIT),
    )(*args)
    flat = lambda a: a.reshape(rows_total, a.shape[-1])
    return flat(o), flat(stats)


def _outproj_kernel(*refs, dilations, n_heads, n_tiles):
    n_pat = 1 + len(dilations)
    n_dil = len(dilations)
    x_ref = refs[0]
    o_refs = refs[1:1 + 2 * n_pat:2]
    s_refs = refs[2:2 + 2 * n_pat:2]
    ga_ref, gn_ref, w_ref, out_ref = refs[1 + 2 * n_pat:5 + 2 * n_pat]
    scratch = refs[5 + 2 * n_pat:]
    o_slabs = scratch[0:n_dil]
    s_slabs = scratch[n_dil:2 * n_dil]
    an_stages = scratch[2 * n_dil:2 * n_dil + 2]
    a_slab = scratch[2 * n_dil + 2]
    o_mids = scratch[2 * n_dil + 3:3 * n_dil + 2]
    s_mids = scratch[3 * n_dil + 2:4 * n_dil + 1]

    tm, d_model = x_ref.shape
    d_attn = n_heads * HEAD_DIM
    state = {}

    def unpermute_task(idx, level, r):
        d_hi = dilations[level]
        d_lo = dilations[level - 1] if level else 1
        n = tm // d_hi
        dst = pl.ds((r % d_lo) * (tm // d_lo) + r // d_lo, n, stride=d_hi // d_lo)
        o_dst, s_dst = (o_mids[level - 1], s_mids[level - 1]) if level else (o_slabs[idx], s_slabs[idx])
        if level == idx:
            s_dst[dst, :] = s_refs[idx + 1][:, r * LANES:(r + 1) * LANES]
            for hh in range(n_heads):
                o_dst[hh, dst, :] = o_refs[idx + 1][:, _head_cols(r * n_heads + hh)].astype(F32)
        else:
            s_dst[dst, :] = s_mids[level][r * n:(r + 1) * n, :]
            for hh in range(n_heads):
                o_dst[hh, dst, :] = o_mids[level][hh, r * n:(r + 1) * n, :]

    def weights_task():
        mxs = [s_refs[0][...]] + [s_slab[...] for s_slab in s_slabs]
        dens = [pltpu.roll(stats, LANES - n_heads, axis=1) for stats in mxs]
        top = functools.reduce(jnp.maximum, mxs)
        es = [jnp.exp2(mx - top) for mx in mxs]
        inv = 1.0 / functools.reduce(jnp.add, [e * den for e, den in zip(es, dens)])
        state["ws"] = [e * inv for e in es]
        state["ssq"] = jnp.zeros((tm, 1), F32)

    def combine_task(hh):
        parts = [o_refs[0][:, _head_cols(hh)].astype(F32)] + [o_slab[hh] for o_slab in o_slabs]
        a = functools.reduce(jnp.add, [w[:, hh:hh + 1] * p for w, p in zip(state["ws"], parts)])
        state["ssq"] = state["ssq"] + jnp.sum(a * a, axis=-1, keepdims=True)
        a_slab[hh] = a

    def norm_task(an_scr, hh):
        if hh == 0:
            state["scale"] = lax.rsqrt(state["ssq"] / d_attn + EPS)
        an_scr[:, _head_cols(hh)] = (
            a_slab[hh] * state["scale"] * ga_ref[:, _head_cols(hh)]).astype(BF16)

    def merge_tasks(an_scr):
        state.clear()
        tasks = [functools.partial(unpermute_task, idx, level, r) for idx in range(n_dil)
                 for level in range(idx, -1, -1) for r in range(dilations[level])]
        tasks.append(weights_task)
        tasks += [functools.partial(combine_task, hh) for hh in range(n_heads)]
        tasks += [functools.partial(norm_task, an_scr, hh) for hh in range(n_heads)]
        return tasks

    def dot_tasks(an_scr):
        def dot_task(col):
            cols = slice(col, col + MXU_COLS)
            acc = jnp.dot(gn_ref[...], w_ref[d_attn:, cols], preferred_element_type=F32)
            acc = acc + jnp.dot(an_scr[...], w_ref[0:d_attn, cols], preferred_element_type=F32)
            out_ref[:, cols] = x_ref[:, cols] + acc

        return [functools.partial(dot_task, col) for col in range(0, d_model, MXU_COLS)]

    i = pl.program_id(0)
    interior = (i > 0) & (i < n_tiles)

    @pl.when(i == 0)
    def _():
        _interleave(merge_tasks(an_stages[0]), [])

    for parity in range(2):
        @pl.when(interior & (i % 2 == parity))
        def _():
            _interleave(dot_tasks(an_stages[1 - parity]), merge_tasks(an_stages[parity]))

    @pl.when(i == n_tiles)
    def _():
        _interleave(dot_tasks(an_stages[(n_tiles - 1) % 2]), [])


def _outproj(x2, branches, ga, gn, w_out, *, layer, dilations, tm):
    m, d_model = x2.shape
    d_attn = branches[0][0].shape[1]
    n_heads = d_attn // HEAD_DIM
    n_tiles = m // tm
    assert all(d % d_prev == 0 for d, d_prev in zip(dilations, (1,) + tuple(dilations)))
    cur = lambda i: (jnp.minimum(i, n_tiles - 1), 0)
    lag = lambda i: (jnp.maximum(i - 1, 0), 0)
    args, in_specs = [x2], [pl.BlockSpec((tm, d_model), lag)]
    for d, branch in zip((1,) + tuple(dilations), branches):
        args += list(branch)
        in_specs += [pl.BlockSpec((tm // d, d * d_attn), cur),
                     pl.BlockSpec((tm // d, d * LANES), cur)]
    args += [ga, gn, w_out]
    in_specs += [
        pl.BlockSpec((None, 1, d_attn), lambda i: (layer, 0, 0)),
        pl.BlockSpec((tm, gn.shape[1]), lag),
        pl.BlockSpec((None,) + w_out.shape[1:], lambda i: (layer, 0, 0),
                     pipeline_mode=pl.Buffered(1)),
    ]
    return pl.pallas_call(
        functools.partial(_outproj_kernel, dilations=dilations, n_heads=n_heads,
                          n_tiles=n_tiles),
        grid=(n_tiles + 1,),
        in_specs=in_specs,
        out_specs=pl.BlockSpec((tm, d_model), lag),
        out_shape=jax.ShapeDtypeStruct((m, d_model), F32),
        scratch_shapes=(
            [pltpu.VMEM((n_heads, tm, HEAD_DIM), F32) for _ in dilations]
            + [pltpu.VMEM((tm, LANES), F32) for _ in dilations]
            + [pltpu.VMEM((tm, d_attn), BF16)] * 2
            + [pltpu.VMEM((n_heads, tm, HEAD_DIM), F32)] * len(dilations)
            + [pltpu.VMEM((tm, LANES), F32)] * (len(dilations) - 1)),
        compiler_params=pltpu.CompilerParams(
            dimension_semantics=("arbitrary",), vmem_limit_bytes=VMEM_LIMIT),
    )(*args)


def _ffn_kernel(*refs, final_norm):
    if final_norm:
        x_ref, g_ref, wgu_ref, wd_ref, fg_ref, o_ref, h_scr = refs
    else:
        x_ref, g_ref, wgu_ref, wd_ref, o_ref, h_scr = refs
    f = pl.program_id(1)
    tf = wd_ref.shape[0]

    @pl.when(f == 0)
    def _():
        x = x_ref[...]
        h_scr[...] = (x * _rms_scale(x) * g_ref[...]).astype(BF16)
        o_ref[...] = x

    h = h_scr[...]
    gate_up = jnp.dot(h, wgu_ref[...], preferred_element_type=F32)
    gate, up = gate_up[:, :tf], gate_up[:, tf:]
    ff = (gate * (1.0 / (1.0 + jnp.exp(-gate))) * up).astype(BF16)
    o_ref[...] += jnp.dot(ff, wd_ref[...], preferred_element_type=F32)

    if final_norm:
        @pl.when(f == pl.num_programs(1) - 1)
        def _():
            y = o_ref[...]
            o_ref[...] = y * _rms_scale(y) * fg_ref[...]


def _ffn(x2, g, w_gate_up, w_down, final_g, *, layer, tm, tf):
    m, d_model = x2.shape
    d_ff = w_down.shape[1]
    final_norm = final_g is not None
    args = [x2, g, w_gate_up, w_down]
    in_specs = [
        pl.BlockSpec((tm, d_model), lambda i, f: (i, 0)),
        pl.BlockSpec((None, 1, d_model), lambda i, f: (layer, 0, 0)),
        pl.BlockSpec((None, d_model, 2 * tf), lambda i, f: (layer, 0, f)),
        pl.BlockSpec((None, tf, d_model), lambda i, f: (layer, f, 0)),
    ]
    if final_norm:
        args.append(final_g)
        in_specs.append(pl.BlockSpec((1, d_model), lambda i, f: (0, 0)))
    return pl.pallas_call(
        functools.partial(_ffn_kernel, final_norm=final_norm),
        grid=(m // tm, d_ff // tf),
        in_specs=in_specs,
        out_specs=pl.BlockSpec((tm, d_model), lambda i, f: (i, 0)),
        out_shape=jax.ShapeDtypeStruct((m, d_model), F32),
        scratch_shapes=[pltpu.VMEM((tm, d_model), BF16)],
        input_output_aliases={0: 0},
        compiler_params=pltpu.CompilerParams(
            dimension_semantics=("arbitrary", "arbitrary"), vmem_limit_bytes=VMEM_LIMIT),
    )(*args)


def _rope_tables(seq):
    pos = jnp.arange(seq, dtype=F32)
    inv = ROPE_THETA ** (-jnp.arange(0, HEAD_DIM, 2, dtype=F32) / HEAD_DIM)
    ang = pos[:, None] * inv[None, :]
    cos, sin = jnp.cos(ang), jnp.sin(ang)
    return jnp.concatenate([cos, cos], axis=-1), jnp.concatenate([-sin, sin], axis=-1)


def kernel(x, norm1_g, w_in, gmlp_ln_g, w_spatial, b_spatial, mix_norm_attn_g,
           mix_norm_gmlp_g, w_out, norm2_g, w_gate, w_up, w_down, final_g):
    b, s, d_model = x.shape
    depth = w_in.shape[0]
    d_gmlp = gmlp_ln_g.shape[1]
    d_attn = mix_norm_attn_g.shape[1]
    n_groups = w_spatial.shape[1]
    d_ff = w_gate.shape[2]
    assert w_in.shape[2] == 3 * d_attn + 2 * d_gmlp and n_groups * GMLP_GROUP == d_gmlp
    assert w_spatial.shape[2] == CHUNK
    assert DILATED_PATTERNS[0][1] == 1
    dilations = tuple(d for _, d in DILATED_PATTERNS[1:])

    tiles = _plan_tiles(s, d_model, d_attn, d_gmlp, d_ff, len(dilations))
    assert s % CHUNK == 0 and tiles.tm_in % CHUNK == 0

    cos2, sin2 = _rope_tables(s)
    x2 = x.reshape(b * s, d_model)
    vec3 = lambda a: a.reshape(a.shape[0], 1, a.shape[1])

    w_in_b, w_out_b = w_in.astype(BF16), w_out.astype(BF16)
    chunked = lambda w: w.reshape(depth, d_model, d_ff // tiles.tf, 1, tiles.tf)
    w_gate_up_b = jnp.concatenate([chunked(w_gate), chunked(w_up)], axis=3).reshape(
        depth, d_model, 2 * d_ff).astype(BF16)
    w_down_b = w_down.astype(BF16)
    ws_b = w_spatial.astype(BF16)
    bsb = jnp.repeat(jnp.swapaxes(b_spatial, 1, 2), GMLP_GROUP, axis=2)
    g1, lng, g2, ga, gf = (vec3(a) for a in (norm1_g, gmlp_ln_g, mix_norm_gmlp_g,
                                              mix_norm_attn_g, norm2_g))

    for l in range(depth):
        outs = _inproj(x2, g1, w_in_b, cos2, sin2, lng, ws_b, bsb, g2, layer=l, seq=s,
                       d_attn=d_attn, d_gmlp=d_gmlp, dilations=dilations, tm=tiles.tm_in)
        gn = outs[-1]
        branches = [
            _attn_branch(*outs[3 * i:3 * i + 3], batch=b, window=window, dilation=dilation,
                         tiles_per_step=tiles.attn_tiles)
            for i, (window, dilation) in enumerate(DILATED_PATTERNS)]
        x2 = _outproj(x2, branches, ga, gn, w_out_b, layer=l, dilations=dilations,
                      tm=tiles.tm_out)
        x2 = _ffn(x2, gf, w_gate_up_b, w_down_b,
                  final_g.reshape(1, -1) if l == depth - 1 else None,
                  layer=l, tm=tiles.tm_ffn, tf=tiles.tf)
    return x2.reshape(b, s, d_model)
```
